```python
import math
import jax
import jax.numpy as jnp
from jax import lax
import numpy as np

D_MODEL = 1024
BATCH = 16
SEQ = 256
DEPTH = 2
DEC_BATCH = 4
DEC_SEQ = 4096
PAST_LEN = 256

GRID_W = 64
N_EVEN = (DEPTH + 1) // 2
N_ODD = DEPTH // 2
ADA_CHUNKS = 6
EPS = 1e-6
D_S5 = D_MODEL // 2
S5_GROUP_CH = 16
S5_GROUPS = D_S5 // S5_GROUP_CH
S5_STATE = 64
S5_MIN_DECAY = 1e-4
S5_DT_MIN = 1e-3
S5_DT_MAX = 1e-1
D_NA = D_MODEL // 2
NA_HEAD_DIM = 64
NA_HEADS = D_NA // NA_HEAD_DIM
NA_WIN_ROWS = 8
NA_WIN_COLS = 16
D_INNER = 2 * D_MODEL
SSD_HEAD_DIM = 64
SSD_HEADS = D_INNER // SSD_HEAD_DIM
SSD_GROUPS = 4
SSD_HEADS_PER_GROUP = SSD_HEADS // SSD_GROUPS
SSD_STATE = 128
SSD_CHUNK = 128
SSD_CONV = 5
SSD_DT_MIN = 1e-3
SSD_DT_MAX = 1e-1
SSD_CONV_DIM = D_INNER + 2 * SSD_GROUPS * SSD_STATE
SSD_IN_DIM = D_INNER + SSD_CONV_DIM + 2 * SSD_HEADS
D_FF = -(-(8 * D_MODEL) // (3 * 256)) * 256

kernel_name = 'hybrid_s5_natten_ssd_prefix_diffusion_step'


def _rmsnorm(x, g):
    x32 = x.astype(jnp.float32)
    y = x32 * lax.rsqrt(jnp.mean(x32 * x32, axis=-1, keepdims=True) + EPS)
    return (y * g.astype(jnp.float32)).astype(x.dtype)


def _adaln(cond, w, b):
    m = jax.nn.silu(cond) @ w + b
    return jnp.split(m[:, None, :], ADA_CHUNKS, axis=-1)


def _modulate(h, shift, scale):
    return h * (1 + scale) + shift


def _swiglu(h, w_gate, w_up, w_down):
    return (jax.nn.silu(h @ w_gate) * (h @ w_up)) @ w_down


def _linear_recurrence_op(left, right):
    a_l, b_l = left
    a_r, b_r = right
    return a_r * a_l, a_r * b_l + b_r


def _s5_mixer(u, a_re, a_im, log_dt, b_re, b_im, c_re, c_im, d_skip, w_glu, h0):
    bsz, seq, _ = u.shape
    u32 = u.astype(jnp.float32)
    uc = u32.reshape(bsz, seq, S5_GROUPS, S5_GROUP_CH).astype(jnp.complex64)
    a = lax.complex(jnp.minimum(a_re.astype(jnp.float32), -S5_MIN_DECAY), a_im.astype(jnp.float32))
    dt = jnp.exp(log_dt.astype(jnp.float32))[..., None]
    a_bar = jnp.exp(a * dt)
    b_bar = ((a_bar - 1) / a)[..., None] * lax.complex(b_re.astype(jnp.float32), b_im.astype(jnp.float32))
    c_mat = lax.complex(c_re.astype(jnp.float32), c_im.astype(jnp.float32))
    y = u32 * d_skip.astype(jnp.float32)
    finals = []
    for direction, reverse in ((0, False), (1, True)):
        first = seq - 1 if reverse else 0
        bu = jnp.einsum('gnp,blgp->blgn', b_bar[direction], uc)
        bu = bu.at[:, first].add(a_bar[direction] * h0[:, direction])
        a_seq = jnp.broadcast_to(a_bar[direction], bu.shape)
        _, h = lax.associative_scan(_linear_recurrence_op, (a_seq, bu), axis=1, reverse=reverse)
        y = y + jnp.einsum('gpn,blgn->blgp', c_mat[direction], h).real.reshape(bsz, seq, D_S5)
        finals.append(h[:, seq - 1 - first])
    y = jax.nn.gelu(y)
    y = y * jax.nn.sigmoid(y @ w_glu.astype(jnp.float32))
    return y.astype(u.dtype), jnp.stack(finals, axis=1)


def _context_attention(q, k, v):
    s = jnp.einsum('bhqd,bhkd->bhqk', q * q.shape[-1] ** -0.5, k).astype(jnp.float32)
    p = jax.nn.softmax(s, axis=-1).astype(v.dtype)
    return jnp.einsum('bhqk,bhkd->bhqd', p, v)


def _neighbourhood_attention(q, k, v, ctx_k, ctx_v, rpb):
    bsz, nh, seq, hd = q.shape
    rows = seq // GRID_W
    kh = min(NA_WIN_ROWS, rows)
    kw = NA_WIN_COLS
    qg = q.reshape(bsz, nh, rows, GRID_W, hd) * hd ** -0.5
    kg = k.reshape(bsz, nh, rows, GRID_W, hd)
    vg = v.reshape(bsz, nh, rows, GRID_W, hd)
    r = jnp.arange(rows)
    w = jnp.arange(GRID_W)
    row_idx = jnp.clip(r - kh // 2, 0, rows - kh)[:, None] + jnp.arange(kh)[None, :]
    k_rows = kg[:, :, row_idx]
    v_rows = vg[:, :, row_idx]
    col_start = jnp.clip(w - kw // 2, 0, GRID_W - kw)
    col_ok = (w[None, :] >= col_start[:, None]) & (w[None, :] < col_start[:, None] + kw)
    dr = row_idx - r[:, None] + (NA_WIN_ROWS - 1)
    dc = jnp.clip(w[None, :] - w[:, None], -(kw - 1), kw - 1) + (kw - 1)
    bias = rpb[:, dr[:, None, :, None], dc[None, :, None, :]].astype(jnp.float32)
    s_loc = jnp.einsum('bhrqd,bhrkwd->bhrqkw', qg, k_rows).astype(jnp.float32) + bias[None]
    s_loc = jnp.where(col_ok[:, None, :], s_loc, -jnp.inf)
    s_ctx = jnp.einsum('bhrqd,bhcd->bhrqc', qg, ctx_k).astype(jnp.float32)
    n_loc = kh * GRID_W
    s = jnp.concatenate([s_loc.reshape(bsz, nh, rows, GRID_W, n_loc), s_ctx], axis=-1)
    p = jax.nn.softmax(s, axis=-1).astype(v.dtype)
    p_loc = p[..., :n_loc].reshape(bsz, nh, rows, GRID_W, kh, GRID_W)
    out = (jnp.einsum('bhrqkw,bhrkwd->bhrqd', p_loc, v_rows)
           + jnp.einsum('bhrqc,bhcd->bhrqd', p[..., n_loc:], ctx_v))
    return out.reshape(bsz, nh, seq, hd)


def _even_mixer(h, w_in, w_out, s5_params, rpb, s5_h0, ctx_k, ctx_v):
    bsz, seq, _ = h.shape
    proj = h @ w_in
    y_s5, s5_final = _s5_mixer(proj[..., :D_S5], *s5_params, s5_h0)
    q, k, v = (proj[..., D_S5 + i * D_NA:D_S5 + (i + 1) * D_NA]
               .reshape(bsz, seq, NA_HEADS, NA_HEAD_DIM).transpose(0, 2, 1, 3) for i in range(3))
    if ctx_k is None:
        attn = _context_attention(q, k, v)
        ctx_kv = (k, v)
    else:
        attn = _neighbourhood_attention(q, k, v, ctx_k, ctx_v, rpb)
        ctx_kv = None
    attn = attn.transpose(0, 2, 1, 3).reshape(bsz, seq, D_NA)
    out = jnp.concatenate([y_s5, attn], axis=-1) @ w_out
    return out, ctx_kv, s5_final


def _depthwise_conv(x, w, b):
    width = w.shape[0]
    out = lax.conv_general_dilated(x, w[:, None, :].astype(x.dtype), window_strides=(1,),
                                   padding=[(width // 2, width // 2)],
                                   dimension_numbers=('NWC', 'WIO', 'NWC'),
                                   feature_group_count=x.shape[-1])
    return out + b


def _decay_matrix(cs):
    n = cs.shape[-1]
    diff = cs[..., :, None] - cs[..., None, :]
    tril = jnp.tril(jnp.ones((n, n), dtype=bool))
    return jnp.exp(jnp.where(tril, diff, -jnp.inf))


def _ssd_scan(x, dt, a, bmat, cmat, h0):
    bsz, seq = x.shape[:2]
    q = min(SSD_CHUNK, seq)
    nc = seq // q
    g, j, p, n = SSD_GROUPS, SSD_HEADS_PER_GROUP, SSD_HEAD_DIM, SSD_STATE
    xg = (x * dt[..., None]).reshape(bsz, nc, q, g, j, p)
    a_cs = jnp.cumsum((dt * a).reshape(bsz, nc, q, g, j).transpose(0, 3, 4, 1, 2), axis=-1)
    bc = bmat.reshape(bsz, nc, q, g, n)
    cc = cmat.reshape(bsz, nc, q, g, n)
    cb = jnp.einsum('bclgn,bcsgn->bgcls', cc, bc)
    y_diag = jnp.einsum('bgjcls,bcsgjp->bclgjp', _decay_matrix(a_cs) * cb[:, :, None], xg)
    decay_to_end = jnp.exp(a_cs[..., -1:] - a_cs)
    chunk_states = jnp.einsum('bclgn,bgjcl,bclgjp->bcgjpn', bc, decay_to_end, xg)
    chunk_states = jnp.concatenate([h0.reshape(bsz, 1, g, j, p, n), chunk_states], axis=1)
    chunk_sum = jnp.pad(a_cs[..., -1], ((0, 0), (0, 0), (0, 0), (1, 0)))
    states = jnp.einsum('bgjzc,bcgjpn->bzgjpn', _decay_matrix(jnp.cumsum(chunk_sum, axis=-1)), chunk_states)
    y_off = jnp.einsum('bclgn,bcgjpn,bgjcl->bclgjp', cc, states[:, :-1], jnp.exp(a_cs))
    y = (y_diag + y_off).reshape(bsz, seq, SSD_HEADS, p)
    return y, states[:, -1].reshape(bsz, SSD_HEADS, p, n)


def _odd_mixer(h, w_in, conv_w, conv_b, a_log, dt_bias, d_skip, norm_g, w_out, h0):
    bsz, seq, _ = h.shape
    proj = h @ w_in
    z = proj[..., :D_INNER]
    xbc = proj[..., D_INNER:D_INNER + SSD_CONV_DIM]
    dt_raw = proj[..., D_INNER + SSD_CONV_DIM:]
    xbc = jax.nn.silu(_depthwise_conv(xbc, conv_w, conv_b)).astype(jnp.float32)
    gn = SSD_GROUPS * SSD_STATE
    xs = xbc[..., :D_INNER].reshape(bsz, seq, SSD_HEADS, SSD_HEAD_DIM)
    bm = xbc[..., D_INNER:D_INNER + gn].reshape(bsz, seq, SSD_GROUPS, SSD_STATE)
    cm = xbc[..., D_INNER + gn:].reshape(bsz, seq, SSD_GROUPS, SSD_STATE)
    dt = jax.nn.softplus(dt_raw.astype(jnp.float32).reshape(bsz, seq, 2, SSD_HEADS) + dt_bias.astype(jnp.float32))
    a = -jnp.exp(a_log.astype(jnp.float32))
    y_f, fin_f = _ssd_scan(xs, dt[:, :, 0], a[0], bm, cm, h0[:, 0])
    y_b, fin_b = _ssd_scan(jnp.flip(xs, 1), jnp.flip(dt[:, :, 1], 1), a[1], jnp.flip(bm, 1), jnp.flip(cm, 1), h0[:, 1])
    y = y_f + jnp.flip(y_b, 1) + d_skip.astype(jnp.float32)[:, None] * xs
    y = y.reshape(bsz, seq, D_INNER) * jax.nn.silu(z.astype(jnp.float32))
    y = _rmsnorm(y, norm_g).astype(h.dtype)
    return y @ w_out, jnp.stack([fin_f, fin_b], axis=1)


def setup_inputs(seed: int = 0) -> dict:
    key = jax.random.key(seed)
    ks = iter(jax.random.split(key, 48))

    def nrm(shape, scale=1.0):
        return scale * jax.random.normal(next(ks), shape, jnp.float32)

    d_ev_in = D_S5 + 3 * D_NA
    d_ev_out = D_S5 + D_NA
    n_idx = jnp.arange(S5_STATE, dtype=jnp.float32)
    ssd_dt = jnp.exp(jax.random.uniform(next(ks), (N_ODD, 2, SSD_HEADS), jnp.float32,
                                        math.log(SSD_DT_MIN), math.log(SSD_DT_MAX)))
    return {
        'x_prompt': nrm((BATCH, SEQ, D_MODEL)),
        'x_sample': nrm((DEC_BATCH, DEC_SEQ, D_MODEL)),
        'cache_na_k': nrm((DEC_BATCH, N_EVEN, NA_HEADS, PAST_LEN, NA_HEAD_DIM)),
        'cache_na_v': nrm((DEC_BATCH, N_EVEN, NA_HEADS, PAST_LEN, NA_HEAD_DIM)),
        'state_s5_re': nrm((DEC_BATCH, N_EVEN, 2, S5_GROUPS, S5_STATE), 0.1),
        'state_s5_im': nrm((DEC_BATCH, N_EVEN, 2, S5_GROUPS, S5_STATE), 0.1),
        'state_ssd': nrm((DEC_BATCH, N_ODD, 2, SSD_HEADS, SSD_HEAD_DIM, SSD_STATE), 0.1),
        'c': nrm((DEC_BATCH, D_MODEL)),
        'c_ctx': nrm((D_MODEL,)),
        'norm_mix_g': 1.0 + nrm((DEPTH, D_MODEL), 0.01),
        'norm_ffn_g': 1.0 + nrm((DEPTH, D_MODEL), 0.01),
        'ada_w': nrm((DEPTH, D_MODEL, ADA_CHUNKS * D_MODEL), 0.5 * D_MODEL ** -0.5),
        'ada_b': nrm((DEPTH, ADA_CHUNKS * D_MODEL), 0.01),
        'ffn_w_gate': nrm((DEPTH, D_MODEL, D_FF), D_MODEL ** -0.5),
        'ffn_w_up': nrm((DEPTH, D_MODEL, D_FF), D_MODEL ** -0.5),
        'ffn_w_down': nrm((DEPTH, D_FF, D_MODEL), D_FF ** -0.5),
        'ev_w_in': nrm((N_EVEN, D_MODEL, d_ev_in), D_MODEL ** -0.5),
        'ev_w_out': nrm((N_EVEN, d_ev_out, D_MODEL), d_ev_out ** -0.5),
        's5_a_re': -0.5 + nrm((N_EVEN, 2, S5_GROUPS, S5_STATE), 0.01),
        's5_a_im': math.pi * n_idx + nrm((N_EVEN, 2, S5_GROUPS, S5_STATE), 0.01),
        's5_log_dt': jax.random.uniform(next(ks), (N_EVEN, 2, S5_GROUPS), jnp.float32,
                                        math.log(S5_DT_MIN), math.log(S5_DT_MAX)),
        's5_b_re': nrm((N_EVEN, 2, S5_GROUPS, S5_STATE, S5_GROUP_CH), (2 * S5_GROUP_CH) ** -0.5),
        's5_b_im': nrm((N_EVEN, 2, S5_GROUPS, S5_STATE, S5_GROUP_CH), (2 * S5_GROUP_CH) ** -0.5),
        's5_c_re': nrm((N_EVEN, 2, S5_GROUPS, S5_GROUP_CH, S5_STATE), S5_STATE ** -0.5),
        's5_c_im': nrm((N_EVEN, 2, S5_GROUPS, S5_GROUP_CH, S5_STATE), S5_STATE ** -0.5),
        's5_d': nrm((N_EVEN, D_S5)),
        's5_w_glu': nrm((N_EVEN, D_S5, D_S5), D_S5 ** -0.5),
        'na_rpb': nrm((N_EVEN, NA_HEADS, 2 * NA_WIN_ROWS - 1, 2 * NA_WIN_COLS - 1), 0.1),
        'od_w_in': nrm((N_ODD, D_MODEL, SSD_IN_DIM), D_MODEL ** -0.5),
        'od_conv_w': nrm((N_ODD, SSD_CONV, SSD_CONV_DIM), SSD_CONV ** -0.5),
        'od_conv_b': nrm((N_ODD, SSD_CONV_DIM), 0.01),
        'ssd_a_log': jnp.log(jax.random.uniform(next(ks), (N_ODD, 2, SSD_HEADS), jnp.float32, 1.0, 16.0)),
        'ssd_dt_bias': ssd_dt + jnp.log(-jnp.expm1(-ssd_dt)),
        'ssd_d': 1.0 + nrm((N_ODD, SSD_HEADS), 0.01),
        'ssd_norm_g': 1.0 + nrm((N_ODD, D_INNER), 0.01),
        'od_w_out': nrm((N_ODD, D_INNER, D_MODEL), D_INNER ** -0.5),
        'final_norm_g': 1.0 + nrm((D_MODEL,), 0.01),
    }


def reference(x_prompt, x_sample, cache_na_k, cache_na_v, state_s5_re, state_s5_im, state_ssd, c, c_ctx,
              norm_mix_g, norm_ffn_g, ada_w, ada_b, ffn_w_gate, ffn_w_up, ffn_w_down,
              ev_w_in, ev_w_out, s5_a_re, s5_a_im, s5_log_dt, s5_b_re, s5_b_im, s5_c_re, s5_c_im,
              s5_d, s5_w_glu, na_rpb, od_w_in, od_conv_w, od_conv_b, ssd_a_log, ssd_dt_bias, ssd_d,
              ssd_norm_g, od_w_out, final_norm_g):
    xp, xs = x_prompt, x_sample
    bsz_p = xp.shape[0]
    new_k, new_v, new_s5, new_ssd = [], [], [], []
    for layer in range(DEPTH):
        sh_p, sc_p, gt_p, shf_p, scf_p, gtf_p = _adaln(c_ctx[None, :], ada_w[layer], ada_b[layer])
        sh_s, sc_s, gt_s, shf_s, scf_s, gtf_s = _adaln(c, ada_w[layer], ada_b[layer])
        hp = _modulate(_rmsnorm(xp, norm_mix_g[layer]), sh_p, sc_p)
        hs = _modulate(_rmsnorm(xs, norm_mix_g[layer]), sh_s, sc_s)
        if layer % 2 == 0:
            e = layer // 2
            s5_params = (s5_a_re[e], s5_a_im[e], s5_log_dt[e], s5_b_re[e], s5_b_im[e],
                         s5_c_re[e], s5_c_im[e], s5_d[e], s5_w_glu[e])
            zero_h0 = jnp.zeros((bsz_p, 2, S5_GROUPS, S5_STATE), jnp.complex64)
            out_p, (k_ctx, v_ctx), s5_fin = _even_mixer(hp, ev_w_in[e], ev_w_out[e], s5_params, na_rpb[e],
                                                        zero_h0, None, None)
            h0_s = lax.complex(state_s5_re[:, e].astype(jnp.float32), state_s5_im[:, e].astype(jnp.float32))
            out_s, _, _ = _even_mixer(hs, ev_w_in[e], ev_w_out[e], s5_params, na_rpb[e],
                                      h0_s, cache_na_k[:, e], cache_na_v[:, e])
            new_k.append(k_ctx)
            new_v.append(v_ctx)
            new_s5.append(s5_fin)
        else:
            o = layer // 2
            ssd_params = (od_w_in[o], od_conv_w[o], od_conv_b[o], ssd_a_log[o], ssd_dt_bias[o],
                          ssd_d[o], ssd_norm_g[o], od_w_out[o])
            zero_h0 = jnp.zeros((bsz_p, 2, SSD_HEADS, SSD_HEAD_DIM, SSD_STATE), jnp.float32)
            out_p, ssd_fin = _odd_mixer(hp, *ssd_params, zero_h0)
            out_s, _ = _odd_mixer(hs, *ssd_params, state_ssd[:, o].astype(jnp.float32))
            new_ssd.append(ssd_fin)
        xp = xp + gt_p * out_p
        xs = xs + gt_s * out_s
        ffn = (ffn_w_gate[layer], ffn_w_up[layer], ffn_w_down[layer])
        xp = xp + gtf_p * _swiglu(_modulate(_rmsnorm(xp, norm_ffn_g[layer]), shf_p, scf_p), *ffn)
        xs = xs + gtf_s * _swiglu(_modulate(_rmsnorm(xs, norm_ffn_g[layer]), shf_s, scf_s), *ffn)
    y_prompt = _rmsnorm(xp, final_norm_g)
    y_sample = _rmsnorm(xs, final_norm_g)
    new_cache_na_k = jnp.stack(new_k, axis=1)
    new_cache_na_v = jnp.stack(new_v, axis=1)
    s5_all = jnp.stack(new_s5, axis=1)
    new_state_s5_re = jnp.real(s5_all)
    new_state_s5_im = jnp.imag(s5_all)
    new_state_ssd = jnp.stack(new_ssd, axis=1)
    return (y_prompt, y_sample, new_cache_na_k, new_cache_na_v, new_state_s5_re, new_state_s5_im, new_state_ssd)
```

```python
import functools
import math

import jax
import jax.numpy as jnp
from jax import lax
from jax.experimental import pallas as pl
from jax.experimental.pallas import tpu as pltpu

F32 = jnp.float32
BF = jnp.bfloat16
HI = lax.Precision.HIGHEST

D_MODEL = 1024
EPS = 1e-6
ADA_CHUNKS = 6
GRID_W = 64
D_S5 = 512
S5_P = 16
S5_G = D_S5 // S5_P
S5_N = 64
S5_MIN_DECAY = 1e-4
S5_Q = 16
S5_PAIRS = S5_G // 2
D_NA = 512
NA_HD = 64
NA_HEADS = D_NA // NA_HD
NA_WIN_ROWS = 8
NA_WIN_COLS = 16
D_INNER = 2048
SSD_P = 64
SSD_H = D_INNER // SSD_P
SSD_G = 4
SSD_N = 128
SSD_Q = 128
SSD_CONV = 5
SSD_GN = SSD_G * SSD_N
SSD_CONV_DIM = D_INNER + 2 * SSD_GN
D_FF = 2816
FF_CHUNK = 256

V7X_VMEM_BYTES = 64 * 1024 * 1024
VMEM_LIMIT = V7X_VMEM_BYTES - 8 * 1024 * 1024
LANES = 128
NEG_BIG = -1e30


def _params(*sem):
    return pltpu.CompilerParams(dimension_semantics=sem, vmem_limit_bytes=VMEM_LIMIT)


def _resident(shape):
    nd = len(shape)
    return pl.BlockSpec(shape, lambda *_: (0,) * nd, pipeline_mode=pl.Buffered(1))


def _dot(a, b):
    return jnp.dot(a, b, preferred_element_type=F32)


def _dot_nt(a, b):
    return lax.dot_general(a, b, (((1,), (1,)), ((), ())), preferred_element_type=F32)


def _dot_hi(a, b):
    return jnp.dot(a, b, preferred_element_type=F32, precision=HI)


def _sigmoid(x):
    return 1.0 / (1.0 + jnp.exp(-x))


def _silu(x):
    return x * _sigmoid(x)


def _rms(x, g):
    return x * lax.rsqrt(jnp.mean(x * x, axis=-1, keepdims=True) + EPS) * g


def _norm_mod(x, g, shift, scale):
    return _rms(x, g) * (1.0 + scale) + shift


def _ada_kernel(c_ref, w_ref, b_ref, o_ref):
    c = c_ref[...]
    o_ref[...] = _dot(_silu(c).astype(BF), w_ref[...].astype(BF)) + b_ref[...]


def _ada(cond, ada_w, ada_b):
    depth, d, n = ada_w.shape
    tn = 1536
    return pl.pallas_call(
        _ada_kernel,
        grid=(depth, n // tn),
        in_specs=[pl.BlockSpec((8, d), lambda l, j: (0, 0)),
                  pl.BlockSpec((None, d, tn), lambda l, j: (l, 0, j)),
                  pl.BlockSpec((None, 1, tn), lambda l, j: (l, 0, j))],
        out_specs=pl.BlockSpec((None, 8, tn), lambda l, j: (l, 0, j)),
        out_shape=jax.ShapeDtypeStruct((depth, 8, n), F32),
        compiler_params=_params("parallel", "parallel"),
        name="adaln",
    )(cond, ada_w, ada_b.reshape(depth, 1, n))


class _Stream:
    def __init__(self, n_p, len_s, n_s, tm):
        assert n_p % tm == 0 and len_s % tm == 0
        self.n_p, self.len_s, self.n_s, self.tm = n_p, len_s, n_s, tm
        self.total = n_p + len_s * n_s
        self.steps = self.total // tm
        self.p_steps = n_p // tm

    def group(self, i):
        t = i * self.tm
        return jnp.where(t < self.n_p, 0, 1 + (t - self.n_p) // self.len_s)

    def mod_spec(self):
        return pl.BlockSpec((None, ADA_CHUNKS, D_MODEL), lambda i: (self.group(i), 0, 0))

    def row_spec(self, width, col=0):
        return pl.BlockSpec((self.tm, width), lambda i: (i, col))

    def prompt_spec(self, width):
        return pl.BlockSpec((self.tm, width), lambda i: (jnp.minimum(i, self.p_steps - 1), 0))

    def sample_spec(self, width):
        return pl.BlockSpec((self.tm, width), lambda i: (jnp.maximum(i - self.p_steps, 0), 0))


def _ffn(x1, mod_ref, g_ref, wg_ref, wu_ref, wd_ref):
    h = _norm_mod(x1, g_ref[...], mod_ref[3:4, :], mod_ref[4:5, :]).astype(BF)

    def body(c, acc):
        off = pl.multiple_of(c * FF_CHUNK, FF_CHUNK)
        gate = _dot(h, wg_ref[:, pl.ds(off, FF_CHUNK)])
        up = _dot(h, wu_ref[:, pl.ds(off, FF_CHUNK)])
        hid = (_silu(gate) * up).astype(BF)
        return acc + _dot(hid, wd_ref[pl.ds(off, FF_CHUNK), :])

    acc = lax.fori_loop(0, D_FF // FF_CHUNK, body, jnp.zeros(x1.shape, F32))
    return x1 + mod_ref[5:6, :] * acc


def _even_in_kernel(x_ref, mod_ref, g_ref, w_ref, u_ref, q_ref, k_ref, v_ref):
    h = _norm_mod(x_ref[...], g_ref[...], mod_ref[0:1, :], mod_ref[1:2, :])
    r = _dot(h.astype(BF), w_ref[...])
    u_ref[...] = r[:, 0:D_S5]
    q_ref[...] = r[:, D_S5:D_S5 + D_NA] * (NA_HD ** -0.5)
    k_ref[...] = r[:, D_S5 + D_NA:D_S5 + 2 * D_NA]
    v_ref[...] = r[:, D_S5 + 2 * D_NA:D_S5 + 3 * D_NA]


def _even_in(st, x, mods, g, w_in):
    n_out = w_in.shape[1]
    out = jax.ShapeDtypeStruct((st.total, D_S5), F32)
    return pl.pallas_call(
        _even_in_kernel,
        grid=(st.steps,),
        in_specs=[st.row_spec(D_MODEL), st.mod_spec(), _resident((1, D_MODEL)),
                  _resident((D_MODEL, n_out))],
        out_specs=[st.row_spec(D_S5)] * 4,
        out_shape=[out] * 4,
        compiler_params=_params("parallel"),
        name="even_in",
    )(x, mods, g, w_in)


def _ctx_attn_kernel(q_ref, k_ref, v_ref, o_ref):
    q = q_ref[...]
    k = k_ref[...].astype(BF)
    v = v_ref[...].astype(BF)
    lane = lax.broadcasted_iota(jnp.int32, q.shape, 1)
    outs = []
    for h in range(2):
        in_head = (lane >= NA_HD * h) & (lane < NA_HD * (h + 1))
        qh = jnp.where(in_head, q, 0.0).astype(BF)
        s = _dot_nt(qh, k)
        p = jnp.exp(s - jnp.max(s, axis=-1, keepdims=True))
        l = jnp.sum(p, axis=-1, keepdims=True)
        outs.append(_dot(p.astype(BF), v) / l)
    o_ref[...] = jnp.where(lane < NA_HD, outs[0], outs[1])


def _ctx_attn(q, k, v, n_seq, seq):
    spec = pl.BlockSpec((seq, LANES), lambda b, p: (b, p))
    return pl.pallas_call(
        _ctx_attn_kernel,
        grid=(n_seq, D_NA // LANES),
        in_specs=[spec, spec, spec],
        out_specs=spec,
        out_shape=jax.ShapeDtypeStruct((n_seq * seq, D_NA), F32),
        compiler_params=_params("parallel", "parallel"),
        name="ctx_attn",
    )(q, k, v)


def _bias_kernel(rpb_ref, o_ref):
    h = pl.program_id(0)
    n_dr = 2 * NA_WIN_ROWS - 1
    n_dc = 2 * NA_WIN_COLS - 1
    wq = lax.broadcasted_iota(jnp.int32, (GRID_W, LANES), 0)
    lane = lax.broadcasted_iota(jnp.int32, (GRID_W, LANES), 1)
    wk = lane & (GRID_W - 1)
    left = lane < GRID_W
    col_start = jnp.clip(wq - NA_WIN_COLS // 2, 0, GRID_W - NA_WIN_COLS)
    ok = (wk >= col_start) & (wk < col_start + NA_WIN_COLS)
    dc = jnp.clip(wk - wq, -(NA_WIN_COLS - 1), NA_WIN_COLS - 1) + (NA_WIN_COLS - 1)
    for dr in range(n_dr - 1):
        acc = jnp.zeros((GRID_W, LANES), F32)
        for kk in range(n_dc):
            lo = rpb_ref[(h * n_dr + dr) * n_dc + kk]
            hi = rpb_ref[(h * n_dr + dr + 1) * n_dc + kk]
            acc = jnp.where(dc == kk, jnp.where(left, lo, hi), acc)
        o_ref[dr] = jnp.where(ok, acc, NEG_BIG)


def _bias_table(rpb):
    n_dr = 2 * NA_WIN_ROWS - 1
    return pl.pallas_call(
        _bias_kernel,
        grid=(NA_HEADS,),
        in_specs=[pl.BlockSpec(memory_space=pltpu.SMEM)],
        out_specs=pl.BlockSpec((None, n_dr - 1, GRID_W, LANES), lambda h: (h, 0, 0, 0)),
        out_shape=jax.ShapeDtypeStruct((NA_HEADS, n_dr - 1, GRID_W, LANES), F32),
        compiler_params=_params("parallel"),
        name="na_bias",
    )(rpb.reshape(-1))


def _na_kernel(q_ref, k_ref, v_ref, ck_ref, cv_ref, bias_ref, o_ref, kb_ref, vb_ref, *, rows):
    kb_ref[...] = k_ref[...].astype(BF)
    vb_ref[...] = v_ref[...].astype(BF)
    ck = ck_ref[...]
    cv = cv_ref[...]
    kh = NA_WIN_ROWS
    win = kh * GRID_W
    lane = lax.broadcasted_iota(jnp.int32, (GRID_W, LANES), 1)

    def body(r, carry):
        start = jnp.clip(r - kh // 2, 0, rows - kh)
        dr0 = start - r + (NA_WIN_ROWS - 1)
        q = q_ref[pl.ds(pl.multiple_of(r * GRID_W, GRID_W), GRID_W), :]
        koff = pl.multiple_of(start * GRID_W, GRID_W)
        kw = kb_ref[pl.ds(koff, win), :]
        vw = vb_ref[pl.ds(koff, win), :]
        outs = []
        for h in range(2):
            in_head = (lane >= NA_HD * h) & (lane < NA_HD * (h + 1))
            qh = jnp.where(in_head, q, 0.0).astype(BF)
            bias = jnp.concatenate([bias_ref[h, dr0 + 2 * i] for i in range(kh // 2)], axis=1)
            s_loc = _dot_nt(qh, kw) + bias
            s_ctx = _dot_nt(qh, ck)
            m = jnp.maximum(jnp.max(s_loc, axis=-1, keepdims=True), jnp.max(s_ctx, axis=-1, keepdims=True))
            p_loc = jnp.exp(s_loc - m)
            p_ctx = jnp.exp(s_ctx - m)
            l = jnp.sum(p_loc, axis=-1, keepdims=True) + jnp.sum(p_ctx, axis=-1, keepdims=True)
            outs.append((_dot(p_loc.astype(BF), vw) + _dot(p_ctx.astype(BF), cv)) / l)
        o_ref[pl.ds(pl.multiple_of(r * GRID_W, GRID_W), GRID_W), :] = jnp.where(lane < NA_HD, outs[0], outs[1])
        return carry

    lax.fori_loop(0, rows, body, 0)


def _na_attn(q, k, v, ck, cv, bias, n_seq, seq, row_base):
    rows = seq // GRID_W
    assert rows >= NA_WIN_ROWS
    past = ck.shape[1]
    spec = pl.BlockSpec((seq, LANES), lambda b, p: (row_base + b, p))
    cspec = pl.BlockSpec((None, past, LANES), lambda b, p: (b, 0, p))
    n_dr = 2 * NA_WIN_ROWS - 2
    return pl.pallas_call(
        functools.partial(_na_kernel, rows=rows),
        grid=(n_seq, D_NA // LANES),
        in_specs=[spec, spec, spec, cspec, cspec,
                  pl.BlockSpec((2, n_dr, GRID_W, LANES), lambda b, p: (p, 0, 0, 0))],
        out_specs=pl.BlockSpec((seq, LANES), lambda b, p: (b, p)),
        out_shape=jax.ShapeDtypeStruct((n_seq * seq, D_NA), F32),
        scratch_shapes=[pltpu.VMEM((seq, LANES), BF), pltpu.VMEM((seq, LANES), BF)],
        compiler_params=_params("parallel", "parallel"),
        name="na_attn",
    )(q, k, v, ck, cv, bias)


def _cexp(zr, zi):
    e = jnp.exp(zr)
    return e * jnp.cos(zi), e * jnp.sin(zi)


def _s5_prep_kernel(arr_ref, ari_ref, ldr_ref, acr_ref, aci_ref, ldc_ref, btr_ref, bti_ref, ctr_ref, cti_ref,
                    tw_ref, bw_ref, cw_ref, sc_ref):
    q = S5_Q
    rows = 2 * q * S5_P
    half = rows // 2
    nst = 2 * S5_N
    expand = (lax.broadcasted_iota(jnp.int32, (S5_P, rows), 0)
              == (lax.broadcasted_iota(jnp.int32, (S5_P, rows), 1) & (S5_P - 1))).astype(F32)
    for d in range(2):
        ar = jnp.minimum(arr_ref[d], -S5_MIN_DECAY)
        ai = ari_ref[d]
        dt = jnp.exp(ldr_ref[d])
        abr, abi = _cexp(ar * dt, ai * dt)
        den = ar * ar + ai * ai
        cfr = ((abr - 1.0) * ar + abi * ai) / den
        cfi = (abi * ar - (abr - 1.0) * ai) / den
        btr = btr_ref[d]
        bti = bti_ref[d]
        bbr = cfr * btr - cfi * bti
        bbi = cfr * bti + cfi * btr
        a16r, a16i = _cexp(ar * dt * q, ai * dt * q)
        sc_ref[2 * d:2 * d + 1, :] = a16r
        sc_ref[2 * d + 1:2 * d + 2, :] = a16i

        rho = lax.broadcasted_iota(jnp.int32, (rows, nst), 0)
        lam = lax.broadcasted_iota(jnp.int32, (rows, nst), 1)
        s_idx = (rho >> 4) & (q - 1)
        expo = ((q - 1 - s_idx) if d == 0 else s_idx).astype(F32)
        pr, pi = _cexp(ar * dt * expo, ai * dt * expo)
        tbr = jnp.concatenate([bbr] * (rows // S5_P), axis=0)
        tbi = jnp.concatenate([bbi] * (rows // S5_P), axis=0)
        same = (rho >> 8) == (lam >> 6)
        bw_ref[:, 2 * nst * d:2 * nst * d + nst] = jnp.where(same, pr * tbr - pi * tbi, 0.0).astype(BF)
        bw_ref[:, 2 * nst * d + nst:2 * nst * (d + 1)] = jnp.where(same, pr * tbi + pi * tbr, 0.0).astype(BF)

        arc = jnp.minimum(acr_ref[d], -S5_MIN_DECAY)
        aic = aci_ref[d]
        dtc = jnp.exp(ldc_ref[d])
        ctr = _dot_hi(ctr_ref[d], expand)
        cti = _dot_hi(cti_ref[d], expand)
        rho5 = lax.broadcasted_iota(jnp.int32, (nst, rows), 0)
        lam5 = lax.broadcasted_iota(jnp.int32, (nst, rows), 1)
        t_idx = (lam5 >> 4) & (q - 1)
        expo5 = ((t_idx + 1) if d == 0 else (q - t_idx)).astype(F32)
        pr5, pi5 = _cexp(arc * dtc * expo5, aic * dtc * expo5)
        same5 = (rho5 >> 6) == (lam5 >> 8)
        cw_ref[2 * nst * d:2 * nst * d + nst, :] = jnp.where(same5, ctr * pr5 - cti * pi5, 0.0).astype(BF)
        cw_ref[2 * nst * d + nst:2 * nst * (d + 1), :] = jnp.where(same5, -(ctr * pi5 + cti * pr5), 0.0).astype(BF)

        rho2 = lax.broadcasted_iota(jnp.int32, (nst, half), 0)
        lam2 = lax.broadcasted_iota(jnp.int32, (nst, half), 1)
        dl = lam2 >> 4
        delta = (dl if d == 0 else (q - 1 - dl)).astype(F32)
        pr2, pi2 = _cexp(arc * dtc * delta, aic * dtc * delta)
        ggr = ctr[:, 0:half] * pr2 - cti[:, 0:half] * pi2
        ggi = ctr[:, 0:half] * pi2 + cti[:, 0:half] * pr2
        lane = lax.broadcasted_iota(jnp.int32, (S5_P, half), 1)
        for e in range(2):
            mine = (rho2 >> 6) == e
            kt = _dot_hi(bbr, jnp.where(mine, ggr, 0.0)) - _dot_hi(bbi, jnp.where(mine, ggi, 0.0))
            for s in range(q):
                if d == 0:
                    blk = jnp.where(lane >= S5_P * s, pltpu.roll(kt, S5_P * s, 1), 0.0)
                else:
                    blk = jnp.where(lane < S5_P * (s + 1), pltpu.roll(kt, (half - S5_P * (q - 1 - s)) % half, 1), 0.0)
                tw_ref[e, S5_P * s:S5_P * (s + 1), half * d:half * (d + 1)] = blk.astype(BF)


def _s5_prep(a_re, a_im, log_dt, b_re, b_im, c_re, c_im):
    nst = 2 * S5_N
    rows = 2 * S5_Q * S5_P
    np_ = S5_PAIRS
    row = lambda t: t.reshape(2, np_, 1, nst)
    col = lambda t: t.reshape(2, np_, nst, 1)
    ld = jnp.broadcast_to(log_dt[:, :, None], (2, S5_G, S5_N))
    bt = lambda t: t.reshape(2, np_, 2, S5_N, S5_P).transpose(0, 1, 4, 2, 3).reshape(2, np_, S5_P, nst)
    ct = lambda t: t.reshape(2, np_, 2, S5_P, S5_N).transpose(0, 1, 2, 4, 3).reshape(2, np_, nst, S5_P)
    rspec = pl.BlockSpec((2, None, 1, nst), lambda j: (0, j, 0, 0))
    cspec = pl.BlockSpec((2, None, nst, 1), lambda j: (0, j, 0, 0))
    btspec = pl.BlockSpec((2, None, S5_P, nst), lambda j: (0, j, 0, 0))
    ctspec = pl.BlockSpec((2, None, nst, S5_P), lambda j: (0, j, 0, 0))
    return pl.pallas_call(
        _s5_prep_kernel,
        grid=(np_,),
        in_specs=[rspec, rspec, rspec, cspec, cspec, cspec, btspec, btspec, ctspec, ctspec],
        out_specs=[pl.BlockSpec((None, 2, rows // 2, rows), lambda j: (j, 0, 0, 0)),
                   pl.BlockSpec((None, rows, 4 * nst), lambda j: (j, 0, 0)),
                   pl.BlockSpec((None, 4 * nst, rows), lambda j: (j, 0, 0)),
                   pl.BlockSpec((None, 4, nst), lambda j: (j, 0, 0))],
        out_shape=[jax.ShapeDtypeStruct((np_, 2, rows // 2, rows), BF),
                   jax.ShapeDtypeStruct((np_, rows, 4 * nst), BF),
                   jax.ShapeDtypeStruct((np_, 4 * nst, rows), BF),
                   jax.ShapeDtypeStruct((np_, 4, nst), F32)],
        compiler_params=_params("parallel"),
        name="s5_prep",
    )(row(a_re), row(a_im), row(ld), col(a_re), col(a_im), col(ld), bt(b_re), bt(b_im), ct(c_re), ct(c_im))


def _s5_kernel(uf_ref, tw_ref, bw_ref, cw_ref, sc_ref, h0_ref, y_ref, fin_ref, z_ref, h_ref, *, bp, mp, bs, ms):
    nst = 2 * S5_N
    half = S5_Q * S5_P
    u = uf_ref[...]
    z_ref[...] = _dot(u, bw_ref[...])
    sc = sc_ref[...]
    afr, afi, abr, abi = sc[0:1], sc[1:2], sc[2:3], sc[3:4]

    def scan(base, nb, nm, init):
        def body(c, carry):
            hfr, hfi, hbr, hbi = carry
            rf = pl.multiple_of(base + c * nb, nb)
            rb = pl.multiple_of(base + (nm - 1 - c) * nb, nb)
            h_ref[pl.ds(rf, nb), 0:nst] = hfr
            h_ref[pl.ds(rf, nb), nst:2 * nst] = hfi
            h_ref[pl.ds(rb, nb), 2 * nst:3 * nst] = hbr
            h_ref[pl.ds(rb, nb), 3 * nst:4 * nst] = hbi
            zfr = z_ref[pl.ds(rf, nb), 0:nst]
            zfi = z_ref[pl.ds(rf, nb), nst:2 * nst]
            zbr = z_ref[pl.ds(rb, nb), 2 * nst:3 * nst]
            zbi = z_ref[pl.ds(rb, nb), 3 * nst:4 * nst]
            return (afr * hfr - afi * hfi + zfr, afr * hfi + afi * hfr + zfi,
                    abr * hbr - abi * hbi + zbr, abr * hbi + abi * hbr + zbi)
        return lax.fori_loop(0, nm, body, init)

    zero = jnp.zeros((bp, nst), F32)
    fin = scan(0, bp, mp, (zero, zero, zero, zero))
    for i in range(4):
        fin_ref[:, nst * i:nst * (i + 1)] = fin[i]
    h0 = h0_ref[...]
    scan(bp * mp, bs, ms, tuple(h0[:, nst * i:nst * (i + 1)] for i in range(4)))

    yc = _dot(h_ref[...].astype(BF), cw_ref[...])
    for e in range(2):
        t = _dot(u[:, half * e:half * (e + 1)], tw_ref[e])
        y_ref[:, half * e:half * (e + 1)] = t[:, 0:half] + t[:, half:2 * half] + yc[:, half * e:half * (e + 1)]


def _s5(uf, tw, bw, cw, sc, h0, bp, mp, bs, ms):
    np_, r, rows = uf.shape
    nst = 2 * S5_N
    return pl.pallas_call(
        functools.partial(_s5_kernel, bp=bp, mp=mp, bs=bs, ms=ms),
        grid=(np_,),
        in_specs=[pl.BlockSpec((None, r, rows), lambda j: (j, 0, 0)),
                  pl.BlockSpec((None, 2, rows // 2, rows), lambda j: (j, 0, 0, 0)),
                  pl.BlockSpec((None, rows, 4 * nst), lambda j: (j, 0, 0)),
                  pl.BlockSpec((None, 4 * nst, rows), lambda j: (j, 0, 0)),
                  pl.BlockSpec((None, 4, nst), lambda j: (j, 0, 0)),
                  pl.BlockSpec((None, bs, 4 * nst), lambda j: (j, 0, 0))],
        out_specs=[pl.BlockSpec((None, r, rows), lambda j: (j, 0, 0)),
                   pl.BlockSpec((None, bp, 4 * nst), lambda j: (j, 0, 0))],
        out_shape=[jax.ShapeDtypeStruct((np_, r, rows), F32),
                   jax.ShapeDtypeStruct((np_, bp, 4 * nst), F32)],
        scratch_shapes=[pltpu.VMEM((r, 4 * nst), F32), pltpu.VMEM((r, 4 * nst), F32)],
        compiler_params=_params("parallel"),
        name="s5_scan",
    )(uf, tw, bw, cw, sc, h0)


def _pad8(n):
    return -(-n // 8) * 8


def _to_chunk_rows(u, n_seq, seq):
    m = seq // S5_Q
    t = u.reshape(n_seq, m, S5_Q, S5_PAIRS, 2, S5_P).transpose(3, 1, 0, 4, 2, 5)
    t = jnp.pad(t, ((0, 0), (0, 0), (0, _pad8(n_seq) - n_seq), (0, 0), (0, 0), (0, 0)))
    return t.reshape(S5_PAIRS, m * _pad8(n_seq), 2 * S5_Q * S5_P)


def _from_chunk_rows(y, n_seq, seq):
    m = seq // S5_Q
    t = y.reshape(S5_PAIRS, m, _pad8(n_seq), 2, S5_Q, S5_P)[:, :, :n_seq].transpose(2, 1, 4, 0, 3, 5)
    return t.reshape(n_seq * seq, D_S5)


def _state_to_pairs(re, im):
    b = re.shape[0]
    parts = [t[:, d].reshape(b, S5_PAIRS, 2 * S5_N) for d in range(2) for t in (re, im)]
    t = jnp.concatenate(parts, axis=-1).transpose(1, 0, 2)
    return jnp.pad(t, ((0, 0), (0, _pad8(b) - b), (0, 0)))


def _pairs_to_state(fin):
    nst = 2 * S5_N
    b = fin.shape[1]
    t = fin.transpose(1, 0, 2).reshape(b, S5_PAIRS, 2, 2, nst)
    t = t.transpose(3, 0, 2, 1, 4).reshape(2, b, 2, S5_G, S5_N)
    return t[0], t[1]


def _gelu_tanh(x):
    return 0.5 * x * (1.0 + jnp.tanh(math.sqrt(2.0 / math.pi) * (x + 0.044715 * (x * x * x))))


def _even_out_kernel(x_ref, mod_ref, yp_ref, ys_ref, u_ref, ap_ref, as_ref, dsk_ref, wglu_ref, wout_ref,
                     gf_ref, wg_ref, wu_ref, wd_ref, o_ref, *, p_steps):
    is_p = pl.program_id(0) < p_steps
    conv = jnp.where(is_p, yp_ref[...], ys_ref[...])
    attn = jnp.where(is_p, ap_ref[...], as_ref[...])
    y = _gelu_tanh(conv + u_ref[...] * dsk_ref[...])
    y = y * _sigmoid(_dot(y.astype(BF), wglu_ref[...]))
    out = _dot(y.astype(BF), wout_ref[0:D_S5, :]) + _dot(attn.astype(BF), wout_ref[D_S5:D_S5 + D_NA, :])
    x1 = x_ref[...] + mod_ref[2:3, :] * out
    o_ref[...] = _ffn(x1, mod_ref, gf_ref, wg_ref, wu_ref, wd_ref)


def _even_out(st, x, mods, y_p, y_s, u, attn_p, attn_s, d_skip, w_glu, w_out, g_ffn, wg, wu, wd):
    return pl.pallas_call(
        functools.partial(_even_out_kernel, p_steps=st.p_steps),
        grid=(st.steps,),
        in_specs=[st.row_spec(D_MODEL), st.mod_spec(), st.prompt_spec(D_S5), st.sample_spec(D_S5),
                  st.row_spec(D_S5), st.prompt_spec(D_NA), st.sample_spec(D_NA),
                  _resident((1, D_S5)), _resident(w_glu.shape), _resident(w_out.shape),
                  _resident((1, D_MODEL)), _resident(wg.shape), _resident(wu.shape), _resident(wd.shape)],
        out_specs=st.row_spec(D_MODEL),
        out_shape=jax.ShapeDtypeStruct((st.total, D_MODEL), F32),
        compiler_params=_params("parallel"),
        name="even_out_ffn",
    )(x, mods, y_p, y_s, u, attn_p, attn_s, d_skip, w_glu, w_out, g_ffn, wg, wu, wd)


def _odd_in_kernel(x_ref, mod_ref, g_ref, wz_ref, wx_ref, wdt_ref, z_ref, xbc_ref, dt_ref):
    h = _norm_mod(x_ref[...], g_ref[...], mod_ref[0:1, :], mod_ref[1:2, :]).astype(BF)
    z_ref[...] = _dot(h, wz_ref[...])
    xbc_ref[...] = _dot(h, wx_ref[...])
    dt_ref[...] = _dot(h, wdt_ref[...])


def _odd_in(st, x, mods, g, wz, wx, wdt):
    return pl.pallas_call(
        _odd_in_kernel,
        grid=(st.steps,),
        in_specs=[st.row_spec(D_MODEL), st.mod_spec(), _resident((1, D_MODEL)),
                  _resident(wz.shape), _resident(wx.shape), _resident(wdt.shape)],
        out_specs=[st.row_spec(D_INNER), st.row_spec(SSD_CONV_DIM), st.row_spec(2 * SSD_H)],
        out_shape=[jax.ShapeDtypeStruct((st.total, D_INNER), F32),
                   jax.ShapeDtypeStruct((st.total, SSD_CONV_DIM), F32),
                   jax.ShapeDtypeStruct((st.total, 2 * SSD_H), F32)],
        compiler_params=_params("parallel"),
        name="odd_in",
    )(x, mods, g, wz, wx, wdt)


def _conv_kernel(x_ref, prev_ref, next_ref, w_ref, b_ref, dtr_ref, dtb_ref, xc_ref, dt_ref, *, lt, n_p, seq_p, seq_s):
    tok = pl.program_id(0) * lt
    in_p = tok < n_p
    pos = jnp.where(in_p, tok % seq_p, (tok - n_p) % seq_s)
    seq = jnp.where(in_p, seq_p, seq_s)
    prev = jnp.where(pos == 0, 0.0, prev_ref[...])
    nxt = jnp.where(pos + lt == seq, 0.0, next_ref[...])
    ext = jnp.concatenate([prev, x_ref[...], nxt], axis=0)
    halo = prev.shape[0]
    acc = b_ref[...] + jnp.zeros((lt, x_ref.shape[1]), F32)
    for kk in range(SSD_CONV):
        off = halo + kk - SSD_CONV // 2
        acc = acc + w_ref[kk:kk + 1, :] * ext[off:off + lt, :]
    xc_ref[...] = _silu(acc)
    v = dtr_ref[...] + dtb_ref[...]
    dt_ref[...] = jnp.maximum(v, 0.0) + jnp.log1p(jnp.exp(-jnp.abs(v)))


def _conv(xbc, dt_raw, conv_w, conv_b, dt_bias, n_p, seq_p, seq_s):
    total, c = xbc.shape
    lt = math.gcd(seq_p, 256)
    halo = 8
    nblk = total // halo
    per = lt // halo
    return pl.pallas_call(
        functools.partial(_conv_kernel, lt=lt, n_p=n_p, seq_p=seq_p, seq_s=seq_s),
        grid=(total // lt,),
        in_specs=[pl.BlockSpec((lt, c), lambda i: (i, 0)),
                  pl.BlockSpec((halo, c), lambda i: (jnp.maximum(i * per - 1, 0), 0)),
                  pl.BlockSpec((halo, c), lambda i: (jnp.minimum((i + 1) * per, nblk - 1), 0)),
                  _resident(conv_w.shape), _resident((1, c)),
                  pl.BlockSpec((lt, 2 * SSD_H), lambda i: (i, 0)), _resident((1, 2 * SSD_H))],
        out_specs=[pl.BlockSpec((lt, c), lambda i: (i, 0)), pl.BlockSpec((lt, 2 * SSD_H), lambda i: (i, 0))],
        out_shape=[jax.ShapeDtypeStruct((total, c), F32), jax.ShapeDtypeStruct((total, 2 * SSD_H), F32)],
        compiler_params=_params("parallel"),
        name="ssd_conv",
    )(xbc, xbc, xbc, conv_w, conv_b.reshape(1, c), dt_raw, dt_bias.reshape(1, 2 * SSD_H))


def _ssd_kernel(xs_ref, b_ref, c_ref, dt_ref, dtt_ref, alr_ref, alc_ref, dsk_ref, h0_ref, y_ref, fin_ref, st_ref,
                *, n_chunks, n_p, seq_p, seq_s):
    q = SSD_Q
    d = pl.program_id(0)
    ci = pl.program_id(1)
    blocks = D_INNER // LANES
    tok = jnp.where(d == 0, ci, n_chunks - 1 - ci) * q
    in_p = tok < n_p
    pos = jnp.where(in_p, tok % seq_p, (tok - n_p) % seq_s)
    first = pos == 0
    last = pos + q == jnp.where(in_p, seq_p, seq_s)
    start = jnp.where(d == 0, first, last)
    end = jnp.where(d == 0, last, first)

    @pl.when(jnp.logical_and(start, in_p))
    def _():
        st_ref[...] = jnp.zeros(st_ref.shape, F32)

    @pl.when(jnp.logical_and(start, jnp.logical_not(in_p)))
    def _():
        h0 = h0_ref[...].reshape(D_INNER, SSD_N)
        for kb in range(blocks):
            st_ref[:, LANES * kb:LANES * (kb + 1)] = h0[LANES * kb:LANES * (kb + 1), :].T

    skip_on = jnp.where(d == 0, 1.0, 0.0)
    sgn = 1 - 2 * d
    li = lax.broadcasted_iota(jnp.int32, (q, q), 0)
    si = lax.broadcasted_iota(jnp.int32, (q, q), 1)
    causal = (li - si) * sgn >= 0
    tri_l = jnp.where(causal, 1.0, 0.0)
    tri_r = jnp.where((si - li) * sgn >= 0, 1.0, 0.0)
    a_row = -jnp.exp(alr_ref[...])
    a_col = -jnp.exp(alc_ref[...])
    dt_col = dt_ref[...]
    dt_row = dtt_ref[...]
    cs_col = _dot_hi(tri_l, dt_col * a_row)
    da_row = dt_row * a_col
    cs_row = _dot_hi(da_row, tri_r)
    tot = jnp.sum(da_row, axis=-1, keepdims=True)
    w_row = dt_row * jnp.exp(tot - cs_row)
    etot = jnp.exp(tot)
    lane = lax.broadcasted_iota(jnp.int32, (q, LANES), 1)
    heads_per_group = SSD_H // SSD_G
    for g in range(SSD_G):
        bg = b_ref[:, SSD_N * g:SSD_N * (g + 1)]
        cg = c_ref[:, SSD_N * g:SSD_N * (g + 1)]
        cb = _dot_nt(cg.astype(BF), bg.astype(BF))
        bgt = bg.T
        for jp in range(heads_per_group // 2):
            pi = g * (heads_per_group // 2) + jp
            cols = slice(LANES * pi, LANES * (pi + 1))
            xp = xs_ref[:, cols]
            sp = st_ref[:, cols]
            ypair = jnp.zeros((q, LANES), F32)
            upd = jnp.zeros((SSD_N, LANES), F32)
            for hh in range(2):
                h = 2 * pi + hh
                mine = (lane >= SSD_P * hh) & (lane < SSD_P * (hh + 1))
                xm = jnp.where(mine, xp, 0.0).astype(BF)
                sm = jnp.where(mine, sp, 0.0).astype(BF)
                csc = jnp.broadcast_to(cs_col[:, h:h + 1], (q, q))
                decay = jnp.exp(jnp.where(causal, csc - cs_row[h:h + 1, :], -jnp.inf))
                m = (decay * cb * dt_row[h:h + 1, :]).astype(BF)
                ce = (cg * jnp.exp(csc)).astype(BF)
                ypair = ypair + _dot(m, xm) + _dot(ce, sm)
                bw = (bgt * w_row[h:h + 1, :]).astype(BF)
                upd = upd + _dot(bw, xm)
            keep = jnp.where(lane < SSD_P, etot[2 * pi:2 * pi + 1, :], etot[2 * pi + 1:2 * pi + 2, :])
            st_ref[:, cols] = keep * sp + upd
            y_ref[:, cols] = ypair + (skip_on * dsk_ref[:, cols]) * xp

    @pl.when(jnp.logical_and(end, in_p))
    def _():
        for kb in range(blocks):
            t = st_ref[:, LANES * kb:LANES * (kb + 1)].T
            fin_ref[2 * kb:2 * kb + 2] = t.reshape(2, SSD_P, SSD_N)


def _ssd(xc, dt_dir, dtt_dir, a_log, d_skip, h0, n_p, seq_p, seq_s):
    q = SSD_Q
    total = xc.shape[0]
    n_chunks = total // q
    n_prompt = n_p // seq_p
    n_sample = h0.shape[0]

    def blk(d, c):
        return jnp.where(d == 0, c, n_chunks - 1 - c)

    def h0_idx(d, c):
        return (jnp.clip((blk(d, c) * q - n_p) // seq_s, 0, n_sample - 1), d, 0, 0, 0)

    def fin_idx(d, c):
        return (jnp.minimum(blk(d, c) * q // seq_p, n_prompt - 1), d, 0, 0, 0)

    state_block = (None, None, SSD_H, SSD_P, SSD_N)
    return pl.pallas_call(
        functools.partial(_ssd_kernel, n_chunks=n_chunks, n_p=n_p, seq_p=seq_p, seq_s=seq_s),
        grid=(2, n_chunks),
        in_specs=[pl.BlockSpec((q, D_INNER), lambda d, c: (blk(d, c), 0)),
                  pl.BlockSpec((q, SSD_GN), lambda d, c: (blk(d, c), D_INNER // SSD_GN)),
                  pl.BlockSpec((q, SSD_GN), lambda d, c: (blk(d, c), D_INNER // SSD_GN + 1)),
                  pl.BlockSpec((None, q, SSD_H), lambda d, c: (d, blk(d, c), 0)),
                  pl.BlockSpec((None, SSD_H, q), lambda d, c: (d, 0, blk(d, c))),
                  pl.BlockSpec((None, 1, SSD_H), lambda d, c: (d, 0, 0)),
                  pl.BlockSpec((None, SSD_H, 1), lambda d, c: (d, 0, 0)),
                  _resident((1, D_INNER)),
                  pl.BlockSpec(state_block, h0_idx)],
        out_specs=[pl.BlockSpec((None, q, D_INNER), lambda d, c: (d, blk(d, c), 0)),
                   pl.BlockSpec(state_block, fin_idx)],
        out_shape=[jax.ShapeDtypeStruct((2, total, D_INNER), F32),
                   jax.ShapeDtypeStruct((n_prompt, 2, SSD_H, SSD_P, SSD_N), F32)],
        scratch_shapes=[pltpu.VMEM((SSD_N, D_INNER), F32)],
        compiler_params=_params("parallel", "arbitrary"),
        name="ssd_scan",
    )(xc, xc, xc, dt_dir, dtt_dir, a_log.reshape(2, 1, SSD_H), a_log.reshape(2, SSD_H, 1), d_skip, h0)


def _odd_out_kernel(x_ref, mod_ref, yf_ref, yb_ref, z_ref, ng_ref, wout_ref,
                    gf_ref, wg_ref, wu_ref, wd_ref, fg_ref, o_ref, *, final):
    y = (yf_ref[...] + yb_ref[...]) * _silu(z_ref[...])
    y = _rms(y, ng_ref[...])
    x1 = x_ref[...] + mod_ref[2:3, :] * _dot(y.astype(BF), wout_ref[...])
    x2 = _ffn(x1, mod_ref, gf_ref, wg_ref, wu_ref, wd_ref)
    o_ref[...] = _rms(x2, fg_ref[...]) if final else x2


def _odd_out(st, x, mods, y, z, norm_g, w_out, g_ffn, wg, wu, wd, final_g, final):
    ydir = lambda d: pl.BlockSpec((None, st.tm, D_INNER), lambda i: (d, i, 0))
    return pl.pallas_call(
        functools.partial(_odd_out_kernel, final=final),
        grid=(st.steps,),
        in_specs=[st.row_spec(D_MODEL), st.mod_spec(), ydir(0), ydir(1), st.row_spec(D_INNER),
                  _resident((1, D_INNER)), _resident(w_out.shape),
                  _resident((1, D_MODEL)), _resident(wg.shape), _resident(wu.shape), _resident(wd.shape),
                  _resident((1, D_MODEL))],
        out_specs=st.row_spec(D_MODEL),
        out_shape=jax.ShapeDtypeStruct((st.total, D_MODEL), F32),
        compiler_params=_params("parallel"),
        name="odd_out_ffn",
    )(x, mods, y, y, z, norm_g, w_out, g_ffn, wg, wu, wd, final_g)


def kernel(x_prompt, x_sample, cache_na_k, cache_na_v, state_s5_re, state_s5_im, state_ssd, c, c_ctx, norm_mix_g, norm_ffn_g, ada_w, ada_b, ffn_w_gate, ffn_w_up, ffn_w_down, ev_w_in, ev_w_out, s5_a_re, s5_a_im, s5_log_dt, s5_b_re, s5_b_im, s5_c_re, s5_c_im, s5_d, s5_w_glu, na_rpb, od_w_in, od_conv_w, od_conv_b, ssd_a_log, ssd_dt_bias, ssd_d, ssd_norm_g, od_w_out, final_norm_g):
    bp, seq_p, d = x_prompt.shape
    bs, seq_s, _ = x_sample.shape
    depth = ada_w.shape[0]
    n_p = bp * seq_p
    assert d == D_MODEL and n_p % seq_s == 0
    st = _Stream(n_p, seq_s, bs, tm=math.gcd(512, math.gcd(n_p, seq_s)))
    st_odd = _Stream(n_p, seq_s, bs, tm=math.gcd(256, math.gcd(n_p, seq_s)))

    x = jnp.concatenate([x_prompt.reshape(n_p, d), x_sample.reshape(bs * seq_s, d)], axis=0)
    cond = jnp.concatenate([c_ctx[None, :], c, jnp.zeros((8 - 1 - bs, d), F32)], axis=0)
    mods = _ada(cond, ada_w, ada_b).reshape(depth, 8, ADA_CHUNKS, d)
    row = lambda t: t.reshape(1, -1)

    new_k, new_v, new_s5_re, new_s5_im, new_ssd = [], [], [], [], []
    for layer in range(depth):
        wg = ffn_w_gate[layer].astype(BF)
        wu = ffn_w_up[layer].astype(BF)
        wd = ffn_w_down[layer].astype(BF)
        g_mix = row(norm_mix_g[layer])
        g_ffn = row(norm_ffn_g[layer])
        if layer % 2 == 0:
            e = layer // 2
            u, q, k, v = _even_in(st, x, mods[layer], g_mix, ev_w_in[e].astype(BF))
            attn_p = _ctx_attn(q, k, v, bp, seq_p)
            heads = lambda t: t[:n_p].reshape(bp, seq_p, NA_HEADS, NA_HD).transpose(0, 2, 1, 3)
            new_k.append(heads(k))
            new_v.append(heads(v))
            ctx = lambda t: t[:, e].transpose(0, 2, 1, 3).reshape(bs, -1, D_NA).astype(BF)
            attn_s = _na_attn(q, k, v, ctx(cache_na_k), ctx(cache_na_v), _bias_table(na_rpb[e]),
                              bs, seq_s, n_p // seq_s)
            tw, bw, cw, sc = _s5_prep(s5_a_re[e], s5_a_im[e], s5_log_dt[e], s5_b_re[e], s5_b_im[e],
                                      s5_c_re[e], s5_c_im[e])
            uf = jnp.concatenate([_to_chunk_rows(u[:n_p], bp, seq_p), _to_chunk_rows(u[n_p:], bs, seq_s)],
                                 axis=1).astype(BF)
            h0 = _state_to_pairs(state_s5_re[:, e], state_s5_im[:, e])
            mp, ms = seq_p // S5_Q, seq_s // S5_Q
            yc, fin = _s5(uf, tw, bw, cw, sc, h0, _pad8(bp), mp, _pad8(bs), ms)
            y_p = _from_chunk_rows(yc[:, :_pad8(bp) * mp], bp, seq_p)
            y_s = _from_chunk_rows(yc[:, _pad8(bp) * mp:], bs, seq_s)
            fre, fim = _pairs_to_state(fin[:, :bp])
            new_s5_re.append(fre)
            new_s5_im.append(fim)
            x = _even_out(st, x, mods[layer], y_p, y_s, u, attn_p, attn_s, row(s5_d[e]),
                          s5_w_glu[e].astype(BF), ev_w_out[e].astype(BF), g_ffn, wg, wu, wd)
        else:
            o = layer // 2
            w_in = od_w_in[o].astype(BF)
            z, xbc, dt_raw = _odd_in(st, x, mods[layer], g_mix, w_in[:, :D_INNER],
                                     w_in[:, D_INNER:D_INNER + SSD_CONV_DIM], w_in[:, D_INNER + SSD_CONV_DIM:])
            xc, dt = _conv(xbc, dt_raw, od_conv_w[o], od_conv_b[o], ssd_dt_bias[o], n_p, seq_p, seq_s)
            dt_dir = dt.reshape(-1, 2, SSD_H).transpose(1, 0, 2)
            dtt_dir = dt_dir.transpose(0, 2, 1)
            y, fin = _ssd(xc, dt_dir, dtt_dir, ssd_a_log[o], row(jnp.repeat(ssd_d[o], SSD_P)), state_ssd[:, o],
                          n_p, seq_p, seq_s)
            new_ssd.append(fin)
            x = _odd_out(st_odd, x, mods[layer], y, z, row(ssd_norm_g[o]), od_w_out[o].astype(BF),
                         g_ffn, wg, wu, wd, row(final_norm_g), layer == depth - 1)
    if depth % 2 == 1:
        raise NotImplementedError("final norm is fused into the last (odd) layer")
    y_prompt = x[:n_p].reshape(bp, seq_p, d)
    y_sample = x[n_p:].reshape(bs, seq_s, d)
    return (y_prompt, y_sample, jnp.stack(new_k, axis=1), jnp.stack(new_v, axis=1),
            jnp.stack(new_s5_re, axis=1), jnp.stack(new_s5_im, axis=1), jnp.stack(new_ssd, axis=1))
```

```python
import functools
import math

import jax
import jax.numpy as jnp
from jax import lax
from jax.experimental import pallas as pl
from jax.experimental.pallas import tpu as pltpu

F32 = jnp.float32
BF = jnp.bfloat16
HI = lax.Precision.HIGHEST

D_MODEL = 1024
EPS = 1e-6
ADA_CHUNKS = 6
GRID_W = 64
D_S5 = 512
S5_P = 16
S5_G = D_S5 // S5_P
S5_N = 64
S5_MIN_DECAY = 1e-4
S5_Q = 16
D_NA = 512
NA_HD = 64
NA_HEADS = D_NA // NA_HD
NA_WIN_ROWS = 8
NA_WIN_COLS = 16
NA_UNROLL = 4
D_INNER = 2048
SSD_P = 64
SSD_H = D_INNER // SSD_P
SSD_G = 4
SSD_N = 128
SSD_Q = 128
SSD_CONV = 5
SSD_GN = SSD_G * SSD_N
SSD_CONV_DIM = D_INNER + 2 * SSD_GN
D_FF = 2816

V7X_VMEM_BYTES = 64 * 1024 * 1024
VMEM_LIMIT = V7X_VMEM_BYTES - 8 * 1024 * 1024
LANES = 128
NEG_BIG = -1e30


def _params(*sem):
    return pltpu.CompilerParams(dimension_semantics=sem, vmem_limit_bytes=VMEM_LIMIT)


def _resident(shape):
    nd = len(shape)
    return pl.BlockSpec(shape, lambda *_: (0,) * nd, pipeline_mode=pl.Buffered(1))


def _dot(a, b):
    return jnp.dot(a, b, preferred_element_type=F32)


def _dot_nt(a, b):
    return lax.dot_general(a, b, (((1,), (1,)), ((), ())), preferred_element_type=F32)


def _dot_hi(a, b):
    return jnp.dot(a, b, preferred_element_type=F32, precision=HI)


def _sigmoid(x):
    return 1.0 / (1.0 + jnp.exp(-x))


def _silu(x):
    return x * _sigmoid(x)


def _softplus(x):
    return jnp.maximum(x, 0.0) + jnp.log1p(jnp.exp(-jnp.abs(x)))


def _rms(x, g):
    return x * lax.rsqrt(jnp.mean(x * x, axis=-1, keepdims=True) + EPS) * g


def _norm_mod(x, g, shift, scale):
    return _rms(x, g) * (1.0 + scale) + shift


def _ada_kernel(c_ref, w_ref, b_ref, o_ref):
    c = c_ref[...]
    o_ref[...] = _dot(_silu(c).astype(BF), w_ref[...].astype(BF)) + b_ref[...]


def _ada(cond, ada_w, ada_b):
    depth, d, n = ada_w.shape
    tn = 1536
    return pl.pallas_call(
        _ada_kernel,
        grid=(depth, n // tn),
        in_specs=[pl.BlockSpec((8, d), lambda l, j: (0, 0)),
                  pl.BlockSpec((None, d, tn), lambda l, j: (l, 0, j)),
                  pl.BlockSpec((None, 1, tn), lambda l, j: (l, 0, j))],
        out_specs=pl.BlockSpec((None, 8, tn), lambda l, j: (l, 0, j)),
        out_shape=jax.ShapeDtypeStruct((depth, 8, n), F32),
        compiler_params=_params("parallel", "parallel"),
        name="adaln",
    )(cond, ada_w, ada_b.reshape(depth, 1, n))


class _Stream:
    def __init__(self, n_p, len_s, n_s, tm):
        assert n_p % tm == 0 and len_s % tm == 0
        self.n_p, self.len_s, self.n_s, self.tm = n_p, len_s, n_s, tm
        self.total = n_p + len_s * n_s
        self.steps = self.total // tm
        self.p_steps = n_p // tm

    def group(self, i):
        t = i * self.tm
        return jnp.where(t < self.n_p, 0, 1 + (t - self.n_p) // self.len_s)

    def mod_spec(self):
        return pl.BlockSpec((None, ADA_CHUNKS, D_MODEL), lambda i: (self.group(i), 0, 0))

    def row_spec(self, width, col=0):
        return pl.BlockSpec((self.tm, width), lambda i: (i, col))

    def prompt_spec(self, width):
        return pl.BlockSpec((self.tm, width), lambda i: (jnp.minimum(i, self.p_steps - 1), 0))

    def sample_spec(self, width):
        return pl.BlockSpec((self.tm, width), lambda i: (jnp.maximum(i - self.p_steps, 0), 0))


def _ffn(x1, mod_ref, g_ref, wg_ref, wu_ref, wd_ref):
    h = _norm_mod(x1, g_ref[...], mod_ref[3:4, :], mod_ref[4:5, :]).astype(BF)
    hid = (_silu(_dot(h, wg_ref[...])) * _dot(h, wu_ref[...])).astype(BF)
    return x1 + mod_ref[5:6, :] * _dot(hid, wd_ref[...])


def _even_in_kernel(xp_ref, xs_ref, mod_ref, g_ref, w_ref, u_ref, q_ref, k_ref, v_ref, *, p_steps):
    x = jnp.where(pl.program_id(0) < p_steps, xp_ref[...], xs_ref[...])
    h = _norm_mod(x, g_ref[...], mod_ref[0:1, :], mod_ref[1:2, :])
    r = _dot(h.astype(BF), w_ref[...])
    u_ref[...] = r[:, 0:D_S5]
    q_ref[...] = r[:, D_S5:D_S5 + D_NA] * (NA_HD ** -0.5)
    k_ref[...] = r[:, D_S5 + D_NA:D_S5 + 2 * D_NA]
    v_ref[...] = r[:, D_S5 + 2 * D_NA:D_S5 + 3 * D_NA]


def _even_in(st, xp, xs, mods, g, w_in):
    n_out = w_in.shape[1]
    out = jax.ShapeDtypeStruct((st.total, D_S5), F32)
    return pl.pallas_call(
        functools.partial(_even_in_kernel, p_steps=st.p_steps),
        grid=(st.steps,),
        in_specs=[st.prompt_spec(D_MODEL), st.sample_spec(D_MODEL), st.mod_spec(), _resident((1, D_MODEL)),
                  _resident((D_MODEL, n_out))],
        out_specs=[st.row_spec(D_S5)] * 4,
        out_shape=[out] * 4,
        compiler_params=_params("parallel"),
        name="even_in",
    )(xp, xs, mods, g, w_in)


def _ctx_attn_kernel(q_ref, k_ref, v_ref, o_ref):
    q = q_ref[...]
    k = k_ref[...].astype(BF)
    v = v_ref[...].astype(BF)
    lane = lax.broadcasted_iota(jnp.int32, q.shape, 1)
    outs = []
    for h in range(2):
        in_head = (lane >= NA_HD * h) & (lane < NA_HD * (h + 1))
        qh = jnp.where(in_head, q, 0.0).astype(BF)
        s = _dot_nt(qh, k)
        p = jnp.exp(s - jnp.max(s, axis=-1, keepdims=True))
        l = jnp.sum(p, axis=-1, keepdims=True)
        outs.append(_dot(p.astype(BF), v) / l)
    o_ref[...] = jnp.where(lane < NA_HD, outs[0], outs[1])


def _ctx_attn(q, k, v, n_seq, seq):
    spec = pl.BlockSpec((seq, LANES), lambda b, p: (b, p))
    return pl.pallas_call(
        _ctx_attn_kernel,
        grid=(n_seq, D_NA // LANES),
        in_specs=[spec, spec, spec],
        out_specs=spec,
        out_shape=jax.ShapeDtypeStruct((n_seq * seq, D_NA), F32),
        compiler_params=_params("parallel", "parallel"),
        name="ctx_attn",
    )(q, k, v)


def _bias_kernel(rpb_ref, o_ref):
    h = pl.program_id(0)
    n_dr = 2 * NA_WIN_ROWS - 1
    n_dc = 2 * NA_WIN_COLS - 1
    wq = lax.broadcasted_iota(jnp.int32, (GRID_W, LANES), 0)
    lane = lax.broadcasted_iota(jnp.int32, (GRID_W, LANES), 1)
    wk = lane & (GRID_W - 1)
    left = lane < GRID_W
    col_start = jnp.clip(wq - NA_WIN_COLS // 2, 0, GRID_W - NA_WIN_COLS)
    ok = (wk >= col_start) & (wk < col_start + NA_WIN_COLS)
    dc = jnp.clip(wk - wq, -(NA_WIN_COLS - 1), NA_WIN_COLS - 1) + (NA_WIN_COLS - 1)
    for dr in range(n_dr - 1):
        acc = jnp.zeros((GRID_W, LANES), F32)
        for kk in range(n_dc):
            lo = rpb_ref[(h * n_dr + dr) * n_dc + kk]
            hi = rpb_ref[(h * n_dr + dr + 1) * n_dc + kk]
            acc = jnp.where(dc == kk, jnp.where(left, lo, hi), acc)
        o_ref[dr] = jnp.where(ok, acc, NEG_BIG)


def _bias_table(rpb):
    n_dr = 2 * NA_WIN_ROWS - 1
    return pl.pallas_call(
        _bias_kernel,
        grid=(NA_HEADS,),
        in_specs=[pl.BlockSpec(memory_space=pltpu.SMEM)],
        out_specs=pl.BlockSpec((None, n_dr - 1, GRID_W, LANES), lambda h: (h, 0, 0, 0)),
        out_shape=jax.ShapeDtypeStruct((NA_HEADS, n_dr - 1, GRID_W, LANES), F32),
        compiler_params=_params("parallel"),
        name="na_bias",
    )(rpb.reshape(-1))


def _na_kernel(q_ref, k_ref, v_ref, ck_ref, cv_ref, bias_ref, o_ref, kb_ref, vb_ref, *, rows):
    kb_ref[...] = k_ref[...].astype(BF)
    vb_ref[...] = v_ref[...].astype(BF)
    ck = ck_ref[...]
    cv = cv_ref[...]
    kh = NA_WIN_ROWS
    win = kh * GRID_W
    lane = lax.broadcasted_iota(jnp.int32, (GRID_W, LANES), 1)

    def body(r, carry):
        start = jnp.clip(r - kh // 2, 0, rows - kh)
        dr0 = start - r + (NA_WIN_ROWS - 1)
        q = q_ref[pl.ds(pl.multiple_of(r * GRID_W, GRID_W), GRID_W), :]
        koff = pl.multiple_of(start * GRID_W, GRID_W)
        kw = kb_ref[pl.ds(koff, win), :]
        vw = vb_ref[pl.ds(koff, win), :]
        outs = []
        for h in range(2):
            in_head = (lane >= NA_HD * h) & (lane < NA_HD * (h + 1))
            qh = jnp.where(in_head, q, 0.0).astype(BF)
            bias = jnp.concatenate([bias_ref[h, dr0 + 2 * i] for i in range(kh // 2)], axis=1)
            s_loc = _dot_nt(qh, kw) + bias
            s_ctx = _dot_nt(qh, ck)
            m = jnp.maximum(jnp.max(s_loc, axis=-1, keepdims=True), jnp.max(s_ctx, axis=-1, keepdims=True))
            p_loc = jnp.exp(s_loc - m)
            p_ctx = jnp.exp(s_ctx - m)
            l = jnp.sum(p_loc, axis=-1, keepdims=True) + jnp.sum(p_ctx, axis=-1, keepdims=True)
            outs.append((_dot(p_loc.astype(BF), vw) + _dot(p_ctx.astype(BF), cv)) / l)
        o_ref[pl.ds(pl.multiple_of(r * GRID_W, GRID_W), GRID_W), :] = jnp.where(lane < NA_HD, outs[0], outs[1])
        return carry

    lax.fori_loop(0, rows, body, 0, unroll=NA_UNROLL)


def _na_attn(q, k, v, ck, cv, bias, n_seq, seq, row_base):
    rows = seq // GRID_W
    assert rows >= NA_WIN_ROWS
    past = ck.shape[1]
    spec = pl.BlockSpec((seq, LANES), lambda b, p: (row_base + b, p))
    cspec = pl.BlockSpec((None, past, LANES), lambda b, p: (b, 0, p))
    n_dr = 2 * NA_WIN_ROWS - 2
    return pl.pallas_call(
        functools.partial(_na_kernel, rows=rows),
        grid=(n_seq, D_NA // LANES),
        in_specs=[spec, spec, spec, cspec, cspec,
                  pl.BlockSpec((2, n_dr, GRID_W, LANES), lambda b, p: (p, 0, 0, 0))],
        out_specs=pl.BlockSpec((seq, LANES), lambda b, p: (b, p)),
        out_shape=jax.ShapeDtypeStruct((n_seq * seq, D_NA), F32),
        scratch_shapes=[pltpu.VMEM((seq, LANES), BF), pltpu.VMEM((seq, LANES), BF)],
        compiler_params=_params("parallel", "parallel"),
        name="na_attn",
    )(q, k, v, ck, cv, bias)


S5_BLK = LANES // S5_P
S5_NB = S5_G // S5_BLK
S5_ST = S5_BLK * S5_N
S5_W = S5_Q * LANES


def _cexp(zr, zi):
    e = jnp.exp(zr)
    return e * jnp.cos(zi), e * jnp.sin(zi)


def _cmul(ar, ai, br, bi):
    return ar * br - ai * bi, ar * bi + ai * br


def _s5_prep_kernel(arr_ref, ari_ref, ldr_ref, acr_ref, aci_ref, ldc_ref, btr_ref, bti_ref, ctr_ref, cti_ref,
                    v_ref, bw_ref, cw_ref, sc_ref,
                    bbr_s, bbi_s, ccr_s, cci_s, pwr_s, pwi_s, pcr_s, pci_s, p0_s, p1_s):
    t = pl.program_id(1)
    q = S5_Q
    nst = S5_ST

    @pl.when(t == 0)
    def _():
        expand = jnp.where(lax.broadcasted_iota(jnp.int32, (S5_P, 2 * LANES), 0)
                           == (lax.broadcasted_iota(jnp.int32, (S5_P, 2 * LANES), 1) & (S5_P - 1)), 1.0, 0.0)
        row_g = lax.broadcasted_iota(jnp.int32, (LANES, nst), 0) >> 4
        col_g = lax.broadcasted_iota(jnp.int32, (LANES, nst), 1) >> 6
        st_g = lax.broadcasted_iota(jnp.int32, (nst, LANES), 0) >> 6
        ch_g = lax.broadcasted_iota(jnp.int32, (nst, LANES), 1) >> 4
        lane_d = lax.broadcasted_iota(jnp.int32, (nst, LANES), 1) >> 4
        taps = []
        for d in range(2):
            ar = jnp.minimum(arr_ref[d], -S5_MIN_DECAY)
            ai = ari_ref[d]
            dt = jnp.exp(ldr_ref[d])
            abr, abi = _cexp(ar * dt, ai * dt)
            den = ar * ar + ai * ai
            cfr = ((abr - 1.0) * ar + abi * ai) / den
            cfi = (abi * ar - (abr - 1.0) * ai) / den
            bbr, bbi = _cmul(cfr, cfi, btr_ref[d], bti_ref[d])
            bbr = jnp.where(row_g == col_g, jnp.concatenate([bbr] * S5_BLK, axis=0), 0.0)
            bbi = jnp.where(row_g == col_g, jnp.concatenate([bbi] * S5_BLK, axis=0), 0.0)
            bbr_s[d] = bbr
            bbi_s[d] = bbi
            pr = jnp.ones((1, nst), F32)
            pi = jnp.zeros((1, nst), F32)
            for e in range(q + 1):
                pwr_s[d, e] = pr
                pwi_s[d, e] = pi
                pr, pi = _cmul(pr, pi, abr, abi)
            sc_ref[2 * d:2 * d + 1, :] = pwr_s[d, q]
            sc_ref[2 * d + 1:2 * d + 2, :] = pwi_s[d, q]
            arc = jnp.minimum(acr_ref[d], -S5_MIN_DECAY)
            dtc = jnp.exp(ldc_ref[d])
            acr, aci = _cexp(jnp.broadcast_to(arc * dtc, (nst, LANES)), jnp.broadcast_to(aci_ref[d] * dtc, (nst, LANES)))
            pr = jnp.ones((nst, LANES), F32)
            pi = jnp.zeros((nst, LANES), F32)
            for e in range(q + 1):
                pcr_s[d, e] = pr
                pci_s[d, e] = pi
                pr, pi = _cmul(pr, pi, acr, aci)
            cxr = _dot_hi(ctr_ref[d], expand)
            cxi = _dot_hi(cti_ref[d], expand)
            ccr_s[d] = jnp.where(st_g == ch_g, cxr[:, 0:LANES], 0.0)
            cci_s[d] = jnp.where(st_g == ch_g, cxi[:, 0:LANES], 0.0)
            pws_r, pws_i = [], []
            for k in range(2):
                sel_r = jnp.zeros((nst, LANES), F32)
                sel_i = jnp.zeros((nst, LANES), F32)
                for j in range(S5_BLK):
                    dl = S5_BLK * k + j
                    e = dl if d == 0 else q - 1 - dl
                    sel_r = jnp.where(lane_d == j, pcr_s[d, e], sel_r)
                    sel_i = jnp.where(lane_d == j, pci_s[d, e], sel_i)
                pws_r.append(sel_r)
                pws_i.append(sel_i)
            ggr, ggi = _cmul(cxr, cxi, jnp.concatenate(pws_r, axis=1), jnp.concatenate(pws_i, axis=1))
            taps.append(_dot_hi(bbr, ggr) - _dot_hi(bbi, ggi))
        lane2 = lax.broadcasted_iota(jnp.int32, (LANES, 2 * LANES), 1)
        kt0 = taps[0] + jnp.where(lane2 < S5_P, pltpu.roll(taps[1], S5_P, 1), 0.0)
        kt1 = jnp.where(lane2 >= 2 * LANES - S5_P, 0.0, taps[1])
        er = lax.broadcasted_iota(jnp.int32, (2 * LANES, S5_W), 0)
        ec = lax.broadcasted_iota(jnp.int32, (2 * LANES, S5_W), 1)
        place = jnp.where(((er >> 4) == (ec >> 7)) & ((er & (S5_P - 1)) == (ec & (S5_P - 1))), 1.0, 0.0).astype(BF)
        own = (lax.broadcasted_iota(jnp.int32, (LANES, S5_W), 0) >> 4) == (
            (lax.broadcasted_iota(jnp.int32, (LANES, S5_W), 1) >> 4) & (S5_BLK - 1))
        zeros = jnp.zeros((LANES, S5_W), BF)
        p0_s[:, 0:S5_W] = zeros
        p0_s[:, S5_W:2 * S5_W] = jnp.where(own, _dot(kt0.astype(BF), place), 0.0).astype(BF)
        p1_s[:, 0:S5_W] = jnp.where(own, _dot(kt1.astype(BF), place), 0.0).astype(BF)
        p1_s[:, S5_W:2 * S5_W] = zeros

    off0 = pl.multiple_of(S5_W - LANES * t, LANES)
    off1 = pl.multiple_of(LANES * (q - 1 - t), LANES)
    v_ref[...] = p0_s[:, pl.ds(off0, S5_W)] + p1_s[:, pl.ds(off1, S5_W)]
    for d in range(2):
        e_b = (q - 1 - t) if d == 0 else t
        br, bi = _cmul(pwr_s[d, e_b], pwi_s[d, e_b], bbr_s[d], bbi_s[d])
        bw_ref[:, 2 * nst * d:2 * nst * d + nst] = br.astype(BF)
        bw_ref[:, 2 * nst * d + nst:2 * nst * (d + 1)] = bi.astype(BF)
        e_c = (t + 1) if d == 0 else (q - t)
        gr, gi = _cmul(ccr_s[d], cci_s[d], pcr_s[d, e_c], pci_s[d, e_c])
        cw_ref[2 * nst * d:2 * nst * d + nst, :] = gr.astype(BF)
        cw_ref[2 * nst * d + nst:2 * nst * (d + 1), :] = (-gi).astype(BF)


def _s5_prep(a_re, a_im, log_dt, b_re, b_im, c_re, c_im):
    nst, nb, q = S5_ST, S5_NB, S5_Q
    row = lambda t: t.reshape(2, nb, 1, nst)
    col = lambda t: t.reshape(2, nb, nst, 1)
    ld = jnp.broadcast_to(log_dt[:, :, None], (2, S5_G, S5_N))
    bt = lambda t: t.reshape(2, nb, S5_BLK, S5_N, S5_P).transpose(0, 1, 4, 2, 3).reshape(2, nb, S5_P, nst)
    ct = lambda t: t.reshape(2, nb, S5_BLK, S5_P, S5_N).transpose(0, 1, 2, 4, 3).reshape(2, nb, nst, S5_P)
    rspec = pl.BlockSpec((2, None, 1, nst), lambda b, t: (0, b, 0, 0))
    cspec = pl.BlockSpec((2, None, nst, 1), lambda b, t: (0, b, 0, 0))
    btspec = pl.BlockSpec((2, None, S5_P, nst), lambda b, t: (0, b, 0, 0))
    ctspec = pl.BlockSpec((2, None, nst, S5_P), lambda b, t: (0, b, 0, 0))
    big = jax.ShapeDtypeStruct((nb, S5_W, S5_W), BF)
    return pl.pallas_call(
        _s5_prep_kernel,
        grid=(nb, q),
        in_specs=[rspec, rspec, rspec, cspec, cspec, cspec, btspec, btspec, ctspec, ctspec],
        out_specs=[pl.BlockSpec((None, LANES, S5_W), lambda b, t: (b, t, 0)),
                   pl.BlockSpec((None, LANES, S5_W), lambda b, t: (b, t, 0)),
                   pl.BlockSpec((None, S5_W, LANES), lambda b, t: (b, 0, t)),
                   pl.BlockSpec((None, 4, nst), lambda b, t: (b, 0, 0))],
        out_shape=[big, big, big, jax.ShapeDtypeStruct((nb, 4, nst), F32)],
        scratch_shapes=[pltpu.VMEM((2, LANES, nst), F32), pltpu.VMEM((2, LANES, nst), F32),
                        pltpu.VMEM((2, nst, LANES), F32), pltpu.VMEM((2, nst, LANES), F32),
                        pltpu.VMEM((2, q + 1, 1, nst), F32), pltpu.VMEM((2, q + 1, 1, nst), F32),
                        pltpu.VMEM((2, q + 1, nst, LANES), F32), pltpu.VMEM((2, q + 1, nst, LANES), F32),
                        pltpu.VMEM((LANES, 2 * S5_W), BF), pltpu.VMEM((LANES, 2 * S5_W), BF)],
        compiler_params=_params("parallel", "arbitrary"),
        name="s5_prep",
    )(row(a_re), row(a_im), row(ld), col(a_re), col(a_im), col(ld), bt(b_re), bt(b_im), ct(c_re), ct(c_im))


def _s5_kernel(u_ref, v_ref, bw_ref, cw_ref, sc_ref, h0_ref, y_ref, fin_ref, z_s, h_s, *, p_steps, n_sub, mp):
    s = pl.program_id(1)
    m = u_ref.shape[0]
    n_tiles = 4 * S5_ST // LANES
    per = S5_ST // LANES
    tiles = lambda x: [x[:, LANES * k:LANES * (k + 1)] for k in range(x.shape[1] // LANES)]
    ucat = jnp.concatenate([u_ref[:, t, :].astype(BF) for t in range(S5_Q)], axis=1)
    for k, zk in enumerate(tiles(_dot(ucat, bw_ref[...]))):
        z_s[k] = zk
    sc = sc_ref[...]
    mult = [tiles(sc[i:i + 1]) for i in range(4)]

    def step(carry, z):
        new = [None] * n_tiles
        for d in range(2):
            for j in range(per):
                re, im = 2 * per * d + j, 2 * per * d + per + j
                nr, ni = _cmul(mult[2 * d][j], mult[2 * d + 1][j], carry[re], carry[im])
                new[re] = nr + z[re]
                new[im] = ni + z[im]
        return tuple(new)

    def visit(carry, rf, rb):
        z = []
        for k in range(n_tiles):
            rows = rf if k < 2 * per else rb
            h_s[k, rows, :] = carry[k]
            z.append(z_s[k, rows, :])
        return step(carry, z)

    @pl.when(s < p_steps)
    def _():
        carry = tuple(jnp.zeros((n_sub, LANES), F32) for _ in range(n_tiles))
        for c in range(mp):
            carry = visit(carry, pl.ds(c, n_sub, stride=mp), pl.ds(mp - 1 - c, n_sub, stride=mp))
        for i in range(4):
            fin_ref[:, i, :] = jnp.concatenate(carry[per * i:per * (i + 1)], axis=1)

    @pl.when(s >= p_steps)
    def _():
        h0 = h0_ref[...]
        init = tuple(t for i in range(4) for t in tiles(h0[i:i + 1]))
        lax.fori_loop(0, m, lambda c, carry: visit(carry, pl.ds(c, 1), pl.ds(m - 1 - c, 1)), init)

    hcat = jnp.concatenate([h_s[k] for k in range(n_tiles)], axis=1).astype(BF)
    ycat = _dot(ucat, v_ref[...]) + _dot(hcat, cw_ref[...])
    for t in range(S5_Q):
        y_ref[:, t, :] = ycat[:, LANES * t:LANES * (t + 1)]


def _s5(u, v, bw, cw, sc, h0, n_p, seq_p, seq_s):
    total = u.shape[0]
    q = S5_Q
    m = seq_s // q
    mp = seq_p // q
    n_sub = seq_s // seq_p
    p_steps = n_p // seq_s
    n_prompt = n_p // seq_p
    wspec = pl.BlockSpec((None, S5_W, S5_W), lambda b, s: (b, 0, 0), pipeline_mode=pl.Buffered(1))
    tok = pl.BlockSpec((m, q, LANES), lambda b, s: (s, 0, b))
    y, fin = pl.pallas_call(
        functools.partial(_s5_kernel, p_steps=p_steps, n_sub=n_sub, mp=mp),
        grid=(S5_NB, total // seq_s),
        in_specs=[tok, wspec, wspec, wspec,
                  pl.BlockSpec((None, 4, S5_ST), lambda b, s: (b, 0, 0)),
                  pl.BlockSpec((None, 4, S5_ST), lambda b, s: (jnp.maximum(s - p_steps, 0), 0, b))],
        out_specs=[tok, pl.BlockSpec((n_sub, 4, S5_ST), lambda b, s: (jnp.minimum(s, p_steps - 1), 0, b))],
        out_shape=[jax.ShapeDtypeStruct((total // q, q, D_S5), F32),
                   jax.ShapeDtypeStruct((n_prompt, 4, S5_G * S5_N), F32)],
        scratch_shapes=[pltpu.VMEM((4 * S5_ST // LANES, m, LANES), F32)] * 2,
        compiler_params=_params("parallel", "arbitrary"),
        name="s5_scan",
    )(u.reshape(total // q, q, D_S5), v, bw, cw, sc, h0)
    return y.reshape(total, D_S5), fin


def _state_planes(re, im):
    b = re.shape[0]
    return jnp.stack([t[:, d].reshape(b, S5_G * S5_N) for d in range(2) for t in (re, im)], axis=1)


def _planes_state(fin):
    t = fin.reshape(fin.shape[0], 2, 2, S5_G, S5_N)
    return t[:, :, 0], t[:, :, 1]


def _gelu_tanh(x):
    return 0.5 * x * (1.0 + jnp.tanh(math.sqrt(2.0 / math.pi) * (x + 0.044715 * (x * x * x))))


def _even_out_kernel(xp_ref, xs_ref, mod_ref, y_ref, u_ref, ap_ref, as_ref, dsk_ref, wglu_ref, wout_ref,
                     gf_ref, wg_ref, wu_ref, wd_ref, o_ref, *, p_steps):
    is_p = pl.program_id(0) < p_steps
    attn = jnp.where(is_p, ap_ref[...], as_ref[...])
    y = _gelu_tanh(y_ref[...] + u_ref[...] * dsk_ref[...])
    y = y * _sigmoid(_dot(y.astype(BF), wglu_ref[...]))
    out = _dot(y.astype(BF), wout_ref[0:D_S5, :]) + _dot(attn.astype(BF), wout_ref[D_S5:D_S5 + D_NA, :])
    x1 = jnp.where(is_p, xp_ref[...], xs_ref[...]) + mod_ref[2:3, :] * out
    o_ref[...] = _ffn(x1, mod_ref, gf_ref, wg_ref, wu_ref, wd_ref)


def _even_out(st, xp, xs, mods, y, u, attn_p, attn_s, d_skip, w_glu, w_out, g_ffn, wg, wu, wd):
    return pl.pallas_call(
        functools.partial(_even_out_kernel, p_steps=st.p_steps),
        grid=(st.steps,),
        in_specs=[st.prompt_spec(D_MODEL), st.sample_spec(D_MODEL), st.mod_spec(), st.row_spec(D_S5),
                  st.row_spec(D_S5), st.prompt_spec(D_NA), st.sample_spec(D_NA),
                  _resident((1, D_S5)), _resident(w_glu.shape), _resident(w_out.shape),
                  _resident((1, D_MODEL)), _resident(wg.shape), _resident(wu.shape), _resident(wd.shape)],
        out_specs=st.row_spec(D_MODEL),
        out_shape=jax.ShapeDtypeStruct((st.total, D_MODEL), F32),
        compiler_params=_params("parallel"),
        name="even_out_ffn",
    )(xp, xs, mods, y, u, attn_p, attn_s, d_skip, w_glu, w_out, g_ffn, wg, wu, wd)


def _odd_in_kernel(x_ref, mod_ref, g_ref, wz_ref, wx_ref, wdt_ref, wdtt_ref, z_ref, xbc_ref, dt_ref, dtt_ref):
    h = _norm_mod(x_ref[...], g_ref[...], mod_ref[0:1, :], mod_ref[1:2, :]).astype(BF)
    z_ref[...] = _dot(h, wz_ref[...])
    xbc_ref[...] = _dot(h, wx_ref[...])
    dt_ref[...] = _dot(h, wdt_ref[...])
    dtt_ref[...] = _dot_nt(wdtt_ref[...], h)


def _odd_in(st, x, mods, g, wz, wx, wdt):
    wdtt = wdt.T
    return pl.pallas_call(
        _odd_in_kernel,
        grid=(st.steps,),
        in_specs=[st.row_spec(D_MODEL), st.mod_spec(), _resident((1, D_MODEL)),
                  _resident(wz.shape), _resident(wx.shape), _resident(wdt.shape), _resident(wdtt.shape)],
        out_specs=[st.row_spec(D_INNER), st.row_spec(SSD_CONV_DIM), st.row_spec(2 * SSD_H),
                   pl.BlockSpec((2 * SSD_H, st.tm), lambda i: (0, i))],
        out_shape=[jax.ShapeDtypeStruct((st.total, D_INNER), F32),
                   jax.ShapeDtypeStruct((st.total, SSD_CONV_DIM), F32),
                   jax.ShapeDtypeStruct((st.total, 2 * SSD_H), F32),
                   jax.ShapeDtypeStruct((2 * SSD_H, st.total), F32)],
        compiler_params=_params("parallel"),
        name="odd_in",
    )(x, mods, g, wz, wx, wdt, wdtt)


def _conv_kernel(x_ref, prev_ref, next_ref, w_ref, b_ref, xc_ref, *, lt, n_p, seq_p, seq_s):
    tok = pl.program_id(0) * lt
    in_p = tok < n_p
    pos = jnp.where(in_p, tok % seq_p, (tok - n_p) % seq_s)
    seq = jnp.where(in_p, seq_p, seq_s)
    prev = jnp.where(pos == 0, 0.0, prev_ref[...])
    nxt = jnp.where(pos + lt == seq, 0.0, next_ref[...])
    ext = jnp.concatenate([prev, x_ref[...], nxt], axis=0)
    halo = prev.shape[0]
    acc = b_ref[...] + jnp.zeros((lt, x_ref.shape[1]), F32)
    for kk in range(SSD_CONV):
        off = halo + kk - SSD_CONV // 2
        acc = acc + w_ref[kk:kk + 1, :] * ext[off:off + lt, :]
    xc_ref[...] = _silu(acc)


def _conv(xbc, conv_w, conv_b, n_p, seq_p, seq_s):
    total, c = xbc.shape
    lt = math.gcd(seq_p, 256)
    halo = 8
    nblk = total // halo
    per = lt // halo
    return pl.pallas_call(
        functools.partial(_conv_kernel, lt=lt, n_p=n_p, seq_p=seq_p, seq_s=seq_s),
        grid=(total // lt,),
        in_specs=[pl.BlockSpec((lt, c), lambda i: (i, 0)),
                  pl.BlockSpec((halo, c), lambda i: (jnp.maximum(i * per - 1, 0), 0)),
                  pl.BlockSpec((halo, c), lambda i: (jnp.minimum((i + 1) * per, nblk - 1), 0)),
                  _resident(conv_w.shape), _resident((1, c))],
        out_specs=pl.BlockSpec((lt, c), lambda i: (i, 0)),
        out_shape=jax.ShapeDtypeStruct((total, c), F32),
        compiler_params=_params("parallel"),
        name="ssd_conv",
    )(xbc, xbc, xbc, conv_w, conv_b.reshape(1, c))


def _ssd_kernel(xs_ref, b_ref, c_ref, dt_ref, dtt_ref, alr_ref, alc_ref, dbr_ref, dbc_ref, dsk_ref, h0_ref,
                y_ref, fin_ref, st_ref, *, n_chunks, n_p, seq_p, seq_s):
    q = SSD_Q
    d = pl.program_id(0)
    ci = pl.program_id(1)
    blocks = D_INNER // LANES
    tok = jnp.where(d == 0, ci, n_chunks - 1 - ci) * q
    in_p = tok < n_p
    pos = jnp.where(in_p, tok % seq_p, (tok - n_p) % seq_s)
    first = pos == 0
    last = pos + q == jnp.where(in_p, seq_p, seq_s)
    start = jnp.where(d == 0, first, last)
    end = jnp.where(d == 0, last, first)

    @pl.when(jnp.logical_and(start, in_p))
    def _():
        st_ref[...] = jnp.zeros(st_ref.shape, F32)

    @pl.when(jnp.logical_and(start, jnp.logical_not(in_p)))
    def _():
        h0 = h0_ref[...].reshape(D_INNER, SSD_N)
        for kb in range(blocks):
            st_ref[:, LANES * kb:LANES * (kb + 1)] = h0[LANES * kb:LANES * (kb + 1), :].T

    skip_on = jnp.where(d == 0, 1.0, 0.0)
    sgn = 1 - 2 * d
    li = lax.broadcasted_iota(jnp.int32, (q, q), 0)
    si = lax.broadcasted_iota(jnp.int32, (q, q), 1)
    causal = (li - si) * sgn >= 0
    tri_l = jnp.where(causal, 1.0, 0.0)
    tri_r = jnp.where((si - li) * sgn >= 0, 1.0, 0.0)
    a_row = -jnp.exp(alr_ref[...])
    a_col = -jnp.exp(alc_ref[...])
    dt_both = dt_ref[...]
    dt_col = _softplus(jnp.where(d == 0, dt_both[:, 0:SSD_H], dt_both[:, SSD_H:2 * SSD_H]) + dbr_ref[...])
    dt_row = _softplus(dtt_ref[...] + dbc_ref[...])
    cs_col = _dot_hi(tri_l, dt_col * a_row)
    da_row = dt_row * a_col
    cs_row = _dot_hi(da_row, tri_r)
    tot = jnp.sum(da_row, axis=-1, keepdims=True)
    w_row = dt_row * jnp.exp(tot - cs_row)
    etot = jnp.exp(tot)
    lane = lax.broadcasted_iota(jnp.int32, (q, LANES), 1)
    heads_per_group = SSD_H // SSD_G
    for g in range(SSD_G):
        bg = b_ref[:, SSD_N * g:SSD_N * (g + 1)]
        cg = c_ref[:, SSD_N * g:SSD_N * (g + 1)]
        cb = _dot_nt(cg.astype(BF), bg.astype(BF))
        bgt = bg.T
        for jp in range(heads_per_group // 2):
            pi = g * (heads_per_group // 2) + jp
            cols = slice(LANES * pi, LANES * (pi + 1))
            xp = xs_ref[:, cols]
            sp = st_ref[:, cols]
            ypair = jnp.zeros((q, LANES), F32)
            upd = jnp.zeros((SSD_N, LANES), F32)
            for hh in range(2):
                h = 2 * pi + hh
                mine = (lane >= SSD_P * hh) & (lane < SSD_P * (hh + 1))
                xm = jnp.where(mine, xp, 0.0).astype(BF)
                sm = jnp.where(mine, sp, 0.0).astype(BF)
                csc = jnp.broadcast_to(cs_col[:, h:h + 1], (q, q))
                decay = jnp.exp(jnp.where(causal, csc - cs_row[h:h + 1, :], -jnp.inf))
                m = (decay * cb * dt_row[h:h + 1, :]).astype(BF)
                ce = (cg * jnp.exp(csc)).astype(BF)
                ypair = ypair + _dot(m, xm) + _dot(ce, sm)
                bw = (bgt * w_row[h:h + 1, :]).astype(BF)
                upd = upd + _dot(bw, xm)
            keep = jnp.where(lane < SSD_P, etot[2 * pi:2 * pi + 1, :], etot[2 * pi + 1:2 * pi + 2, :])
            st_ref[:, cols] = keep * sp + upd
            y_ref[:, cols] = ypair + (skip_on * dsk_ref[:, cols]) * xp

    @pl.when(jnp.logical_and(end, in_p))
    def _():
        for kb in range(blocks):
            t = st_ref[:, LANES * kb:LANES * (kb + 1)].T
            fin_ref[2 * kb:2 * kb + 2] = t.reshape(2, SSD_P, SSD_N)


def _ssd(xc, dt_raw, dtt_raw, dt_bias, a_log, d_skip, h0, n_p, seq_p, seq_s):
    q = SSD_Q
    total = xc.shape[0]
    n_chunks = total // q
    n_prompt = n_p // seq_p
    n_sample = h0.shape[0]

    def blk(d, c):
        return jnp.where(d == 0, c, n_chunks - 1 - c)

    def h0_idx(d, c):
        return (jnp.clip((blk(d, c) * q - n_p) // seq_s, 0, n_sample - 1), d, 0, 0, 0)

    def fin_idx(d, c):
        return (jnp.minimum(blk(d, c) * q // seq_p, n_prompt - 1), d, 0, 0, 0)

    state_block = (None, None, SSD_H, SSD_P, SSD_N)
    return pl.pallas_call(
        functools.partial(_ssd_kernel, n_chunks=n_chunks, n_p=n_p, seq_p=seq_p, seq_s=seq_s),
        grid=(2, n_chunks),
        in_specs=[pl.BlockSpec((q, D_INNER), lambda d, c: (blk(d, c), 0)),
                  pl.BlockSpec((q, SSD_GN), lambda d, c: (blk(d, c), D_INNER // SSD_GN)),
                  pl.BlockSpec((q, SSD_GN), lambda d, c: (blk(d, c), D_INNER // SSD_GN + 1)),
                  pl.BlockSpec((q, 2 * SSD_H), lambda d, c: (blk(d, c), 0)),
                  pl.BlockSpec((SSD_H, q), lambda d, c: (d, blk(d, c))),
                  pl.BlockSpec((None, 1, SSD_H), lambda d, c: (d, 0, 0)),
                  pl.BlockSpec((None, SSD_H, 1), lambda d, c: (d, 0, 0)),
                  pl.BlockSpec((None, 1, SSD_H), lambda d, c: (d, 0, 0)),
                  pl.BlockSpec((None, SSD_H, 1), lambda d, c: (d, 0, 0)),
                  _resident((1, D_INNER)),
                  pl.BlockSpec(state_block, h0_idx)],
        out_specs=[pl.BlockSpec((None, q, D_INNER), lambda d, c: (d, blk(d, c), 0)),
                   pl.BlockSpec(state_block, fin_idx)],
        out_shape=[jax.ShapeDtypeStruct((2, total, D_INNER), F32),
                   jax.ShapeDtypeStruct((n_prompt, 2, SSD_H, SSD_P, SSD_N), F32)],
        scratch_shapes=[pltpu.VMEM((SSD_N, D_INNER), F32)],
        compiler_params=_params("parallel", "arbitrary"),
        name="ssd_scan",
    )(xc, xc, xc, dt_raw, dtt_raw, a_log.reshape(2, 1, SSD_H), a_log.reshape(2, SSD_H, 1),
      dt_bias.reshape(2, 1, SSD_H), dt_bias.reshape(2, SSD_H, 1), d_skip, h0)


def _odd_out_kernel(x_ref, mod_ref, yf_ref, yb_ref, z_ref, ng_ref, wout_ref,
                    gf_ref, wg_ref, wu_ref, wd_ref, fg_ref, *o_refs, p_steps, final):
    y = (yf_ref[...] + yb_ref[...]) * _silu(z_ref[...])
    y = _rms(y, ng_ref[...])
    x1 = x_ref[...] + mod_ref[2:3, :] * _dot(y.astype(BF), wout_ref[...])
    x2 = _ffn(x1, mod_ref, gf_ref, wg_ref, wu_ref, wd_ref)
    if not final:
        o_refs[0][...] = x2
        return
    out = _rms(x2, fg_ref[...])
    is_p = pl.program_id(0) < p_steps

    @pl.when(is_p)
    def _():
        o_refs[0][...] = out

    @pl.when(jnp.logical_not(is_p))
    def _():
        o_refs[1][...] = out


def _odd_out(st, x, mods, y, z, norm_g, w_out, g_ffn, wg, wu, wd, final_g, final):
    ydir = lambda d: pl.BlockSpec((None, st.tm, D_INNER), lambda i: (d, i, 0))
    return pl.pallas_call(
        functools.partial(_odd_out_kernel, p_steps=st.p_steps, final=final),
        grid=(st.steps,),
        in_specs=[st.row_spec(D_MODEL), st.mod_spec(), ydir(0), ydir(1), st.row_spec(D_INNER),
                  _resident((1, D_INNER)), _resident(w_out.shape),
                  _resident((1, D_MODEL)), _resident(wg.shape), _resident(wu.shape), _resident(wd.shape),
                  _resident((1, D_MODEL))],
        out_specs=[st.prompt_spec(D_MODEL), st.sample_spec(D_MODEL)] if final else st.row_spec(D_MODEL),
        out_shape=([jax.ShapeDtypeStruct((st.n_p, D_MODEL), F32),
                    jax.ShapeDtypeStruct((st.total - st.n_p, D_MODEL), F32)] if final
                   else jax.ShapeDtypeStruct((st.total, D_MODEL), F32)),
        compiler_params=_params("arbitrary"),
        name="odd_out_ffn",
    )(x, mods, y, y, z, norm_g, w_out, g_ffn, wg, wu, wd, final_g)


def kernel(x_prompt, x_sample, cache_na_k, cache_na_v, state_s5_re, state_s5_im, state_ssd, c, c_ctx, norm_mix_g, norm_ffn_g, ada_w, ada_b, ffn_w_gate, ffn_w_up, ffn_w_down, ev_w_in, ev_w_out, s5_a_re, s5_a_im, s5_log_dt, s5_b_re, s5_b_im, s5_c_re, s5_c_im, s5_d, s5_w_glu, na_rpb, od_w_in, od_conv_w, od_conv_b, ssd_a_log, ssd_dt_bias, ssd_d, ssd_norm_g, od_w_out, final_norm_g):
    bp, seq_p, d = x_prompt.shape
    bs, seq_s, _ = x_sample.shape
    depth = ada_w.shape[0]
    n_p = bp * seq_p
    assert d == D_MODEL and n_p % seq_s == 0
    st = _Stream(n_p, seq_s, bs, tm=math.gcd(512, math.gcd(n_p, seq_s)))
    st_odd = _Stream(n_p, seq_s, bs, tm=math.gcd(256, math.gcd(n_p, seq_s)))

    xp = x_prompt.reshape(n_p, d)
    xs = x_sample.reshape(bs * seq_s, d)
    cond = jnp.concatenate([c_ctx[None, :], c, jnp.zeros((8 - 1 - bs, d), F32)], axis=0)
    mods = _ada(cond, ada_w, ada_b).reshape(depth, 8, ADA_CHUNKS, d)
    row = lambda t: t.reshape(1, -1)

    new_k, new_v, new_s5_re, new_s5_im, new_ssd = [], [], [], [], []
    for layer in range(depth):
        wg = ffn_w_gate[layer].astype(BF)
        wu = ffn_w_up[layer].astype(BF)
        wd = ffn_w_down[layer].astype(BF)
        g_mix = row(norm_mix_g[layer])
        g_ffn = row(norm_ffn_g[layer])
        if layer % 2 == 0:
            e = layer // 2
            if layer > 0:
                xp, xs = x[:n_p], x[n_p:]
            u, q, k, v = _even_in(st, xp, xs, mods[layer], g_mix, ev_w_in[e].astype(BF))
            attn_p = _ctx_attn(q, k, v, bp, seq_p)
            heads = lambda t: t[:n_p].reshape(bp, seq_p, NA_HEADS, NA_HD).transpose(0, 2, 1, 3)
            new_k.append(heads(k))
            new_v.append(heads(v))
            ctx = lambda t: t[:, e].transpose(0, 2, 1, 3).reshape(bs, -1, D_NA).astype(BF)
            attn_s = _na_attn(q, k, v, ctx(cache_na_k), ctx(cache_na_v), _bias_table(na_rpb[e]),
                              bs, seq_s, n_p // seq_s)
            v_op, bw, cw, sc = _s5_prep(s5_a_re[e], s5_a_im[e], s5_log_dt[e], s5_b_re[e], s5_b_im[e],
                                        s5_c_re[e], s5_c_im[e])
            y, fin = _s5(u, v_op, bw, cw, sc, _state_planes(state_s5_re[:, e], state_s5_im[:, e]),
                         n_p, seq_p, seq_s)
            fre, fim = _planes_state(fin)
            new_s5_re.append(fre)
            new_s5_im.append(fim)
            x = _even_out(st, xp, xs, mods[layer], y, u, attn_p, attn_s, row(s5_d[e]),
                          s5_w_glu[e].astype(BF), ev_w_out[e].astype(BF), g_ffn, wg, wu, wd)
        else:
            o = layer // 2
            w_in = od_w_in[o].astype(BF)
            z, xbc, dt_raw, dtt_raw = _odd_in(st, x, mods[layer], g_mix, w_in[:, :D_INNER],
                                              w_in[:, D_INNER:D_INNER + SSD_CONV_DIM],
                                              w_in[:, D_INNER + SSD_CONV_DIM:])
            xc = _conv(xbc, od_conv_w[o], od_conv_b[o], n_p, seq_p, seq_s)
            y, fin = _ssd(xc, dt_raw, dtt_raw, ssd_dt_bias[o], ssd_a_log[o], row(jnp.repeat(ssd_d[o], SSD_P)),
                          state_ssd[:, o], n_p, seq_p, seq_s)
            new_ssd.append(fin)
            x = _odd_out(st_odd, x, mods[layer], y, z, row(ssd_norm_g[o]), od_w_out[o].astype(BF),
                         g_ffn, wg, wu, wd, row(final_norm_g), layer == depth - 1)
    if depth % 2 == 1:
        raise NotImplementedError("final norm is fused into the last (odd) layer")
    y_prompt = x[0].reshape(bp, seq_p, d)
    y_sample = x[1].reshape(bs, seq_s, d)
    return (y_prompt, y_sample, jnp.stack(new_k, axis=1), jnp.stack(new_v, axis=1),
            jnp.stack(new_s5_re, axis=1), jnp.stack(new_s5_im, axis=1), jnp.stack(new_ssd, axis=1))
```

```python
import functools
import math

import jax
import jax.numpy as jnp
from jax import lax
from jax.experimental import pallas as pl
from jax.experimental.pallas import tpu as pltpu

F32 = jnp.float32
BF = jnp.bfloat16
HI = lax.Precision.HIGHEST

D_MODEL = 1024
EPS = 1e-6
ADA_CHUNKS = 6
GRID_W = 64
D_S5 = 512
S5_P = 16
S5_G = D_S5 // S5_P
S5_N = 64
S5_MIN_DECAY = 1e-4
S5_Q = 16
D_NA = 512
NA_HD = 64
NA_HEADS = D_NA // NA_HD
NA_WIN_ROWS = 8
NA_WIN_COLS = 16
NA_ROWS = 4
D_INNER = 2048
SSD_P = 64
SSD_H = D_INNER // SSD_P
SSD_G = 4
SSD_N = 128
SSD_Q = 128
SSD_CONV = 5
SSD_GN = SSD_G * SSD_N
SSD_CONV_DIM = D_INNER + 2 * SSD_GN
D_FF = 2816

V7X_VMEM_BYTES = 64 * 1024 * 1024
VMEM_LIMIT = V7X_VMEM_BYTES - 8 * 1024 * 1024
LANES = 128
NEG_BIG = -1e30


def _params(*sem):
    return pltpu.CompilerParams(dimension_semantics=sem, vmem_limit_bytes=VMEM_LIMIT)


def _resident(shape):
    nd = len(shape)
    return pl.BlockSpec(shape, lambda *_: (0,) * nd, pipeline_mode=pl.Buffered(1))


def _dot(a, b):
    return jnp.dot(a, b, preferred_element_type=F32)


def _dot_nt(a, b):
    return lax.dot_general(a, b, (((1,), (1,)), ((), ())), preferred_element_type=F32)


def _dot_hi(a, b):
    return jnp.dot(a, b, preferred_element_type=F32, precision=HI)


def _sigmoid(x):
    return 1.0 / (1.0 + jnp.exp(-x))


def _silu(x):
    return x * _sigmoid(x)


def _softplus(x):
    return jnp.maximum(x, 0.0) + jnp.log1p(jnp.exp(-jnp.abs(x)))


def _rms(x, g):
    return x * lax.rsqrt(jnp.mean(x * x, axis=-1, keepdims=True) + EPS) * g


def _norm_mod(x, g, shift, scale):
    return _rms(x, g) * (1.0 + scale) + shift


def _ada_kernel(c_ref, w_ref, b_ref, o_ref):
    c = c_ref[...]
    o_ref[...] = _dot(_silu(c).astype(BF), w_ref[...].astype(BF)) + b_ref[...]


def _ada(cond, ada_w, ada_b):
    depth, d, n = ada_w.shape
    tn = 1536
    return pl.pallas_call(
        _ada_kernel,
        grid=(depth, n // tn),
        in_specs=[pl.BlockSpec((8, d), lambda l, j: (0, 0)),
                  pl.BlockSpec((None, d, tn), lambda l, j: (l, 0, j)),
                  pl.BlockSpec((None, 1, tn), lambda l, j: (l, 0, j))],
        out_specs=pl.BlockSpec((None, 8, tn), lambda l, j: (l, 0, j)),
        out_shape=jax.ShapeDtypeStruct((depth, 8, n), F32),
        compiler_params=_params("parallel", "parallel"),
        name="adaln",
    )(cond, ada_w, ada_b.reshape(depth, 1, n))


class _Stream:
    def __init__(self, n_p, len_s, n_s, tm):
        assert n_p % tm == 0 and len_s % tm == 0
        self.n_p, self.len_s, self.n_s, self.tm = n_p, len_s, n_s, tm
        self.total = n_p + len_s * n_s
        self.steps = self.total // tm
        self.p_steps = n_p // tm

    def group(self, i):
        t = i * self.tm
        return jnp.where(t < self.n_p, 0, 1 + (t - self.n_p) // self.len_s)

    def mod_spec(self):
        return pl.BlockSpec((None, ADA_CHUNKS, D_MODEL), lambda i: (self.group(i), 0, 0))

    def row_spec(self, width, col=0):
        return pl.BlockSpec((self.tm, width), lambda i: (i, col))

    def prompt_spec(self, width):
        return pl.BlockSpec((self.tm, width), lambda i: (jnp.minimum(i, self.p_steps - 1), 0))

    def sample_spec(self, width):
        return pl.BlockSpec((self.tm, width), lambda i: (jnp.maximum(i - self.p_steps, 0), 0))


def _ffn(x1, mod_ref, g_ref, wg_ref, wu_ref, wd_ref):
    h = _norm_mod(x1, g_ref[...], mod_ref[3:4, :], mod_ref[4:5, :]).astype(BF)
    hid = (_silu(_dot(h, wg_ref[...])) * _dot(h, wu_ref[...])).astype(BF)
    return x1 + mod_ref[5:6, :] * _dot(hid, wd_ref[...])


def _even_in_kernel(xp_ref, xs_ref, mod_ref, g_ref, w_ref, u_ref, q_ref, k_ref, v_ref, *, p_steps):
    x = jnp.where(pl.program_id(0) < p_steps, xp_ref[...], xs_ref[...])
    h = _norm_mod(x, g_ref[...], mod_ref[0:1, :], mod_ref[1:2, :])
    r = _dot(h.astype(BF), w_ref[...])
    u_ref[...] = r[:, 0:D_S5]
    q_ref[...] = r[:, D_S5:D_S5 + D_NA] * (NA_HD ** -0.5)
    k_ref[...] = r[:, D_S5 + D_NA:D_S5 + 2 * D_NA]
    v_ref[...] = r[:, D_S5 + 2 * D_NA:D_S5 + 3 * D_NA]


def _even_in(st, xp, xs, mods, g, w_in):
    n_out = w_in.shape[1]
    out = jax.ShapeDtypeStruct((st.total, D_S5), F32)
    return pl.pallas_call(
        functools.partial(_even_in_kernel, p_steps=st.p_steps),
        grid=(st.steps,),
        in_specs=[st.prompt_spec(D_MODEL), st.sample_spec(D_MODEL), st.mod_spec(), _resident((1, D_MODEL)),
                  _resident((D_MODEL, n_out))],
        out_specs=[st.row_spec(D_S5)] * 4,
        out_shape=[out] * 4,
        compiler_params=_params("parallel"),
        name="even_in",
    )(xp, xs, mods, g, w_in)


def _ctx_attn_kernel(q_ref, k_ref, v_ref, o_ref):
    q = q_ref[...]
    k = k_ref[...].astype(BF)
    v = v_ref[...].astype(BF)
    lane = lax.broadcasted_iota(jnp.int32, q.shape, 1)
    outs = []
    for h in range(2):
        in_head = (lane >= NA_HD * h) & (lane < NA_HD * (h + 1))
        qh = jnp.where(in_head, q, 0.0).astype(BF)
        s = _dot_nt(qh, k)
        p = jnp.exp(s - jnp.max(s, axis=-1, keepdims=True))
        l = jnp.sum(p, axis=-1, keepdims=True)
        outs.append(_dot(p.astype(BF), v) / l)
    o_ref[...] = jnp.where(lane < NA_HD, outs[0], outs[1])


def _ctx_attn(q, k, v, n_seq, seq):
    spec = pl.BlockSpec((seq, LANES), lambda b, p: (b, p))
    return pl.pallas_call(
        _ctx_attn_kernel,
        grid=(n_seq, D_NA // LANES),
        in_specs=[spec, spec, spec],
        out_specs=spec,
        out_shape=jax.ShapeDtypeStruct((n_seq * seq, D_NA), F32),
        compiler_params=_params("parallel", "parallel"),
        name="ctx_attn",
    )(q, k, v)


NA_N_DR = 2 * NA_WIN_ROWS - 1
NA_BOTH, NA_LEFT, NA_RIGHT = 0, NA_N_DR - 1, 2 * NA_N_DR - 1
NA_NONE = 3 * NA_N_DR - 1
NA_UNION = NA_WIN_ROWS + NA_ROWS


def _bias_kernel(rpb_ref, o_ref):
    h = pl.program_id(0)
    n_dc = 2 * NA_WIN_COLS - 1
    wq = lax.broadcasted_iota(jnp.int32, (GRID_W, LANES), 0)
    lane = lax.broadcasted_iota(jnp.int32, (GRID_W, LANES), 1)
    wk = lane & (GRID_W - 1)
    left = lane < GRID_W
    col_start = jnp.clip(wq - NA_WIN_COLS // 2, 0, GRID_W - NA_WIN_COLS)
    ok = (wk >= col_start) & (wk < col_start + NA_WIN_COLS)
    dc = jnp.clip(wk - wq, -(NA_WIN_COLS - 1), NA_WIN_COLS - 1) + (NA_WIN_COLS - 1)

    def entry(e, carry):
        is_both = e < NA_LEFT
        is_left = jnp.logical_and(e >= NA_LEFT, e < NA_RIGHT)
        is_right = jnp.logical_and(e >= NA_RIGHT, e < NA_NONE)
        d_left = jnp.where(is_both, e, jnp.where(is_left, e - NA_LEFT, 0))
        d_right = jnp.where(is_both, e + 1, jnp.where(is_right, e - NA_RIGHT, 0))
        left_on = jnp.where(jnp.logical_or(is_both, is_left), 1, 0)
        right_on = jnp.where(jnp.logical_or(is_both, is_right), 1, 0)
        acc = jnp.zeros((GRID_W, LANES), F32)
        for kk in range(n_dc):
            lo = rpb_ref[(h * NA_N_DR + d_left) * n_dc + kk]
            hi = rpb_ref[(h * NA_N_DR + d_right) * n_dc + kk]
            acc = jnp.where(dc == kk, jnp.where(left, lo, hi), acc)
        side_on = jnp.where(left, left_on, right_on) > 0
        o_ref[e] = jnp.where(ok & side_on, acc, NEG_BIG)
        return carry

    lax.fori_loop(0, NA_NONE + 1, entry, 0)


def _bias_table(rpb):
    return pl.pallas_call(
        _bias_kernel,
        grid=(NA_HEADS,),
        in_specs=[pl.BlockSpec(memory_space=pltpu.SMEM)],
        out_specs=pl.BlockSpec((None, NA_NONE + 1, GRID_W, LANES), lambda h: (h, 0, 0, 0)),
        out_shape=jax.ShapeDtypeStruct((NA_HEADS, NA_NONE + 1, GRID_W, LANES), F32),
        compiler_params=_params("parallel"),
        name="na_bias",
    )(rpb.reshape(-1))


def _na_kernel(q_ref, k_ref, v_ref, ck_ref, cv_ref, bias_ref, o_ref, kb_ref, vb_ref, *, rows):
    kb_ref[...] = k_ref[...].astype(BF)
    vb_ref[...] = v_ref[...].astype(BF)
    ck = ck_ref[...]
    cv = cv_ref[...]
    kh = NA_WIN_ROWS
    nq = NA_ROWS * GRID_W
    lane = lax.broadcasted_iota(jnp.int32, (nq, LANES), 1)

    def body(g, carry):
        r0 = g * NA_ROWS
        first = jnp.clip(r0 - kh // 2, 0, rows - NA_UNION)
        qoff = pl.multiple_of(r0 * GRID_W, nq)
        q = q_ref[pl.ds(qoff, nq), :]
        koff = pl.multiple_of(first * GRID_W, GRID_W)
        kw = kb_ref[pl.ds(koff, NA_UNION * GRID_W), :]
        vw = vb_ref[pl.ds(koff, NA_UNION * GRID_W), :]
        tile_idx = []
        for j in range(NA_ROWS):
            r = r0 + j
            start = jnp.clip(r - kh // 2, 0, rows - kh)
            for ip in range(NA_UNION // 2):
                k0 = first + 2 * ip
                in0 = jnp.logical_and(k0 >= start, k0 < start + kh)
                in1 = jnp.logical_and(k0 + 1 >= start, k0 + 1 < start + kh)
                dr0 = k0 - r + (NA_WIN_ROWS - 1)
                idx = jnp.where(jnp.logical_and(in0, in1), NA_BOTH + dr0,
                                jnp.where(in0, NA_LEFT + dr0, jnp.where(in1, NA_RIGHT + dr0 + 1, NA_NONE)))
                tile_idx.append(jnp.clip(idx, 0, NA_NONE))
        outs = []
        for h in range(2):
            in_head = (lane >= NA_HD * h) & (lane < NA_HD * (h + 1))
            qh = jnp.where(in_head, q, 0.0).astype(BF)
            per_row = NA_UNION // 2
            bias = jnp.concatenate(
                [jnp.concatenate([bias_ref[h, tile_idx[j * per_row + ip]] for ip in range(per_row)], axis=1)
                 for j in range(NA_ROWS)], axis=0)
            s_loc = _dot_nt(qh, kw) + bias
            s_ctx = _dot_nt(qh, ck)
            m = jnp.maximum(jnp.max(s_loc, axis=-1, keepdims=True), jnp.max(s_ctx, axis=-1, keepdims=True))
            p_loc = jnp.exp(s_loc - m)
            p_ctx = jnp.exp(s_ctx - m)
            l = jnp.sum(p_loc, axis=-1, keepdims=True) + jnp.sum(p_ctx, axis=-1, keepdims=True)
            outs.append((_dot(p_loc.astype(BF), vw) + _dot(p_ctx.astype(BF), cv)) / l)
        o_ref[pl.ds(qoff, nq), :] = jnp.where(lane < NA_HD, outs[0], outs[1])
        return carry

    lax.fori_loop(0, rows // NA_ROWS, body, 0)


def _na_attn(q, k, v, ck, cv, bias, n_seq, seq, row_base):
    rows = seq // GRID_W
    assert rows >= NA_UNION and rows % NA_ROWS == 0
    past = ck.shape[1]
    spec = pl.BlockSpec((seq, LANES), lambda b, p: (row_base + b, p))
    cspec = pl.BlockSpec((None, past, LANES), lambda b, p: (b, 0, p))
    return pl.pallas_call(
        functools.partial(_na_kernel, rows=rows),
        grid=(n_seq, D_NA // LANES),
        in_specs=[spec, spec, spec, cspec, cspec,
                  pl.BlockSpec((2, NA_NONE + 1, GRID_W, LANES), lambda b, p: (p, 0, 0, 0))],
        out_specs=pl.BlockSpec((seq, LANES), lambda b, p: (b, p)),
        out_shape=jax.ShapeDtypeStruct((n_seq * seq, D_NA), F32),
        scratch_shapes=[pltpu.VMEM((seq, LANES), BF), pltpu.VMEM((seq, LANES), BF)],
        compiler_params=_params("parallel", "parallel"),
        name="na_attn",
    )(q, k, v, ck, cv, bias)


S5_BLK = LANES // S5_P
S5_NB = S5_G // S5_BLK
S5_ST = S5_BLK * S5_N
S5_W = S5_Q * LANES


def _cexp(zr, zi):
    e = jnp.exp(zr)
    return e * jnp.cos(zi), e * jnp.sin(zi)


def _cmul(ar, ai, br, bi):
    return ar * br - ai * bi, ar * bi + ai * br


def _s5_prep_kernel(arr_ref, ari_ref, ldr_ref, acr_ref, aci_ref, ldc_ref, btr_ref, bti_ref, ctr_ref, cti_ref,
                    v_ref, bw_ref, cw_ref, sc_ref,
                    bbr_s, bbi_s, ccr_s, cci_s, pwr_s, pwi_s, pcr_s, pci_s, p0_s, p1_s):
    t = pl.program_id(1)
    q = S5_Q
    nst = S5_ST

    @pl.when(t == 0)
    def _():
        expand = jnp.where(lax.broadcasted_iota(jnp.int32, (S5_P, 2 * LANES), 0)
                           == (lax.broadcasted_iota(jnp.int32, (S5_P, 2 * LANES), 1) & (S5_P - 1)), 1.0, 0.0)
        row_g = lax.broadcasted_iota(jnp.int32, (LANES, nst), 0) >> 4
        col_g = lax.broadcasted_iota(jnp.int32, (LANES, nst), 1) >> 6
        st_g = lax.broadcasted_iota(jnp.int32, (nst, LANES), 0) >> 6
        ch_g = lax.broadcasted_iota(jnp.int32, (nst, LANES), 1) >> 4
        lane_d = lax.broadcasted_iota(jnp.int32, (nst, LANES), 1) >> 4
        taps = []
        for d in range(2):
            ar = jnp.minimum(arr_ref[d], -S5_MIN_DECAY)
            ai = ari_ref[d]
            dt = jnp.exp(ldr_ref[d])
            abr, abi = _cexp(ar * dt, ai * dt)
            den = ar * ar + ai * ai
            cfr = ((abr - 1.0) * ar + abi * ai) / den
            cfi = (abi * ar - (abr - 1.0) * ai) / den
            bbr, bbi = _cmul(cfr, cfi, btr_ref[d], bti_ref[d])
            bbr = jnp.where(row_g == col_g, jnp.concatenate([bbr] * S5_BLK, axis=0), 0.0)
            bbi = jnp.where(row_g == col_g, jnp.concatenate([bbi] * S5_BLK, axis=0), 0.0)
            bbr_s[d] = bbr
            bbi_s[d] = bbi
            pr = jnp.ones((1, nst), F32)
            pi = jnp.zeros((1, nst), F32)
            for e in range(q + 1):
                pwr_s[d, e] = pr
                pwi_s[d, e] = pi
                pr, pi = _cmul(pr, pi, abr, abi)
            sc_ref[2 * d:2 * d + 1, :] = pwr_s[d, q]
            sc_ref[2 * d + 1:2 * d + 2, :] = pwi_s[d, q]
            arc = jnp.minimum(acr_ref[d], -S5_MIN_DECAY)
            dtc = jnp.exp(ldc_ref[d])
            acr, aci = _cexp(jnp.broadcast_to(arc * dtc, (nst, LANES)), jnp.broadcast_to(aci_ref[d] * dtc, (nst, LANES)))
            pr = jnp.ones((nst, LANES), F32)
            pi = jnp.zeros((nst, LANES), F32)
            for e in range(q + 1):
                pcr_s[d, e] = pr
                pci_s[d, e] = pi
                pr, pi = _cmul(pr, pi, acr, aci)
            cxr = _dot_hi(ctr_ref[d], expand)
            cxi = _dot_hi(cti_ref[d], expand)
            ccr_s[d] = jnp.where(st_g == ch_g, cxr[:, 0:LANES], 0.0)
            cci_s[d] = jnp.where(st_g == ch_g, cxi[:, 0:LANES], 0.0)
            pws_r, pws_i = [], []
            for k in range(2):
                sel_r = jnp.zeros((nst, LANES), F32)
                sel_i = jnp.zeros((nst, LANES), F32)
                for j in range(S5_BLK):
                    dl = S5_BLK * k + j
                    e = dl if d == 0 else q - 1 - dl
                    sel_r = jnp.where(lane_d == j, pcr_s[d, e], sel_r)
                    sel_i = jnp.where(lane_d == j, pci_s[d, e], sel_i)
                pws_r.append(sel_r)
                pws_i.append(sel_i)
            ggr, ggi = _cmul(cxr, cxi, jnp.concatenate(pws_r, axis=1), jnp.concatenate(pws_i, axis=1))
            taps.append(_dot_hi(bbr, ggr) - _dot_hi(bbi, ggi))
        lane2 = lax.broadcasted_iota(jnp.int32, (LANES, 2 * LANES), 1)
        kt0 = taps[0] + jnp.where(lane2 < S5_P, pltpu.roll(taps[1], S5_P, 1), 0.0)
        kt1 = jnp.where(lane2 >= 2 * LANES - S5_P, 0.0, taps[1])
        er = lax.broadcasted_iota(jnp.int32, (2 * LANES, S5_W), 0)
        ec = lax.broadcasted_iota(jnp.int32, (2 * LANES, S5_W), 1)
        place = jnp.where(((er >> 4) == (ec >> 7)) & ((er & (S5_P - 1)) == (ec & (S5_P - 1))), 1.0, 0.0).astype(BF)
        own = (lax.broadcasted_iota(jnp.int32, (LANES, S5_W), 0) >> 4) == (
            (lax.broadcasted_iota(jnp.int32, (LANES, S5_W), 1) >> 4) & (S5_BLK - 1))
        zeros = jnp.zeros((LANES, S5_W), BF)
        p0_s[:, 0:S5_W] = zeros
        p0_s[:, S5_W:2 * S5_W] = jnp.where(own, _dot(kt0.astype(BF), place), 0.0).astype(BF)
        p1_s[:, 0:S5_W] = jnp.where(own, _dot(kt1.astype(BF), place), 0.0).astype(BF)
        p1_s[:, S5_W:2 * S5_W] = zeros

    off0 = pl.multiple_of(S5_W - LANES * t, LANES)
    off1 = pl.multiple_of(LANES * (q - 1 - t), LANES)
    v_ref[...] = p0_s[:, pl.ds(off0, S5_W)] + p1_s[:, pl.ds(off1, S5_W)]
    for d in range(2):
        e_b = (q - 1 - t) if d == 0 else t
        br, bi = _cmul(pwr_s[d, e_b], pwi_s[d, e_b], bbr_s[d], bbi_s[d])
        bw_ref[:, 2 * nst * d:2 * nst * d + nst] = br.astype(BF)
        bw_ref[:, 2 * nst * d + nst:2 * nst * (d + 1)] = bi.astype(BF)
        e_c = (t + 1) if d == 0 else (q - t)
        gr, gi = _cmul(ccr_s[d], cci_s[d], pcr_s[d, e_c], pci_s[d, e_c])
        cw_ref[2 * nst * d:2 * nst * d + nst, :] = gr.astype(BF)
        cw_ref[2 * nst * d + nst:2 * nst * (d + 1), :] = (-gi).astype(BF)


def _s5_prep(a_re, a_im, log_dt, b_re, b_im, c_re, c_im):
    nst, nb, q = S5_ST, S5_NB, S5_Q
    row = lambda t: t.reshape(2, nb, 1, nst)
    col = lambda t: t.reshape(2, nb, nst, 1)
    ld = jnp.broadcast_to(log_dt[:, :, None], (2, S5_G, S5_N))
    bt = lambda t: t.reshape(2, nb, S5_BLK, S5_N, S5_P).transpose(0, 1, 4, 2, 3).reshape(2, nb, S5_P, nst)
    ct = lambda t: t.reshape(2, nb, S5_BLK, S5_P, S5_N).transpose(0, 1, 2, 4, 3).reshape(2, nb, nst, S5_P)
    rspec = pl.BlockSpec((2, None, 1, nst), lambda b, t: (0, b, 0, 0))
    cspec = pl.BlockSpec((2, None, nst, 1), lambda b, t: (0, b, 0, 0))
    btspec = pl.BlockSpec((2, None, S5_P, nst), lambda b, t: (0, b, 0, 0))
    ctspec = pl.BlockSpec((2, None, nst, S5_P), lambda b, t: (0, b, 0, 0))
    big = jax.ShapeDtypeStruct((nb, S5_W, S5_W), BF)
    return pl.pallas_call(
        _s5_prep_kernel,
        grid=(nb, q),
        in_specs=[rspec, rspec, rspec, cspec, cspec, cspec, btspec, btspec, ctspec, ctspec],
        out_specs=[pl.BlockSpec((None, LANES, S5_W), lambda b, t: (b, t, 0)),
                   pl.BlockSpec((None, LANES, S5_W), lambda b, t: (b, t, 0)),
                   pl.BlockSpec((None, S5_W, LANES), lambda b, t: (b, 0, t)),
                   pl.BlockSpec((None, 4, nst), lambda b, t: (b, 0, 0))],
        out_shape=[big, big, big, jax.ShapeDtypeStruct((nb, 4, nst), F32)],
        scratch_shapes=[pltpu.VMEM((2, LANES, nst), F32), pltpu.VMEM((2, LANES, nst), F32),
                        pltpu.VMEM((2, nst, LANES), F32), pltpu.VMEM((2, nst, LANES), F32),
                        pltpu.VMEM((2, q + 1, 1, nst), F32), pltpu.VMEM((2, q + 1, 1, nst), F32),
                        pltpu.VMEM((2, q + 1, nst, LANES), F32), pltpu.VMEM((2, q + 1, nst, LANES), F32),
                        pltpu.VMEM((LANES, 2 * S5_W), BF), pltpu.VMEM((LANES, 2 * S5_W), BF)],
        compiler_params=_params("parallel", "arbitrary"),
        name="s5_prep",
    )(row(a_re), row(a_im), row(ld), col(a_re), col(a_im), col(ld), bt(b_re), bt(b_im), ct(c_re), ct(c_im))


def _s5_kernel(u_ref, v_ref, bw_ref, cw_ref, sc_ref, h0_ref, y_ref, fin_ref, z_s, h_s, *, p_steps, n_sub, mp):
    s = pl.program_id(1)
    m = u_ref.shape[0]
    n_tiles = 4 * S5_ST // LANES
    per = S5_ST // LANES
    tiles = lambda x: [x[:, LANES * k:LANES * (k + 1)] for k in range(x.shape[1] // LANES)]
    ucat = jnp.concatenate([u_ref[:, t, :].astype(BF) for t in range(S5_Q)], axis=1)
    for k, zk in enumerate(tiles(_dot(ucat, bw_ref[...]))):
        z_s[k] = zk
    sc = sc_ref[...]
    mult = [tiles(sc[i:i + 1]) for i in range(4)]

    def step(carry, z):
        new = [None] * n_tiles
        for d in range(2):
            for j in range(per):
                re, im = 2 * per * d + j, 2 * per * d + per + j
                nr, ni = _cmul(mult[2 * d][j], mult[2 * d + 1][j], carry[re], carry[im])
                new[re] = nr + z[re]
                new[im] = ni + z[im]
        return tuple(new)

    def visit(carry, rf, rb):
        z = []
        for k in range(n_tiles):
            rows = rf if k < 2 * per else rb
            h_s[k, rows, :] = carry[k]
            z.append(z_s[k, rows, :])
        return step(carry, z)

    @pl.when(s < p_steps)
    def _():
        carry = tuple(jnp.zeros((n_sub, LANES), F32) for _ in range(n_tiles))
        for c in range(mp):
            carry = visit(carry, pl.ds(c, n_sub, stride=mp), pl.ds(mp - 1 - c, n_sub, stride=mp))
        for i in range(4):
            fin_ref[:, i, :] = jnp.concatenate(carry[per * i:per * (i + 1)], axis=1)

    @pl.when(s >= p_steps)
    def _():
        h0 = h0_ref[...]
        init = tuple(t for i in range(4) for t in tiles(h0[i:i + 1]))
        lax.fori_loop(0, m, lambda c, carry: visit(carry, pl.ds(c, 1), pl.ds(m - 1 - c, 1)), init)

    hcat = jnp.concatenate([h_s[k] for k in range(n_tiles)], axis=1).astype(BF)
    ycat = _dot(ucat, v_ref[...]) + _dot(hcat, cw_ref[...])
    for t in range(S5_Q):
        y_ref[:, t, :] = ycat[:, LANES * t:LANES * (t + 1)]


def _s5(u, v, bw, cw, sc, h0, n_p, seq_p, seq_s):
    total = u.shape[0]
    q = S5_Q
    m = seq_s // q
    mp = seq_p // q
    n_sub = seq_s // seq_p
    p_steps = n_p // seq_s
    n_prompt = n_p // seq_p
    wspec = pl.BlockSpec((None, S5_W, S5_W), lambda b, s: (b, 0, 0), pipeline_mode=pl.Buffered(1))
    tok = pl.BlockSpec((m, q, LANES), lambda b, s: (s, 0, b))
    y, fin = pl.pallas_call(
        functools.partial(_s5_kernel, p_steps=p_steps, n_sub=n_sub, mp=mp),
        grid=(S5_NB, total // seq_s),
        in_specs=[tok, wspec, wspec, wspec,
                  pl.BlockSpec((None, 4, S5_ST), lambda b, s: (b, 0, 0)),
                  pl.BlockSpec((None, 4, S5_ST), lambda b, s: (jnp.maximum(s - p_steps, 0), 0, b))],
        out_specs=[tok, pl.BlockSpec((n_sub, 4, S5_ST), lambda b, s: (jnp.minimum(s, p_steps - 1), 0, b))],
        out_shape=[jax.ShapeDtypeStruct((total // q, q, D_S5), F32),
                   jax.ShapeDtypeStruct((n_prompt, 4, S5_G * S5_N), F32)],
        scratch_shapes=[pltpu.VMEM((4 * S5_ST // LANES, m, LANES), F32)] * 2,
        compiler_params=_params("parallel", "arbitrary"),
        name="s5_scan",
    )(u.reshape(total // q, q, D_S5), v, bw, cw, sc, h0)
    return y.reshape(total, D_S5), fin


def _state_planes(re, im):
    b = re.shape[0]
    return jnp.stack([t[:, d].reshape(b, S5_G * S5_N) for d in range(2) for t in (re, im)], axis=1)


def _planes_state(fin):
    t = fin.reshape(fin.shape[0], 2, 2, S5_G, S5_N)
    return t[:, :, 0], t[:, :, 1]


def _gelu_tanh(x):
    return 0.5 * x * (1.0 + jnp.tanh(math.sqrt(2.0 / math.pi) * (x + 0.044715 * (x * x * x))))


def _even_out_kernel(xp_ref, xs_ref, mod_ref, y_ref, u_ref, ap_ref, as_ref, dsk_ref, wglu_ref, wout_ref,
                     gf_ref, wg_ref, wu_ref, wd_ref, o_ref, *, p_steps):
    is_p = pl.program_id(0) < p_steps
    attn = jnp.where(is_p, ap_ref[...], as_ref[...])
    y = _gelu_tanh(y_ref[...] + u_ref[...] * dsk_ref[...])
    y = y * _sigmoid(_dot(y.astype(BF), wglu_ref[...]))
    out = _dot(y.astype(BF), wout_ref[0:D_S5, :]) + _dot(attn.astype(BF), wout_ref[D_S5:D_S5 + D_NA, :])
    x1 = jnp.where(is_p, xp_ref[...], xs_ref[...]) + mod_ref[2:3, :] * out
    o_ref[...] = _ffn(x1, mod_ref, gf_ref, wg_ref, wu_ref, wd_ref)


def _even_out(st, xp, xs, mods, y, u, attn_p, attn_s, d_skip, w_glu, w_out, g_ffn, wg, wu, wd):
    return pl.pallas_call(
        functools.partial(_even_out_kernel, p_steps=st.p_steps),
        grid=(st.steps,),
        in_specs=[st.prompt_spec(D_MODEL), st.sample_spec(D_MODEL), st.mod_spec(), st.row_spec(D_S5),
                  st.row_spec(D_S5), st.prompt_spec(D_NA), st.sample_spec(D_NA),
                  _resident((1, D_S5)), _resident(w_glu.shape), _resident(w_out.shape),
                  _resident((1, D_MODEL)), _resident(wg.shape), _resident(wu.shape), _resident(wd.shape)],
        out_specs=st.row_spec(D_MODEL),
        out_shape=jax.ShapeDtypeStruct((st.total, D_MODEL), F32),
        compiler_params=_params("parallel"),
        name="even_out_ffn",
    )(xp, xs, mods, y, u, attn_p, attn_s, d_skip, w_glu, w_out, g_ffn, wg, wu, wd)


def _odd_in_kernel(x_ref, mod_ref, g_ref, wz_ref, wx_ref, wdt_ref, wdtt_ref, z_ref, xbc_ref, dt_ref, dtt_ref):
    h = _norm_mod(x_ref[...], g_ref[...], mod_ref[0:1, :], mod_ref[1:2, :]).astype(BF)
    z_ref[...] = _dot(h, wz_ref[...])
    xbc_ref[...] = _dot(h, wx_ref[...])
    dt_ref[...] = _dot(h, wdt_ref[...])
    dtt_ref[...] = _dot_nt(wdtt_ref[...], h)


def _odd_in(st, x, mods, g, wz, wx, wdt):
    wdtt = wdt.T
    return pl.pallas_call(
        _odd_in_kernel,
        grid=(st.steps,),
        in_specs=[st.row_spec(D_MODEL), st.mod_spec(), _resident((1, D_MODEL)),
                  _resident(wz.shape), _resident(wx.shape), _resident(wdt.shape), _resident(wdtt.shape)],
        out_specs=[st.row_spec(D_INNER), st.row_spec(SSD_CONV_DIM), st.row_spec(2 * SSD_H),
                   pl.BlockSpec((2 * SSD_H, st.tm), lambda i: (0, i))],
        out_shape=[jax.ShapeDtypeStruct((st.total, D_INNER), F32),
                   jax.ShapeDtypeStruct((st.total, SSD_CONV_DIM), F32),
                   jax.ShapeDtypeStruct((st.total, 2 * SSD_H), F32),
                   jax.ShapeDtypeStruct((2 * SSD_H, st.total), F32)],
        compiler_params=_params("parallel"),
        name="odd_in",
    )(x, mods, g, wz, wx, wdt, wdtt)


def _conv_kernel(x_ref, prev_ref, next_ref, w_ref, b_ref, xc_ref, *, lt, n_p, seq_p, seq_s):
    tok = pl.program_id(0) * lt
    in_p = tok < n_p
    pos = jnp.where(in_p, tok % seq_p, (tok - n_p) % seq_s)
    seq = jnp.where(in_p, seq_p, seq_s)
    halo = prev_ref.shape[0]
    prev = jnp.where(pos == 0, 0.0, prev_ref[...])
    nxt = jnp.where(pos + lt == seq, 0.0, next_ref[...])
    ext = jnp.concatenate([prev, x_ref[...], nxt], axis=0)
    n_ext = lt + 2 * halo
    acc = b_ref[...] + jnp.zeros((lt, x_ref.shape[1]), F32)
    for kk in range(SSD_CONV):
        shift = (SSD_CONV // 2 - kk) % n_ext
        tap = ext if shift == 0 else pltpu.roll(ext, shift, 0)
        acc = acc + w_ref[kk:kk + 1, :] * tap[halo:halo + lt, :]
    xc_ref[...] = _silu(acc)


def _conv(xbc, conv_w, conv_b, n_p, seq_p, seq_s):
    total, c = xbc.shape
    lt = math.gcd(seq_p, 256)
    halo = 8
    nblk = total // halo
    per = lt // halo
    return pl.pallas_call(
        functools.partial(_conv_kernel, lt=lt, n_p=n_p, seq_p=seq_p, seq_s=seq_s),
        grid=(total // lt,),
        in_specs=[pl.BlockSpec((lt, c), lambda i: (i, 0)),
                  pl.BlockSpec((halo, c), lambda i: (jnp.maximum(i * per - 1, 0), 0)),
                  pl.BlockSpec((halo, c), lambda i: (jnp.minimum((i + 1) * per, nblk - 1), 0)),
                  _resident(conv_w.shape), _resident((1, c))],
        out_specs=pl.BlockSpec((lt, c), lambda i: (i, 0)),
        out_shape=jax.ShapeDtypeStruct((total, c), F32),
        compiler_params=_params("parallel"),
        name="ssd_conv",
    )(xbc, xbc, xbc, conv_w, conv_b.reshape(1, c))


def _ssd_kernel(xs_ref, b_ref, c_ref, dt_ref, dtt_ref, alr_ref, alc_ref, dbr_ref, dbc_ref, dsk_ref, h0_ref,
                y_ref, fin_ref, st_ref, *, n_chunks, n_p, seq_p, seq_s):
    q = SSD_Q
    d = pl.program_id(0)
    ci = pl.program_id(1)
    blocks = D_INNER // LANES
    tok = jnp.where(d == 0, ci, n_chunks - 1 - ci) * q
    in_p = tok < n_p
    pos = jnp.where(in_p, tok % seq_p, (tok - n_p) % seq_s)
    first = pos == 0
    last = pos + q == jnp.where(in_p, seq_p, seq_s)
    start = jnp.where(d == 0, first, last)
    end = jnp.where(d == 0, last, first)

    @pl.when(jnp.logical_and(start, in_p))
    def _():
        st_ref[...] = jnp.zeros(st_ref.shape, F32)

    @pl.when(jnp.logical_and(start, jnp.logical_not(in_p)))
    def _():
        h0 = h0_ref[...].reshape(D_INNER, SSD_N)
        for kb in range(blocks):
            st_ref[:, LANES * kb:LANES * (kb + 1)] = h0[LANES * kb:LANES * (kb + 1), :].T

    skip_on = jnp.where(d == 0, 1.0, 0.0)
    sgn = 1 - 2 * d
    li = lax.broadcasted_iota(jnp.int32, (q, q), 0)
    si = lax.broadcasted_iota(jnp.int32, (q, q), 1)
    causal = (li - si) * sgn >= 0
    tri_l = jnp.where(causal, 1.0, 0.0)
    tri_r = jnp.where((si - li) * sgn >= 0, 1.0, 0.0)
    a_row = -jnp.exp(alr_ref[...])
    a_col = -jnp.exp(alc_ref[...])
    dt_both = dt_ref[...]
    dt_col = _softplus(jnp.where(d == 0, dt_both[:, 0:SSD_H], dt_both[:, SSD_H:2 * SSD_H]) + dbr_ref[...])
    dt_row = _softplus(dtt_ref[...] + dbc_ref[...])
    cs_col = _dot_hi(tri_l, dt_col * a_row)
    da_row = dt_row * a_col
    cs_row = _dot_hi(da_row, tri_r)
    tot = jnp.sum(da_row, axis=-1, keepdims=True)
    w_row = dt_row * jnp.exp(tot - cs_row)
    etot = jnp.exp(tot)
    src_row = cs_row - jnp.log(dt_row)
    w_row_bf = w_row.astype(BF)
    first_head = lax.broadcasted_iota(jnp.int32, (q, LANES), 1) < SSD_P
    first_head2 = lax.broadcasted_iota(jnp.int32, (q + SSD_N, LANES), 1) < SSD_P
    heads_per_group = SSD_H // SSD_G
    gw = heads_per_group * SSD_P
    for g in range(SSD_G):
        bg = b_ref[:, SSD_N * g:SSD_N * (g + 1)]
        cg = c_ref[:, SSD_N * g:SSD_N * (g + 1)].astype(BF)
        cb = _dot_nt(cg, bg.astype(BF)).astype(BF)
        bgt = bg.T.astype(BF)
        c_state = _dot(cg, st_ref[:, gw * g:gw * (g + 1)].astype(BF))
        for jp in range(heads_per_group // 2):
            pi = g * (heads_per_group // 2) + jp
            cols = slice(LANES * pi, LANES * (pi + 1))
            xp = xs_ref[:, cols]
            xb = xp.astype(BF)
            res, grow = [], []
            for hh in range(2):
                h = 2 * pi + hh
                csc = jnp.broadcast_to(cs_col[:, h:h + 1], (q, q))
                grow.append(jnp.exp(csc))
                m = jnp.exp(jnp.where(causal, csc - src_row[h:h + 1, :], -jnp.inf)).astype(BF) * cb
                bw = bgt * w_row_bf[h:h + 1, :]
                res.append(_dot(jnp.concatenate([m, bw], axis=0), xb))
            both = jnp.where(first_head2, res[0], res[1])
            keep = jnp.where(first_head, etot[2 * pi:2 * pi + 1, :], etot[2 * pi + 1:2 * pi + 2, :])
            st_ref[:, cols] = keep * st_ref[:, cols] + both[q:q + SSD_N, :]
            y_off = jnp.where(first_head, grow[0], grow[1]) * c_state[:, LANES * jp:LANES * (jp + 1)]
            y_ref[:, cols] = both[0:q, :] + y_off + (skip_on * dsk_ref[:, cols]) * xp

    @pl.when(jnp.logical_and(end, in_p))
    def _():
        for kb in range(blocks):
            t = st_ref[:, LANES * kb:LANES * (kb + 1)].T
            fin_ref[2 * kb:2 * kb + 2] = t.reshape(2, SSD_P, SSD_N)


def _ssd(xc, dt_raw, dtt_raw, dt_bias, a_log, d_skip, h0, n_p, seq_p, seq_s):
    q = SSD_Q
    total = xc.shape[0]
    n_chunks = total // q
    n_prompt = n_p // seq_p
    n_sample = h0.shape[0]

    def blk(d, c):
        return jnp.where(d == 0, c, n_chunks - 1 - c)

    def h0_idx(d, c):
        return (jnp.clip((blk(d, c) * q - n_p) // seq_s, 0, n_sample - 1), d, 0, 0, 0)

    def fin_idx(d, c):
        return (jnp.minimum(blk(d, c) * q // seq_p, n_prompt - 1), d, 0, 0, 0)

    state_block = (None, None, SSD_H, SSD_P, SSD_N)
    return pl.pallas_call(
        functools.partial(_ssd_kernel, n_chunks=n_chunks, n_p=n_p, seq_p=seq_p, seq_s=seq_s),
        grid=(2, n_chunks),
        in_specs=[pl.BlockSpec((q, D_INNER), lambda d, c: (blk(d, c), 0)),
                  pl.BlockSpec((q, SSD_GN), lambda d, c: (blk(d, c), D_INNER // SSD_GN)),
                  pl.BlockSpec((q, SSD_GN), lambda d, c: (blk(d, c), D_INNER // SSD_GN + 1)),
                  pl.BlockSpec((q, 2 * SSD_H), lambda d, c: (blk(d, c), 0)),
                  pl.BlockSpec((SSD_H, q), lambda d, c: (d, blk(d, c))),
                  pl.BlockSpec((None, 1, SSD_H), lambda d, c: (d, 0, 0)),
                  pl.BlockSpec((None, SSD_H, 1), lambda d, c: (d, 0, 0)),
                  pl.BlockSpec((None, 1, SSD_H), lambda d, c: (d, 0, 0)),
                  pl.BlockSpec((None, SSD_H, 1), lambda d, c: (d, 0, 0)),
                  _resident((1, D_INNER)),
                  pl.BlockSpec(state_block, h0_idx)],
        out_specs=[pl.BlockSpec((None, q, D_INNER), lambda d, c: (d, blk(d, c), 0)),
                   pl.BlockSpec(state_block, fin_idx)],
        out_shape=[jax.ShapeDtypeStruct((2, total, D_INNER), F32),
                   jax.ShapeDtypeStruct((n_prompt, 2, SSD_H, SSD_P, SSD_N), F32)],
        scratch_shapes=[pltpu.VMEM((SSD_N, D_INNER), F32)],
        compiler_params=_params("parallel", "arbitrary"),
        name="ssd_scan",
    )(xc, xc, xc, dt_raw, dtt_raw, a_log.reshape(2, 1, SSD_H), a_log.reshape(2, SSD_H, 1),
      dt_bias.reshape(2, 1, SSD_H), dt_bias.reshape(2, SSD_H, 1), d_skip, h0)


def _odd_out_kernel(x_ref, mod_ref, yf_ref, yb_ref, z_ref, ng_ref, wout_ref,
                    gf_ref, wg_ref, wu_ref, wd_ref, fg_ref, *o_refs, p_steps, final):
    y = (yf_ref[...] + yb_ref[...]) * _silu(z_ref[...])
    y = _rms(y, ng_ref[...])
    x1 = x_ref[...] + mod_ref[2:3, :] * _dot(y.astype(BF), wout_ref[...])
    x2 = _ffn(x1, mod_ref, gf_ref, wg_ref, wu_ref, wd_ref)
    if not final:
        o_refs[0][...] = x2
        return
    out = _rms(x2, fg_ref[...])
    is_p = pl.program_id(0) < p_steps

    @pl.when(is_p)
    def _():
        o_refs[0][...] = out

    @pl.when(jnp.logical_not(is_p))
    def _():
        o_refs[1][...] = out


def _odd_out(st, x, mods, y, z, norm_g, w_out, g_ffn, wg, wu, wd, final_g, final):
    ydir = lambda d: pl.BlockSpec((None, st.tm, D_INNER), lambda i: (d, i, 0))
    return pl.pallas_call(
        functools.partial(_odd_out_kernel, p_steps=st.p_steps, final=final),
        grid=(st.steps,),
        in_specs=[st.row_spec(D_MODEL), st.mod_spec(), ydir(0), ydir(1), st.row_spec(D_INNER),
                  _resident((1, D_INNER)), _resident(w_out.shape),
                  _resident((1, D_MODEL)), _resident(wg.shape), _resident(wu.shape), _resident(wd.shape),
                  _resident((1, D_MODEL))],
        out_specs=[st.prompt_spec(D_MODEL), st.sample_spec(D_MODEL)] if final else st.row_spec(D_MODEL),
        out_shape=([jax.ShapeDtypeStruct((st.n_p, D_MODEL), F32),
                    jax.ShapeDtypeStruct((st.total - st.n_p, D_MODEL), F32)] if final
                   else jax.ShapeDtypeStruct((st.total, D_MODEL), F32)),
        compiler_params=_params("arbitrary"),
        name="odd_out_ffn",
    )(x, mods, y, y, z, norm_g, w_out, g_ffn, wg, wu, wd, final_g)


def kernel(x_prompt, x_sample, cache_na_k, cache_na_v, state_s5_re, state_s5_im, state_ssd, c, c_ctx, norm_mix_g, norm_ffn_g, ada_w, ada_b, ffn_w_gate, ffn_w_up, ffn_w_down, ev_w_in, ev_w_out, s5_a_re, s5_a_im, s5_log_dt, s5_b_re, s5_b_im, s5_c_re, s5_c_im, s5_d, s5_w_glu, na_rpb, od_w_in, od_conv_w, od_conv_b, ssd_a_log, ssd_dt_bias, ssd_d, ssd_norm_g, od_w_out, final_norm_g):
    bp, seq_p, d = x_prompt.shape
    bs, seq_s, _ = x_sample.shape
    depth = ada_w.shape[0]
    n_p = bp * seq_p
    assert d == D_MODEL and n_p % seq_s == 0
    st = _Stream(n_p, seq_s, bs, tm=math.gcd(512, math.gcd(n_p, seq_s)))
    st_odd = _Stream(n_p, seq_s, bs, tm=math.gcd(256, math.gcd(n_p, seq_s)))

    xp = x_prompt.reshape(n_p, d)
    xs = x_sample.reshape(bs * seq_s, d)
    cond = jnp.concatenate([c_ctx[None, :], c, jnp.zeros((8 - 1 - bs, d), F32)], axis=0)
    mods = _ada(cond, ada_w, ada_b).reshape(depth, 8, ADA_CHUNKS, d)
    row = lambda t: t.reshape(1, -1)

    new_k, new_v, new_s5_re, new_s5_im, new_ssd = [], [], [], [], []
    for layer in range(depth):
        wg = ffn_w_gate[layer].astype(BF)
        wu = ffn_w_up[layer].astype(BF)
        wd = ffn_w_down[layer].astype(BF)
        g_mix = row(norm_mix_g[layer])
        g_ffn = row(norm_ffn_g[layer])
        if layer % 2 == 0:
            e = layer // 2
            if layer > 0:
                xp, xs = x[:n_p], x[n_p:]
            u, q, k, v = _even_in(st, xp, xs, mods[layer], g_mix, ev_w_in[e].astype(BF))
            attn_p = _ctx_attn(q, k, v, bp, seq_p)
            heads = lambda t: t[:n_p].reshape(bp, seq_p, NA_HEADS, NA_HD).transpose(0, 2, 1, 3)
            new_k.append(heads(k))
            new_v.append(heads(v))
            ctx = lambda t: t[:, e].transpose(0, 2, 1, 3).reshape(bs, -1, D_NA).astype(BF)
            attn_s = _na_attn(q, k, v, ctx(cache_na_k), ctx(cache_na_v), _bias_table(na_rpb[e]),
                              bs, seq_s, n_p // seq_s)
            v_op, bw, cw, sc = _s5_prep(s5_a_re[e], s5_a_im[e], s5_log_dt[e], s5_b_re[e], s5_b_im[e],
                                        s5_c_re[e], s5_c_im[e])
            y, fin = _s5(u, v_op, bw, cw, sc, _state_planes(state_s5_re[:, e], state_s5_im[:, e]),
                         n_p, seq_p, seq_s)
            fre, fim = _planes_state(fin)
            new_s5_re.append(fre)
            new_s5_im.append(fim)
            x = _even_out(st, xp, xs, mods[layer], y, u, attn_p, attn_s, row(s5_d[e]),
                          s5_w_glu[e].astype(BF), ev_w_out[e].astype(BF), g_ffn, wg, wu, wd)
        else:
            o = layer // 2
            w_in = od_w_in[o].astype(BF)
            z, xbc, dt_raw, dtt_raw = _odd_in(st, x, mods[layer], g_mix, w_in[:, :D_INNER],
                                              w_in[:, D_INNER:D_INNER + SSD_CONV_DIM],
                                              w_in[:, D_INNER + SSD_CONV_DIM:])
            xc = _conv(xbc, od_conv_w[o], od_conv_b[o], n_p, seq_p, seq_s)
            y, fin = _ssd(xc, dt_raw, dtt_raw, ssd_dt_bias[o], ssd_a_log[o], row(jnp.repeat(ssd_d[o], SSD_P)),
                          state_ssd[:, o], n_p, seq_p, seq_s)
            new_ssd.append(fin)
            x = _odd_out(st_odd, x, mods[layer], y, z, row(ssd_norm_g[o]), od_w_out[o].astype(BF),
                         g_ffn, wg, wu, wd, row(final_norm_g), layer == depth - 1)
    if depth % 2 == 1:
        raise NotImplementedError("final norm is fused into the last (odd) layer")
    y_prompt = x[0].reshape(bp, seq_p, d)
    y_sample = x[1].reshape(bs, seq_s, d)
    return (y_prompt, y_sample, jnp.stack(new_k, axis=1), jnp.stack(new_v, axis=1),
            jnp.stack(new_s5_re, axis=1), jnp.stack(new_s5_im, axis=1), jnp.stack(new_ssd, axis=1))
```

```python
import functools
import math

import jax
import jax.numpy as jnp
from jax import lax
from jax.experimental import pallas as pl
from jax.experimental.pallas import tpu as pltpu

F32 = jnp.float32
BF = jnp.bfloat16
HI = lax.Precision.HIGHEST

D_MODEL = 1024
EPS = 1e-6
ADA_CHUNKS = 6
GRID_W = 64
D_S5 = 512
S5_P = 16
S5_G = D_S5 // S5_P
S5_N = 64
S5_MIN_DECAY = 1e-4
S5_Q = 16
D_NA = 512
NA_HD = 64
NA_HEADS = D_NA // NA_HD
NA_WIN_ROWS = 8
NA_WIN_COLS = 16
NA_ROWS = 4
D_INNER = 2048
SSD_P = 64
SSD_H = D_INNER // SSD_P
SSD_G = 4
SSD_N = 128
SSD_Q = 128
SSD_CONV = 5
SSD_GN = SSD_G * SSD_N
SSD_CONV_DIM = D_INNER + 2 * SSD_GN
D_FF = 2816

V7X_VMEM_BYTES = 64 * 1024 * 1024
VMEM_LIMIT = V7X_VMEM_BYTES - 4 * 1024 * 1024
LANES = 128
NEG_BIG = -1e30


def _params(*sem):
    return pltpu.CompilerParams(dimension_semantics=sem, vmem_limit_bytes=VMEM_LIMIT)


def _resident(shape):
    nd = len(shape)
    return pl.BlockSpec(shape, lambda *_: (0,) * nd, pipeline_mode=pl.Buffered(1))


def _dot(a, b):
    return jnp.dot(a, b, preferred_element_type=F32)


def _dot_nt(a, b):
    return lax.dot_general(a, b, (((1,), (1,)), ((), ())), preferred_element_type=F32)


def _dot_hi(a, b):
    return jnp.dot(a, b, preferred_element_type=F32, precision=HI)


def _sigmoid(x):
    return 1.0 / (1.0 + jnp.exp(-x))


def _silu(x):
    return x * _sigmoid(x)


def _softplus(x):
    return jnp.maximum(x, 0.0) + jnp.log1p(jnp.exp(-jnp.abs(x)))


def _rms(x, g):
    return x * lax.rsqrt(jnp.mean(x * x, axis=-1, keepdims=True) + EPS) * g


def _norm_mod(x, g, shift, scale):
    return _rms(x, g) * (1.0 + scale) + shift


def _ada_kernel(c_ref, w_ref, b_ref, o_ref):
    c = c_ref[...]
    o_ref[...] = _dot(_silu(c).astype(BF), w_ref[...].astype(BF)) + b_ref[...]


def _ada(cond, ada_w, ada_b):
    depth, d, n = ada_w.shape
    tn = 1536
    return pl.pallas_call(
        _ada_kernel,
        grid=(depth, n // tn),
        in_specs=[pl.BlockSpec((8, d), lambda l, j: (0, 0)),
                  pl.BlockSpec((None, d, tn), lambda l, j: (l, 0, j)),
                  pl.BlockSpec((None, 1, tn), lambda l, j: (l, 0, j))],
        out_specs=pl.BlockSpec((None, 8, tn), lambda l, j: (l, 0, j)),
        out_shape=jax.ShapeDtypeStruct((depth, 8, n), F32),
        compiler_params=_params("parallel", "parallel"),
        name="adaln",
    )(cond, ada_w, ada_b.reshape(depth, 1, n))


class _Stream:
    def __init__(self, n_p, len_s, n_s, tm):
        assert n_p % tm == 0 and len_s % tm == 0
        self.n_p, self.len_s, self.n_s, self.tm = n_p, len_s, n_s, tm
        self.total = n_p + len_s * n_s
        self.steps = self.total // tm
        self.p_steps = n_p // tm

    def group(self, i):
        t = i * self.tm
        return jnp.where(t < self.n_p, 0, 1 + (t - self.n_p) // self.len_s)

    def mod_spec(self):
        return pl.BlockSpec((None, ADA_CHUNKS, D_MODEL), lambda i: (self.group(i), 0, 0))

    def row_spec(self, width, col=0):
        return pl.BlockSpec((self.tm, width), lambda i: (i, col))

    def prompt_spec(self, width):
        return pl.BlockSpec((self.tm, width), lambda i: (jnp.minimum(i, self.p_steps - 1), 0))

    def sample_spec(self, width):
        return pl.BlockSpec((self.tm, width), lambda i: (jnp.maximum(i - self.p_steps, 0), 0))


def _ffn(x1, mod_ref, g_ref, wg_ref, wu_ref, wd_ref):
    h = _norm_mod(x1, g_ref[...], mod_ref[3:4, :], mod_ref[4:5, :]).astype(BF)
    hid = (_silu(_dot(h, wg_ref[...])) * _dot(h, wu_ref[...])).astype(BF)
    return x1 + mod_ref[5:6, :] * _dot(hid, wd_ref[...])


def _even_in_kernel(xp_ref, xs_ref, mod_ref, g_ref, w_ref, u_ref, q_ref, k_ref, v_ref, *, p_steps):
    x = jnp.where(pl.program_id(0) < p_steps, xp_ref[...], xs_ref[...])
    h = _norm_mod(x, g_ref[...], mod_ref[0:1, :], mod_ref[1:2, :])
    r = _dot(h.astype(BF), w_ref[...])
    u_ref[...] = r[:, 0:D_S5]
    q_ref[...] = r[:, D_S5:D_S5 + D_NA] * (NA_HD ** -0.5)
    k_ref[...] = r[:, D_S5 + D_NA:D_S5 + 2 * D_NA]
    v_ref[...] = r[:, D_S5 + 2 * D_NA:D_S5 + 3 * D_NA]


def _even_in(st, xp, xs, mods, g, w_in):
    n_out = w_in.shape[1]
    out = jax.ShapeDtypeStruct((st.total, D_S5), F32)
    return pl.pallas_call(
        functools.partial(_even_in_kernel, p_steps=st.p_steps),
        grid=(st.steps,),
        in_specs=[st.prompt_spec(D_MODEL), st.sample_spec(D_MODEL), st.mod_spec(), _resident((1, D_MODEL)),
                  _resident((D_MODEL, n_out))],
        out_specs=[st.row_spec(D_S5)] * 4,
        out_shape=[out] * 4,
        compiler_params=_params("parallel"),
        name="even_in",
    )(xp, xs, mods, g, w_in)


def _ctx_attn_kernel(q_ref, k_ref, v_ref, o_ref):
    q = q_ref[...]
    k = k_ref[...].astype(BF)
    v = v_ref[...].astype(BF)
    lane = lax.broadcasted_iota(jnp.int32, q.shape, 1)
    outs = []
    for h in range(2):
        in_head = (lane >= NA_HD * h) & (lane < NA_HD * (h + 1))
        qh = jnp.where(in_head, q, 0.0).astype(BF)
        s = _dot_nt(qh, k)
        p = jnp.exp(s - jnp.max(s, axis=-1, keepdims=True))
        l = jnp.sum(p, axis=-1, keepdims=True)
        outs.append(_dot(p.astype(BF), v) / l)
    o_ref[...] = jnp.where(lane < NA_HD, outs[0], outs[1])


def _ctx_attn(q, k, v, n_seq, seq):
    spec = pl.BlockSpec((seq, LANES), lambda b, p: (b, p))
    return pl.pallas_call(
        _ctx_attn_kernel,
        grid=(n_seq, D_NA // LANES),
        in_specs=[spec, spec, spec],
        out_specs=spec,
        out_shape=jax.ShapeDtypeStruct((n_seq * seq, D_NA), F32),
        compiler_params=_params("parallel", "parallel"),
        name="ctx_attn",
    )(q, k, v)


NA_N_DR = 2 * NA_WIN_ROWS - 1
NA_BOTH, NA_LEFT, NA_RIGHT = 0, NA_N_DR - 1, 2 * NA_N_DR - 1
NA_NONE = 3 * NA_N_DR - 1
NA_UNION = NA_WIN_ROWS + NA_ROWS


def _bias_kernel(rpb_ref, o_ref):
    h = pl.program_id(0)
    n_dc = 2 * NA_WIN_COLS - 1
    wq = lax.broadcasted_iota(jnp.int32, (GRID_W, LANES), 0)
    lane = lax.broadcasted_iota(jnp.int32, (GRID_W, LANES), 1)
    wk = lane & (GRID_W - 1)
    left = lane < GRID_W
    col_start = jnp.clip(wq - NA_WIN_COLS // 2, 0, GRID_W - NA_WIN_COLS)
    ok = (wk >= col_start) & (wk < col_start + NA_WIN_COLS)
    dc = jnp.clip(wk - wq, -(NA_WIN_COLS - 1), NA_WIN_COLS - 1) + (NA_WIN_COLS - 1)

    def entry(e, carry):
        is_both = e < NA_LEFT
        is_left = jnp.logical_and(e >= NA_LEFT, e < NA_RIGHT)
        is_right = jnp.logical_and(e >= NA_RIGHT, e < NA_NONE)
        d_left = jnp.where(is_both, e, jnp.where(is_left, e - NA_LEFT, 0))
        d_right = jnp.where(is_both, e + 1, jnp.where(is_right, e - NA_RIGHT, 0))
        left_on = jnp.where(jnp.logical_or(is_both, is_left), 1, 0)
        right_on = jnp.where(jnp.logical_or(is_both, is_right), 1, 0)
        acc = jnp.zeros((GRID_W, LANES), F32)
        for kk in range(n_dc):
            lo = rpb_ref[(h * NA_N_DR + d_left) * n_dc + kk]
            hi = rpb_ref[(h * NA_N_DR + d_right) * n_dc + kk]
            acc = jnp.where(dc == kk, jnp.where(left, lo, hi), acc)
        side_on = jnp.where(left, left_on, right_on) > 0
        o_ref[e] = jnp.where(ok & side_on, acc, NEG_BIG)
        return carry

    lax.fori_loop(0, NA_NONE + 1, entry, 0)


def _bias_table(rpb):
    return pl.pallas_call(
        _bias_kernel,
        grid=(NA_HEADS,),
        in_specs=[pl.BlockSpec(memory_space=pltpu.SMEM)],
        out_specs=pl.BlockSpec((None, NA_NONE + 1, GRID_W, LANES), lambda h: (h, 0, 0, 0)),
        out_shape=jax.ShapeDtypeStruct((NA_HEADS, NA_NONE + 1, GRID_W, LANES), F32),
        compiler_params=_params("parallel"),
        name="na_bias",
    )(rpb.reshape(-1))


def _na_kernel(q_ref, k_ref, v_ref, ck_ref, cv_ref, bias_ref, o_ref, kb_ref, vb_ref, *, rows):
    kb_ref[...] = k_ref[...].astype(BF)
    vb_ref[...] = v_ref[...].astype(BF)
    ck = ck_ref[...]
    cv = cv_ref[...]
    kh = NA_WIN_ROWS
    nq = NA_ROWS * GRID_W
    lane = lax.broadcasted_iota(jnp.int32, (nq, LANES), 1)

    def body(g, carry):
        r0 = g * NA_ROWS
        first = jnp.clip(r0 - kh // 2, 0, rows - NA_UNION)
        qoff = pl.multiple_of(r0 * GRID_W, nq)
        q = q_ref[pl.ds(qoff, nq), :]
        koff = pl.multiple_of(first * GRID_W, GRID_W)
        kw = kb_ref[pl.ds(koff, NA_UNION * GRID_W), :]
        vw = vb_ref[pl.ds(koff, NA_UNION * GRID_W), :]
        tile_idx = []
        for j in range(NA_ROWS):
            r = r0 + j
            start = jnp.clip(r - kh // 2, 0, rows - kh)
            for ip in range(NA_UNION // 2):
                k0 = first + 2 * ip
                in0 = jnp.logical_and(k0 >= start, k0 < start + kh)
                in1 = jnp.logical_and(k0 + 1 >= start, k0 + 1 < start + kh)
                dr0 = k0 - r + (NA_WIN_ROWS - 1)
                idx = jnp.where(jnp.logical_and(in0, in1), NA_BOTH + dr0,
                                jnp.where(in0, NA_LEFT + dr0, jnp.where(in1, NA_RIGHT + dr0 + 1, NA_NONE)))
                tile_idx.append(jnp.clip(idx, 0, NA_NONE))
        outs = []
        for h in range(2):
            in_head = (lane >= NA_HD * h) & (lane < NA_HD * (h + 1))
            qh = jnp.where(in_head, q, 0.0).astype(BF)
            per_row = NA_UNION // 2
            bias = jnp.concatenate(
                [jnp.concatenate([bias_ref[h, tile_idx[j * per_row + ip]] for ip in range(per_row)], axis=1)
                 for j in range(NA_ROWS)], axis=0)
            s_loc = _dot_nt(qh, kw) + bias
            s_ctx = _dot_nt(qh, ck)
            m = jnp.maximum(jnp.max(s_loc, axis=-1, keepdims=True), jnp.max(s_ctx, axis=-1, keepdims=True))
            p_loc = jnp.exp(s_loc - m)
            p_ctx = jnp.exp(s_ctx - m)
            l = jnp.sum(p_loc, axis=-1, keepdims=True) + jnp.sum(p_ctx, axis=-1, keepdims=True)
            outs.append((_dot(p_loc.astype(BF), vw) + _dot(p_ctx.astype(BF), cv)) / l)
        o_ref[pl.ds(qoff, nq), :] = jnp.where(lane < NA_HD, outs[0], outs[1])
        return carry

    lax.fori_loop(0, rows // NA_ROWS, body, 0)


def _na_attn(q, k, v, ck, cv, bias, n_seq, seq, row_base):
    rows = seq // GRID_W
    assert rows >= NA_UNION and rows % NA_ROWS == 0
    past = ck.shape[1]
    spec = pl.BlockSpec((seq, LANES), lambda b, p: (row_base + b, p))
    cspec = pl.BlockSpec((None, past, LANES), lambda b, p: (b, 0, p))
    return pl.pallas_call(
        functools.partial(_na_kernel, rows=rows),
        grid=(n_seq, D_NA // LANES),
        in_specs=[spec, spec, spec, cspec, cspec,
                  pl.BlockSpec((2, NA_NONE + 1, GRID_W, LANES), lambda b, p: (p, 0, 0, 0))],
        out_specs=pl.BlockSpec((seq, LANES), lambda b, p: (b, p)),
        out_shape=jax.ShapeDtypeStruct((n_seq * seq, D_NA), F32),
        scratch_shapes=[pltpu.VMEM((seq, LANES), BF), pltpu.VMEM((seq, LANES), BF)],
        compiler_params=_params("parallel", "parallel"),
        name="na_attn",
    )(q, k, v, ck, cv, bias)


S5_BLK = LANES // S5_P
S5_NB = S5_G // S5_BLK
S5_ST = S5_BLK * S5_N
S5_W = S5_Q * LANES


def _cexp(zr, zi):
    e = jnp.exp(zr)
    return e * jnp.cos(zi), e * jnp.sin(zi)


def _cmul(ar, ai, br, bi):
    return ar * br - ai * bi, ar * bi + ai * br


def _s5_prep_kernel(arr_ref, ari_ref, ldr_ref, acr_ref, aci_ref, ldc_ref, btr_ref, bti_ref, ctr_ref, cti_ref,
                    v_ref, bw_ref, cw_ref, sc_ref,
                    bbr_s, bbi_s, ccr_s, cci_s, pwr_s, pwi_s, pcr_s, pci_s, p0_s, p1_s):
    t = pl.program_id(1)
    q = S5_Q
    nst = S5_ST

    @pl.when(t == 0)
    def _():
        expand = jnp.where(lax.broadcasted_iota(jnp.int32, (S5_P, 2 * LANES), 0)
                           == (lax.broadcasted_iota(jnp.int32, (S5_P, 2 * LANES), 1) & (S5_P - 1)), 1.0, 0.0)
        row_g = lax.broadcasted_iota(jnp.int32, (LANES, nst), 0) >> 4
        col_g = lax.broadcasted_iota(jnp.int32, (LANES, nst), 1) >> 6
        st_g = lax.broadcasted_iota(jnp.int32, (nst, LANES), 0) >> 6
        ch_g = lax.broadcasted_iota(jnp.int32, (nst, LANES), 1) >> 4
        lane_d = lax.broadcasted_iota(jnp.int32, (nst, LANES), 1) >> 4
        taps = []
        for d in range(2):
            ar = jnp.minimum(arr_ref[d], -S5_MIN_DECAY)
            ai = ari_ref[d]
            dt = jnp.exp(ldr_ref[d])
            abr, abi = _cexp(ar * dt, ai * dt)
            den = ar * ar + ai * ai
            cfr = ((abr - 1.0) * ar + abi * ai) / den
            cfi = (abi * ar - (abr - 1.0) * ai) / den
            bbr, bbi = _cmul(cfr, cfi, btr_ref[d], bti_ref[d])
            bbr = jnp.where(row_g == col_g, jnp.concatenate([bbr] * S5_BLK, axis=0), 0.0)
            bbi = jnp.where(row_g == col_g, jnp.concatenate([bbi] * S5_BLK, axis=0), 0.0)
            bbr_s[d] = bbr
            bbi_s[d] = bbi
            pr = jnp.ones((1, nst), F32)
            pi = jnp.zeros((1, nst), F32)
            for e in range(q + 1):
                pwr_s[d, e] = pr
                pwi_s[d, e] = pi
                pr, pi = _cmul(pr, pi, abr, abi)
            sc_ref[2 * d:2 * d + 1, :] = pwr_s[d, q]
            sc_ref[2 * d + 1:2 * d + 2, :] = pwi_s[d, q]
            arc = jnp.minimum(acr_ref[d], -S5_MIN_DECAY)
            dtc = jnp.exp(ldc_ref[d])
            acr, aci = _cexp(jnp.broadcast_to(arc * dtc, (nst, LANES)), jnp.broadcast_to(aci_ref[d] * dtc, (nst, LANES)))
            pr = jnp.ones((nst, LANES), F32)
            pi = jnp.zeros((nst, LANES), F32)
            for e in range(q + 1):
                pcr_s[d, e] = pr
                pci_s[d, e] = pi
                pr, pi = _cmul(pr, pi, acr, aci)
            cxr = _dot_hi(ctr_ref[d], expand)
            cxi = _dot_hi(cti_ref[d], expand)
            ccr_s[d] = jnp.where(st_g == ch_g, cxr[:, 0:LANES], 0.0)
            cci_s[d] = jnp.where(st_g == ch_g, cxi[:, 0:LANES], 0.0)
            pws_r, pws_i = [], []
            for k in range(2):
                sel_r = jnp.zeros((nst, LANES), F32)
                sel_i = jnp.zeros((nst, LANES), F32)
                for j in range(S5_BLK):
                    dl = S5_BLK * k + j
                    e = dl if d == 0 else q - 1 - dl
                    sel_r = jnp.where(lane_d == j, pcr_s[d, e], sel_r)
                    sel_i = jnp.where(lane_d == j, pci_s[d, e], sel_i)
                pws_r.append(sel_r)
                pws_i.append(sel_i)
            ggr, ggi = _cmul(cxr, cxi, jnp.concatenate(pws_r, axis=1), jnp.concatenate(pws_i, axis=1))
            taps.append(_dot_hi(bbr, ggr) - _dot_hi(bbi, ggi))
        lane2 = lax.broadcasted_iota(jnp.int32, (LANES, 2 * LANES), 1)
        kt0 = taps[0] + jnp.where(lane2 < S5_P, pltpu.roll(taps[1], S5_P, 1), 0.0)
        kt1 = jnp.where(lane2 >= 2 * LANES - S5_P, 0.0, taps[1])
        er = lax.broadcasted_iota(jnp.int32, (2 * LANES, S5_W), 0)
        ec = lax.broadcasted_iota(jnp.int32, (2 * LANES, S5_W), 1)
        place = jnp.where(((er >> 4) == (ec >> 7)) & ((er & (S5_P - 1)) == (ec & (S5_P - 1))), 1.0, 0.0).astype(BF)
        own = (lax.broadcasted_iota(jnp.int32, (LANES, S5_W), 0) >> 4) == (
            (lax.broadcasted_iota(jnp.int32, (LANES, S5_W), 1) >> 4) & (S5_BLK - 1))
        zeros = jnp.zeros((LANES, S5_W), BF)
        p0_s[:, 0:S5_W] = zeros
        p0_s[:, S5_W:2 * S5_W] = jnp.where(own, _dot(kt0.astype(BF), place), 0.0).astype(BF)
        p1_s[:, 0:S5_W] = jnp.where(own, _dot(kt1.astype(BF), place), 0.0).astype(BF)
        p1_s[:, S5_W:2 * S5_W] = zeros

    off0 = pl.multiple_of(S5_W - LANES * t, LANES)
    off1 = pl.multiple_of(LANES * (q - 1 - t), LANES)
    v_ref[...] = p0_s[:, pl.ds(off0, S5_W)] + p1_s[:, pl.ds(off1, S5_W)]
    for d in range(2):
        e_b = (q - 1 - t) if d == 0 else t
        br, bi = _cmul(pwr_s[d, e_b], pwi_s[d, e_b], bbr_s[d], bbi_s[d])
        bw_ref[:, 2 * nst * d:2 * nst * d + nst] = br.astype(BF)
        bw_ref[:, 2 * nst * d + nst:2 * nst * (d + 1)] = bi.astype(BF)
        e_c = (t + 1) if d == 0 else (q - t)
        gr, gi = _cmul(ccr_s[d], cci_s[d], pcr_s[d, e_c], pci_s[d, e_c])
        cw_ref[2 * nst * d:2 * nst * d + nst, :] = gr.astype(BF)
        cw_ref[2 * nst * d + nst:2 * nst * (d + 1), :] = (-gi).astype(BF)


def _s5_prep(a_re, a_im, log_dt, b_re, b_im, c_re, c_im):
    nst, nb, q = S5_ST, S5_NB, S5_Q
    row = lambda t: t.reshape(2, nb, 1, nst)
    col = lambda t: t.reshape(2, nb, nst, 1)
    ld = jnp.broadcast_to(log_dt[:, :, None], (2, S5_G, S5_N))
    bt = lambda t: t.reshape(2, nb, S5_BLK, S5_N, S5_P).transpose(0, 1, 4, 2, 3).reshape(2, nb, S5_P, nst)
    ct = lambda t: t.reshape(2, nb, S5_BLK, S5_P, S5_N).transpose(0, 1, 2, 4, 3).reshape(2, nb, nst, S5_P)
    rspec = pl.BlockSpec((2, None, 1, nst), lambda b, t: (0, b, 0, 0))
    cspec = pl.BlockSpec((2, None, nst, 1), lambda b, t: (0, b, 0, 0))
    btspec = pl.BlockSpec((2, None, S5_P, nst), lambda b, t: (0, b, 0, 0))
    ctspec = pl.BlockSpec((2, None, nst, S5_P), lambda b, t: (0, b, 0, 0))
    big = jax.ShapeDtypeStruct((nb, S5_W, S5_W), BF)
    return pl.pallas_call(
        _s5_prep_kernel,
        grid=(nb, q),
        in_specs=[rspec, rspec, rspec, cspec, cspec, cspec, btspec, btspec, ctspec, ctspec],
        out_specs=[pl.BlockSpec((None, LANES, S5_W), lambda b, t: (b, t, 0)),
                   pl.BlockSpec((None, LANES, S5_W), lambda b, t: (b, t, 0)),
                   pl.BlockSpec((None, S5_W, LANES), lambda b, t: (b, 0, t)),
                   pl.BlockSpec((None, 4, nst), lambda b, t: (b, 0, 0))],
        out_shape=[big, big, big, jax.ShapeDtypeStruct((nb, 4, nst), F32)],
        scratch_shapes=[pltpu.VMEM((2, LANES, nst), F32), pltpu.VMEM((2, LANES, nst), F32),
                        pltpu.VMEM((2, nst, LANES), F32), pltpu.VMEM((2, nst, LANES), F32),
                        pltpu.VMEM((2, q + 1, 1, nst), F32), pltpu.VMEM((2, q + 1, 1, nst), F32),
                        pltpu.VMEM((2, q + 1, nst, LANES), F32), pltpu.VMEM((2, q + 1, nst, LANES), F32),
                        pltpu.VMEM((LANES, 2 * S5_W), BF), pltpu.VMEM((LANES, 2 * S5_W), BF)],
        compiler_params=_params("parallel", "arbitrary"),
        name="s5_prep",
    )(row(a_re), row(a_im), row(ld), col(a_re), col(a_im), col(ld), bt(b_re), bt(b_im), ct(c_re), ct(c_im))


def _s5_kernel(u_ref, v_ref, bw_ref, cw_ref, sc_ref, h0_ref, y_ref, fin_ref, z_s, h_s, *, p_steps, n_sub, mp):
    s = pl.program_id(1)
    m = u_ref.shape[0] // S5_Q
    n_tiles = 4 * S5_ST // LANES
    per = S5_ST // LANES
    tiles = lambda x: [x[:, LANES * k:LANES * (k + 1)] for k in range(x.shape[1] // LANES)]
    chunk_rows = lambda t: pl.ds(t, m, stride=S5_Q)
    ucat = jnp.concatenate([u_ref[chunk_rows(t), :].astype(BF) for t in range(S5_Q)], axis=1)
    for k, zk in enumerate(tiles(_dot(ucat, bw_ref[...]))):
        z_s[k] = zk
    sc = sc_ref[...]
    mult = [tiles(sc[i:i + 1]) for i in range(4)]

    def step(carry, z):
        new = [None] * n_tiles
        for d in range(2):
            for j in range(per):
                re, im = 2 * per * d + j, 2 * per * d + per + j
                nr, ni = _cmul(mult[2 * d][j], mult[2 * d + 1][j], carry[re], carry[im])
                new[re] = nr + z[re]
                new[im] = ni + z[im]
        return tuple(new)

    def visit(carry, rf, rb):
        z = []
        for k in range(n_tiles):
            rows = rf if k < 2 * per else rb
            h_s[k, rows, :] = carry[k]
            z.append(z_s[k, rows, :])
        return step(carry, z)

    @pl.when(s < p_steps)
    def _():
        carry = tuple(jnp.zeros((n_sub, LANES), F32) for _ in range(n_tiles))
        for c in range(mp):
            carry = visit(carry, pl.ds(c, n_sub, stride=mp), pl.ds(mp - 1 - c, n_sub, stride=mp))
        for i in range(4):
            fin_ref[:, i, :] = jnp.concatenate(carry[per * i:per * (i + 1)], axis=1)

    @pl.when(s >= p_steps)
    def _():
        h0 = h0_ref[...]
        init = tuple(t for i in range(4) for t in tiles(h0[i:i + 1]))
        lax.fori_loop(0, m, lambda c, carry: visit(carry, pl.ds(c, 1), pl.ds(m - 1 - c, 1)), init)

    hcat = jnp.concatenate([h_s[k] for k in range(n_tiles)], axis=1).astype(BF)
    ycat = _dot(ucat, v_ref[...]) + _dot(hcat, cw_ref[...])
    for t in range(S5_Q):
        y_ref[chunk_rows(t), :] = ycat[:, LANES * t:LANES * (t + 1)]


def _s5(u, v, bw, cw, sc, h0, n_p, seq_p, seq_s):
    total = u.shape[0]
    q = S5_Q
    m = seq_s // q
    mp = seq_p // q
    n_sub = seq_s // seq_p
    p_steps = n_p // seq_s
    n_prompt = n_p // seq_p
    wspec = pl.BlockSpec((None, S5_W, S5_W), lambda b, s: (b, 0, 0), pipeline_mode=pl.Buffered(1))
    tok = pl.BlockSpec((seq_s, LANES), lambda b, s: (s, b))
    y, fin = pl.pallas_call(
        functools.partial(_s5_kernel, p_steps=p_steps, n_sub=n_sub, mp=mp),
        grid=(S5_NB, total // seq_s),
        in_specs=[tok, wspec, wspec, wspec,
                  pl.BlockSpec((None, 4, S5_ST), lambda b, s: (b, 0, 0)),
                  pl.BlockSpec((None, 4, S5_ST), lambda b, s: (jnp.maximum(s - p_steps, 0), 0, b))],
        out_specs=[tok, pl.BlockSpec((n_sub, 4, S5_ST), lambda b, s: (jnp.minimum(s, p_steps - 1), 0, b))],
        out_shape=[jax.ShapeDtypeStruct((total, D_S5), F32),
                   jax.ShapeDtypeStruct((n_prompt, 4, S5_G * S5_N), F32)],
        scratch_shapes=[pltpu.VMEM((4 * S5_ST // LANES, m, LANES), F32)] * 2,
        compiler_params=_params("parallel", "arbitrary"),
        name="s5_scan",
    )(u, v, bw, cw, sc, h0)
    return y, fin


def _state_planes(re, im):
    b = re.shape[0]
    return jnp.stack([t[:, d].reshape(b, S5_G * S5_N) for d in range(2) for t in (re, im)], axis=1)


def _planes_state(fin):
    t = fin.reshape(fin.shape[0], 2, 2, S5_G, S5_N)
    return t[:, :, 0], t[:, :, 1]


def _gelu_tanh(x):
    return 0.5 * x * (1.0 + jnp.tanh(math.sqrt(2.0 / math.pi) * (x + 0.044715 * (x * x * x))))


def _even_out_kernel(xp_ref, xs_ref, mod_ref, y_ref, u_ref, ap_ref, as_ref, dsk_ref, wglu_ref, wout_ref,
                     gf_ref, wg_ref, wu_ref, wd_ref, o_ref, *, p_steps):
    is_p = pl.program_id(0) < p_steps
    attn = jnp.where(is_p, ap_ref[...], as_ref[...])
    y = _gelu_tanh(y_ref[...] + u_ref[...] * dsk_ref[...])
    y = y * _sigmoid(_dot(y.astype(BF), wglu_ref[...]))
    out = _dot(y.astype(BF), wout_ref[0:D_S5, :]) + _dot(attn.astype(BF), wout_ref[D_S5:D_S5 + D_NA, :])
    x1 = jnp.where(is_p, xp_ref[...], xs_ref[...]) + mod_ref[2:3, :] * out
    o_ref[...] = _ffn(x1, mod_ref, gf_ref, wg_ref, wu_ref, wd_ref)


def _even_out(st, xp, xs, mods, y, u, attn_p, attn_s, d_skip, w_glu, w_out, g_ffn, wg, wu, wd):
    return pl.pallas_call(
        functools.partial(_even_out_kernel, p_steps=st.p_steps),
        grid=(st.steps,),
        in_specs=[st.prompt_spec(D_MODEL), st.sample_spec(D_MODEL), st.mod_spec(), st.row_spec(D_S5),
                  st.row_spec(D_S5), st.prompt_spec(D_NA), st.sample_spec(D_NA),
                  _resident((1, D_S5)), _resident(w_glu.shape), _resident(w_out.shape),
                  _resident((1, D_MODEL)), _resident(wg.shape), _resident(wu.shape), _resident(wd.shape)],
        out_specs=st.row_spec(D_MODEL),
        out_shape=jax.ShapeDtypeStruct((st.total, D_MODEL), F32),
        compiler_params=_params("parallel"),
        name="even_out_ffn",
    )(xp, xs, mods, y, u, attn_p, attn_s, d_skip, w_glu, w_out, g_ffn, wg, wu, wd)


def _odd_in_kernel(x_ref, mod_ref, g_ref, w_ref, wdtt_ref, z_ref, xbc_ref, dt_ref, dtt_ref):
    h = _norm_mod(x_ref[...], g_ref[...], mod_ref[0:1, :], mod_ref[1:2, :]).astype(BF)
    z_ref[...] = _dot(h, w_ref[:, 0:D_INNER])
    xbc_ref[...] = _dot(h, w_ref[:, D_INNER:D_INNER + SSD_CONV_DIM])
    dt_ref[...] = _dot(h, w_ref[:, D_INNER + SSD_CONV_DIM:D_INNER + SSD_CONV_DIM + 2 * SSD_H])
    dtt_ref[...] = _dot_nt(wdtt_ref[...], h)


def _odd_in(st, x, mods, g, w_in):
    wdtt = w_in[:, D_INNER + SSD_CONV_DIM:].T
    return pl.pallas_call(
        _odd_in_kernel,
        grid=(st.steps,),
        in_specs=[st.row_spec(D_MODEL), st.mod_spec(), _resident((1, D_MODEL)),
                  _resident(w_in.shape), _resident(wdtt.shape)],
        out_specs=[st.row_spec(D_INNER), st.row_spec(SSD_CONV_DIM), st.row_spec(2 * SSD_H),
                   pl.BlockSpec((2 * SSD_H, st.tm), lambda i: (0, i))],
        out_shape=[jax.ShapeDtypeStruct((st.total, D_INNER), F32),
                   jax.ShapeDtypeStruct((st.total, SSD_CONV_DIM), F32),
                   jax.ShapeDtypeStruct((st.total, 2 * SSD_H), F32),
                   jax.ShapeDtypeStruct((2 * SSD_H, st.total), F32)],
        compiler_params=_params("parallel"),
        name="odd_in",
    )(x, mods, g, w_in, wdtt)


def _conv_kernel(x_ref, prev_ref, next_ref, w_ref, b_ref, xc_ref, *, lt, n_p, seq_p, seq_s):
    tok = pl.program_id(0) * lt
    in_p = tok < n_p
    pos = jnp.where(in_p, tok % seq_p, (tok - n_p) % seq_s)
    seq = jnp.where(in_p, seq_p, seq_s)
    halo = prev_ref.shape[0]
    prev = jnp.where(pos == 0, 0.0, prev_ref[...])
    nxt = jnp.where(pos + lt == seq, 0.0, next_ref[...])
    ext = jnp.concatenate([prev, x_ref[...], nxt], axis=0)
    n_ext = lt + 2 * halo
    acc = b_ref[...] + jnp.zeros((lt, x_ref.shape[1]), F32)
    for kk in range(SSD_CONV):
        shift = (SSD_CONV // 2 - kk) % n_ext
        tap = ext if shift == 0 else pltpu.roll(ext, shift, 0)
        acc = acc + w_ref[kk:kk + 1, :] * tap[halo:halo + lt, :]
    xc_ref[...] = _silu(acc)


def _conv(xbc, conv_w, conv_b, n_p, seq_p, seq_s):
    total, c = xbc.shape
    lt = math.gcd(seq_p, 256)
    halo = 8
    nblk = total // halo
    per = lt // halo
    return pl.pallas_call(
        functools.partial(_conv_kernel, lt=lt, n_p=n_p, seq_p=seq_p, seq_s=seq_s),
        grid=(total // lt,),
        in_specs=[pl.BlockSpec((lt, c), lambda i: (i, 0)),
                  pl.BlockSpec((halo, c), lambda i: (jnp.maximum(i * per - 1, 0), 0)),
                  pl.BlockSpec((halo, c), lambda i: (jnp.minimum((i + 1) * per, nblk - 1), 0)),
                  _resident(conv_w.shape), _resident((1, c))],
        out_specs=pl.BlockSpec((lt, c), lambda i: (i, 0)),
        out_shape=jax.ShapeDtypeStruct((total, c), F32),
        compiler_params=_params("parallel"),
        name="ssd_conv",
    )(xbc, xbc, xbc, conv_w, conv_b.reshape(1, c))


def _ssd_kernel(xs_ref, b_ref, c_ref, dt_ref, dtt_ref, alr_ref, alc_ref, dbr_ref, dbc_ref, dsk_ref, h0_ref,
                y_ref, fin_ref, st_ref, *, n_chunks, n_p, seq_p, seq_s):
    q = SSD_Q
    d = pl.program_id(0)
    ci = pl.program_id(1)
    blocks = D_INNER // LANES
    tok = jnp.where(d == 0, ci, n_chunks - 1 - ci) * q
    in_p = tok < n_p
    pos = jnp.where(in_p, tok % seq_p, (tok - n_p) % seq_s)
    first = pos == 0
    last = pos + q == jnp.where(in_p, seq_p, seq_s)
    start = jnp.where(d == 0, first, last)
    end = jnp.where(d == 0, last, first)

    @pl.when(jnp.logical_and(start, in_p))
    def _():
        st_ref[...] = jnp.zeros(st_ref.shape, F32)

    @pl.when(jnp.logical_and(start, jnp.logical_not(in_p)))
    def _():
        h0 = h0_ref[...].reshape(D_INNER, SSD_N)
        for kb in range(blocks):
            st_ref[:, LANES * kb:LANES * (kb + 1)] = h0[LANES * kb:LANES * (kb + 1), :].T

    skip_on = jnp.where(d == 0, 1.0, 0.0)
    sgn = 1 - 2 * d
    li = lax.broadcasted_iota(jnp.int32, (q, q), 0)
    si = lax.broadcasted_iota(jnp.int32, (q, q), 1)
    causal = (li - si) * sgn >= 0
    tri_l = jnp.where(causal, 1.0, 0.0)
    tri_r = jnp.where((si - li) * sgn >= 0, 1.0, 0.0)
    a_row = -jnp.exp(alr_ref[...])
    a_col = -jnp.exp(alc_ref[...])
    dt_both = dt_ref[...]
    dt_col = _softplus(jnp.where(d == 0, dt_both[:, 0:SSD_H], dt_both[:, SSD_H:2 * SSD_H]) + dbr_ref[...])
    dt_row = _softplus(dtt_ref[...] + dbc_ref[...])
    cs_col = _dot_hi(tri_l, dt_col * a_row)
    da_row = dt_row * a_col
    cs_row = _dot_hi(da_row, tri_r)
    tot = jnp.sum(da_row, axis=-1, keepdims=True)
    w_row = dt_row * jnp.exp(tot - cs_row)
    etot = jnp.exp(tot)
    src_row = cs_row - jnp.log(dt_row)
    w_row_bf = w_row.astype(BF)
    first_head = lax.broadcasted_iota(jnp.int32, (q, LANES), 1) < SSD_P
    first_head2 = lax.broadcasted_iota(jnp.int32, (q + SSD_N, LANES), 1) < SSD_P
    heads_per_group = SSD_H // SSD_G
    gw = heads_per_group * SSD_P
    for g in range(SSD_G):
        bg = b_ref[:, SSD_N * g:SSD_N * (g + 1)]
        cg = c_ref[:, SSD_N * g:SSD_N * (g + 1)].astype(BF)
        cb = _dot_nt(cg, bg.astype(BF)).astype(BF)
        bgt = bg.T.astype(BF)
        c_state = _dot(cg, st_ref[:, gw * g:gw * (g + 1)].astype(BF))
        for jp in range(heads_per_group // 2):
            pi = g * (heads_per_group // 2) + jp
            cols = slice(LANES * pi, LANES * (pi + 1))
            xp = xs_ref[:, cols]
            xb = xp.astype(BF)
            res, grow = [], []
            for hh in range(2):
                h = 2 * pi + hh
                csc = jnp.broadcast_to(cs_col[:, h:h + 1], (q, q))
                grow.append(jnp.exp(csc))
                m = jnp.exp(jnp.where(causal, csc - src_row[h:h + 1, :], -jnp.inf)).astype(BF) * cb
                bw = bgt * w_row_bf[h:h + 1, :]
                res.append(_dot(jnp.concatenate([m, bw], axis=0), xb))
            both = jnp.where(first_head2, res[0], res[1])
            keep = jnp.where(first_head, etot[2 * pi:2 * pi + 1, :], etot[2 * pi + 1:2 * pi + 2, :])
            st_ref[:, cols] = keep * st_ref[:, cols] + both[q:q + SSD_N, :]
            y_off = jnp.where(first_head, grow[0], grow[1]) * c_state[:, LANES * jp:LANES * (jp + 1)]
            y_ref[:, cols] = (both[0:q, :] + y_off + (skip_on * dsk_ref[:, cols]) * xp).astype(y_ref.dtype)

    @pl.when(jnp.logical_and(end, in_p))
    def _():
        for kb in range(blocks):
            t = st_ref[:, LANES * kb:LANES * (kb + 1)].T
            fin_ref[2 * kb:2 * kb + 2] = t.reshape(2, SSD_P, SSD_N)


def _ssd(xc, dt_raw, dtt_raw, dt_bias, a_log, d_skip, h0, n_p, seq_p, seq_s):
    q = SSD_Q
    total = xc.shape[0]
    n_chunks = total // q
    n_prompt = n_p // seq_p
    n_sample = h0.shape[0]

    def blk(d, c):
        return jnp.where(d == 0, c, n_chunks - 1 - c)

    def h0_idx(d, c):
        return (jnp.clip((blk(d, c) * q - n_p) // seq_s, 0, n_sample - 1), d, 0, 0, 0)

    def fin_idx(d, c):
        return (jnp.minimum(blk(d, c) * q // seq_p, n_prompt - 1), d, 0, 0, 0)

    state_block = (None, None, SSD_H, SSD_P, SSD_N)
    return pl.pallas_call(
        functools.partial(_ssd_kernel, n_chunks=n_chunks, n_p=n_p, seq_p=seq_p, seq_s=seq_s),
        grid=(2, n_chunks),
        in_specs=[pl.BlockSpec((q, D_INNER), lambda d, c: (blk(d, c), 0)),
                  pl.BlockSpec((q, SSD_GN), lambda d, c: (blk(d, c), D_INNER // SSD_GN)),
                  pl.BlockSpec((q, SSD_GN), lambda d, c: (blk(d, c), D_INNER // SSD_GN + 1)),
                  pl.BlockSpec((q, 2 * SSD_H), lambda d, c: (blk(d, c), 0)),
                  pl.BlockSpec((SSD_H, q), lambda d, c: (d, blk(d, c))),
                  pl.BlockSpec((None, 1, SSD_H), lambda d, c: (d, 0, 0)),
                  pl.BlockSpec((None, SSD_H, 1), lambda d, c: (d, 0, 0)),
                  pl.BlockSpec((None, 1, SSD_H), lambda d, c: (d, 0, 0)),
                  pl.BlockSpec((None, SSD_H, 1), lambda d, c: (d, 0, 0)),
                  _resident((1, D_INNER)),
                  pl.BlockSpec(state_block, h0_idx)],
        out_specs=[pl.BlockSpec((None, q, D_INNER), lambda d, c: (d, blk(d, c), 0)),
                   pl.BlockSpec(state_block, fin_idx)],
        out_shape=[jax.ShapeDtypeStruct((2, total, D_INNER), BF),
                   jax.ShapeDtypeStruct((n_prompt, 2, SSD_H, SSD_P, SSD_N), F32)],
        scratch_shapes=[pltpu.VMEM((SSD_N, D_INNER), F32)],
        compiler_params=_params("parallel", "arbitrary"),
        name="ssd_scan",
    )(xc, xc, xc, dt_raw, dtt_raw, a_log.reshape(2, 1, SSD_H), a_log.reshape(2, SSD_H, 1),
      dt_bias.reshape(2, 1, SSD_H), dt_bias.reshape(2, SSD_H, 1), d_skip, h0)


def _odd_out_kernel(x_ref, mod_ref, yf_ref, yb_ref, z_ref, ng_ref, wout_ref,
                    gf_ref, wg_ref, wu_ref, wd_ref, fg_ref, *o_refs, p_steps, final):
    y = (yf_ref[...].astype(F32) + yb_ref[...].astype(F32)) * _silu(z_ref[...])
    y = _rms(y, ng_ref[...])
    x1 = x_ref[...] + mod_ref[2:3, :] * _dot(y.astype(BF), wout_ref[...])
    x2 = _ffn(x1, mod_ref, gf_ref, wg_ref, wu_ref, wd_ref)
    if not final:
        o_refs[0][...] = x2
        return
    out = _rms(x2, fg_ref[...])
    is_p = pl.program_id(0) < p_steps

    @pl.when(is_p)
    def _():
        o_refs[0][...] = out

    @pl.when(jnp.logical_not(is_p))
    def _():
        o_refs[1][...] = out


def _odd_out(st, x, mods, y, z, norm_g, w_out, g_ffn, wg, wu, wd, final_g, final):
    ydir = lambda d: pl.BlockSpec((None, st.tm, D_INNER), lambda i: (d, i, 0))
    return pl.pallas_call(
        functools.partial(_odd_out_kernel, p_steps=st.p_steps, final=final),
        grid=(st.steps,),
        in_specs=[st.row_spec(D_MODEL), st.mod_spec(), ydir(0), ydir(1), st.row_spec(D_INNER),
                  _resident((1, D_INNER)), _resident(w_out.shape),
                  _resident((1, D_MODEL)), _resident(wg.shape), _resident(wu.shape), _resident(wd.shape),
                  _resident((1, D_MODEL))],
        out_specs=[st.prompt_spec(D_MODEL), st.sample_spec(D_MODEL)] if final else st.row_spec(D_MODEL),
        out_shape=([jax.ShapeDtypeStruct((st.n_p, D_MODEL), F32),
                    jax.ShapeDtypeStruct((st.total - st.n_p, D_MODEL), F32)] if final
                   else jax.ShapeDtypeStruct((st.total, D_MODEL), F32)),
        compiler_params=_params("arbitrary"),
        name="odd_out_ffn",
    )(x, mods, y, y, z, norm_g, w_out, g_ffn, wg, wu, wd, final_g)


def kernel(x_prompt, x_sample, cache_na_k, cache_na_v, state_s5_re, state_s5_im, state_ssd, c, c_ctx, norm_mix_g, norm_ffn_g, ada_w, ada_b, ffn_w_gate, ffn_w_up, ffn_w_down, ev_w_in, ev_w_out, s5_a_re, s5_a_im, s5_log_dt, s5_b_re, s5_b_im, s5_c_re, s5_c_im, s5_d, s5_w_glu, na_rpb, od_w_in, od_conv_w, od_conv_b, ssd_a_log, ssd_dt_bias, ssd_d, ssd_norm_g, od_w_out, final_norm_g):
    bp, seq_p, d = x_prompt.shape
    bs, seq_s, _ = x_sample.shape
    depth = ada_w.shape[0]
    n_p = bp * seq_p
    assert d == D_MODEL and n_p % seq_s == 0
    st = _Stream(n_p, seq_s, bs, tm=math.gcd(512, math.gcd(n_p, seq_s)))

    xp = x_prompt.reshape(n_p, d)
    xs = x_sample.reshape(bs * seq_s, d)
    cond = jnp.concatenate([c_ctx[None, :], c, jnp.zeros((8 - 1 - bs, d), F32)], axis=0)
    mods = _ada(cond, ada_w, ada_b).reshape(depth, 8, ADA_CHUNKS, d)
    row = lambda t: t.reshape(1, -1)

    new_k, new_v, new_s5_re, new_s5_im, new_ssd = [], [], [], [], []
    for layer in range(depth):
        wg = ffn_w_gate[layer].astype(BF)
        wu = ffn_w_up[layer].astype(BF)
        wd = ffn_w_down[layer].astype(BF)
        g_mix = row(norm_mix_g[layer])
        g_ffn = row(norm_ffn_g[layer])
        if layer % 2 == 0:
            e = layer // 2
            if layer > 0:
                xp, xs = x[:n_p], x[n_p:]
            u, q, k, v = _even_in(st, xp, xs, mods[layer], g_mix, ev_w_in[e].astype(BF))
            attn_p = _ctx_attn(q, k, v, bp, seq_p)
            heads = lambda t: t[:n_p].reshape(bp, seq_p, NA_HEADS, NA_HD).transpose(0, 2, 1, 3)
            new_k.append(heads(k))
            new_v.append(heads(v))
            ctx = lambda t: t[:, e].transpose(0, 2, 1, 3).reshape(bs, -1, D_NA).astype(BF)
            attn_s = _na_attn(q, k, v, ctx(cache_na_k), ctx(cache_na_v), _bias_table(na_rpb[e]),
                              bs, seq_s, n_p // seq_s)
            v_op, bw, cw, sc = _s5_prep(s5_a_re[e], s5_a_im[e], s5_log_dt[e], s5_b_re[e], s5_b_im[e],
                                        s5_c_re[e], s5_c_im[e])
            y, fin = _s5(u, v_op, bw, cw, sc, _state_planes(state_s5_re[:, e], state_s5_im[:, e]),
                         n_p, seq_p, seq_s)
            fre, fim = _planes_state(fin)
            new_s5_re.append(fre)
            new_s5_im.append(fim)
            x = _even_out(st, xp, xs, mods[layer], y, u, attn_p, attn_s, row(s5_d[e]),
                          s5_w_glu[e].astype(BF), ev_w_out[e].astype(BF), g_ffn, wg, wu, wd)
        else:
            o = layer // 2
            z, xbc, dt_raw, dtt_raw = _odd_in(st, x, mods[layer], g_mix, od_w_in[o].astype(BF))
            xc = _conv(xbc, od_conv_w[o], od_conv_b[o], n_p, seq_p, seq_s)
            y, fin = _ssd(xc, dt_raw, dtt_raw, ssd_dt_bias[o], ssd_a_log[o], row(jnp.repeat(ssd_d[o], SSD_P)),
                          state_ssd[:, o], n_p, seq_p, seq_s)
            new_ssd.append(fin)
            x = _odd_out(st, x, mods[layer], y, z, row(ssd_norm_g[o]), od_w_out[o].astype(BF),
                         g_ffn, wg, wu, wd, row(final_norm_g), layer == depth - 1)
    if depth % 2 == 1:
        raise NotImplementedError("final norm is fused into the last (odd) layer")
    y_prompt = x[0].reshape(bp, seq_p, d)
    y_sample = x[1].reshape(bs, seq_s, d)
    return (y_prompt, y_sample, jnp.stack(new_k, axis=1), jnp.stack(new_v, axis=1),
            jnp.stack(new_s5_re, axis=1), jnp.stack(new_s5_im, axis=1), jnp.stack(new_ssd, axis=1))
```

```python
import functools
import math

import jax
import jax.numpy as jnp
from jax import lax
from jax.experimental import pallas as pl
from jax.experimental.pallas import tpu as pltpu

F32 = jnp.float32
BF = jnp.bfloat16
HI = lax.Precision.HIGHEST

D_MODEL = 1024
EPS = 1e-6
ADA_CHUNKS = 6
GRID_W = 64
D_S5 = 512
S5_P = 16
S5_G = D_S5 // S5_P
S5_N = 64
S5_MIN_DECAY = 1e-4
S5_Q = 16
D_NA = 512
NA_HD = 64
NA_HEADS = D_NA // NA_HD
NA_WIN_ROWS = 8
NA_WIN_COLS = 16
NA_ROWS = 4
D_INNER = 2048
SSD_P = 64
SSD_H = D_INNER // SSD_P
SSD_G = 4
SSD_N = 128
SSD_Q = 128
SSD_CONV = 5
SSD_GN = SSD_G * SSD_N
SSD_CONV_DIM = D_INNER + 2 * SSD_GN
D_FF = 2816

V7X_VMEM_BYTES = 64 * 1024 * 1024
VMEM_LIMIT = V7X_VMEM_BYTES - 4 * 1024 * 1024
LANES = 128
NEG_BIG = -1e30


def _params(*sem):
    return pltpu.CompilerParams(dimension_semantics=sem, vmem_limit_bytes=VMEM_LIMIT)


def _resident(shape):
    nd = len(shape)
    return pl.BlockSpec(shape, lambda *_: (0,) * nd, pipeline_mode=pl.Buffered(1))


def _dot(a, b):
    return jnp.dot(a, b, preferred_element_type=F32)


def _dot_nt(a, b):
    return lax.dot_general(a, b, (((1,), (1,)), ((), ())), preferred_element_type=F32)


def _dot_hi(a, b):
    return jnp.dot(a, b, preferred_element_type=F32, precision=HI)


def _sigmoid(x):
    return 1.0 / (1.0 + jnp.exp(-x))


def _silu(x):
    return x * _sigmoid(x)


def _softplus(x):
    return jnp.maximum(x, 0.0) + jnp.log1p(jnp.exp(-jnp.abs(x)))


def _rms(x, g):
    return x * lax.rsqrt(jnp.mean(x * x, axis=-1, keepdims=True) + EPS) * g


def _norm_mod(x, g, shift, scale):
    return _rms(x, g) * (1.0 + scale) + shift


def _ada_kernel(c_ref, w_ref, b_ref, o_ref):
    c = c_ref[...]
    o_ref[...] = _dot(_silu(c).astype(BF), w_ref[...].astype(BF)) + b_ref[...]


def _ada(cond, ada_w, ada_b):
    depth, d, n = ada_w.shape
    tn = 1536
    return pl.pallas_call(
        _ada_kernel,
        grid=(depth, n // tn),
        in_specs=[pl.BlockSpec((8, d), lambda l, j: (0, 0)),
                  pl.BlockSpec((None, d, tn), lambda l, j: (l, 0, j)),
                  pl.BlockSpec((None, 1, tn), lambda l, j: (l, 0, j))],
        out_specs=pl.BlockSpec((None, 8, tn), lambda l, j: (l, 0, j)),
        out_shape=jax.ShapeDtypeStruct((depth, 8, n), F32),
        compiler_params=_params("parallel", "parallel"),
        name="adaln",
    )(cond, ada_w, ada_b.reshape(depth, 1, n))


class _Stream:
    def __init__(self, n_p, len_s, n_s, tm):
        assert n_p % tm == 0 and len_s % tm == 0
        self.n_p, self.len_s, self.n_s, self.tm = n_p, len_s, n_s, tm
        self.total = n_p + len_s * n_s
        self.steps = self.total // tm
        self.p_steps = n_p // tm

    def group(self, i):
        t = i * self.tm
        return jnp.where(t < self.n_p, 0, 1 + (t - self.n_p) // self.len_s)

    def mod_spec(self):
        return pl.BlockSpec((None, ADA_CHUNKS, D_MODEL), lambda i: (self.group(i), 0, 0))

    def row_spec(self, width, col=0):
        return pl.BlockSpec((self.tm, width), lambda i: (i, col))

    def prompt_spec(self, width):
        return pl.BlockSpec((self.tm, width), lambda i: (jnp.minimum(i, self.p_steps - 1), 0))

    def sample_spec(self, width):
        return pl.BlockSpec((self.tm, width), lambda i: (jnp.maximum(i - self.p_steps, 0), 0))


def _ffn(x1, mod_ref, g_ref, wg_ref, wu_ref, wd_ref):
    h = _norm_mod(x1, g_ref[...], mod_ref[3:4, :], mod_ref[4:5, :]).astype(BF)
    hid = (_silu(_dot(h, wg_ref[...])) * _dot(h, wu_ref[...])).astype(BF)
    return x1 + mod_ref[5:6, :] * _dot(hid, wd_ref[...])


def _even_in_kernel(xp_ref, xs_ref, mod_ref, g_ref, w_ref, u_ref, q_ref, k_ref, v_ref, *, p_steps):
    x = jnp.where(pl.program_id(0) < p_steps, xp_ref[...], xs_ref[...])
    h = _norm_mod(x, g_ref[...], mod_ref[0:1, :], mod_ref[1:2, :])
    r = _dot(h.astype(BF), w_ref[...])
    u_ref[...] = r[:, 0:D_S5]
    q_ref[...] = r[:, D_S5:D_S5 + D_NA] * (NA_HD ** -0.5)
    k_ref[...] = r[:, D_S5 + D_NA:D_S5 + 2 * D_NA]
    v_ref[...] = r[:, D_S5 + 2 * D_NA:D_S5 + 3 * D_NA]


def _even_in(st, xp, xs, mods, g, w_in):
    n_out = w_in.shape[1]
    out = jax.ShapeDtypeStruct((st.total, D_S5), F32)
    return pl.pallas_call(
        functools.partial(_even_in_kernel, p_steps=st.p_steps),
        grid=(st.steps,),
        in_specs=[st.prompt_spec(D_MODEL), st.sample_spec(D_MODEL), st.mod_spec(), _resident((1, D_MODEL)),
                  _resident((D_MODEL, n_out))],
        out_specs=[st.row_spec(D_S5)] * 4,
        out_shape=[out] * 4,
        compiler_params=_params("parallel"),
        name="even_in",
    )(xp, xs, mods, g, w_in)


def _ctx_attn_kernel(q_ref, k_ref, v_ref, o_ref):
    lane = lax.broadcasted_iota(jnp.int32, (q_ref.shape[0], LANES), 1)
    for pair in range(D_NA // LANES):
        cols = slice(LANES * pair, LANES * (pair + 1))
        q = q_ref[:, cols]
        k = k_ref[:, cols].astype(BF)
        v = v_ref[:, cols].astype(BF)
        outs = []
        for h in range(2):
            in_head = (lane >= NA_HD * h) & (lane < NA_HD * (h + 1))
            qh = jnp.where(in_head, q, 0.0).astype(BF)
            s = _dot_nt(qh, k)
            p = jnp.exp(s - jnp.max(s, axis=-1, keepdims=True))
            l = jnp.sum(p, axis=-1, keepdims=True)
            outs.append(_dot(p.astype(BF), v) / l)
        o_ref[:, cols] = jnp.where(lane < NA_HD, outs[0], outs[1])


def _ctx_attn(q, k, v, n_seq, seq):
    spec = pl.BlockSpec((seq, D_NA), lambda b: (b, 0))
    return pl.pallas_call(
        _ctx_attn_kernel,
        grid=(n_seq,),
        in_specs=[spec, spec, spec],
        out_specs=spec,
        out_shape=jax.ShapeDtypeStruct((n_seq * seq, D_NA), F32),
        compiler_params=_params("parallel"),
        name="ctx_attn",
    )(q, k, v)


NA_N_DR = 2 * NA_WIN_ROWS - 1
NA_BOTH, NA_LEFT, NA_RIGHT = 0, NA_N_DR - 1, 2 * NA_N_DR - 1
NA_NONE = 3 * NA_N_DR - 1
NA_UNION = NA_WIN_ROWS + NA_ROWS


def _bias_kernel(rpb_ref, o_ref):
    h = pl.program_id(0)
    n_dc = 2 * NA_WIN_COLS - 1
    wq = lax.broadcasted_iota(jnp.int32, (GRID_W, LANES), 0)
    lane = lax.broadcasted_iota(jnp.int32, (GRID_W, LANES), 1)
    wk = lane & (GRID_W - 1)
    left = lane < GRID_W
    col_start = jnp.clip(wq - NA_WIN_COLS // 2, 0, GRID_W - NA_WIN_COLS)
    ok = (wk >= col_start) & (wk < col_start + NA_WIN_COLS)
    dc = jnp.clip(wk - wq, -(NA_WIN_COLS - 1), NA_WIN_COLS - 1) + (NA_WIN_COLS - 1)

    def entry(e, carry):
        is_both = e < NA_LEFT
        is_left = jnp.logical_and(e >= NA_LEFT, e < NA_RIGHT)
        is_right = jnp.logical_and(e >= NA_RIGHT, e < NA_NONE)
        d_left = jnp.where(is_both, e, jnp.where(is_left, e - NA_LEFT, 0))
        d_right = jnp.where(is_both, e + 1, jnp.where(is_right, e - NA_RIGHT, 0))
        left_on = jnp.where(jnp.logical_or(is_both, is_left), 1, 0)
        right_on = jnp.where(jnp.logical_or(is_both, is_right), 1, 0)
        acc = jnp.zeros((GRID_W, LANES), F32)
        for kk in range(n_dc):
            lo = rpb_ref[(h * NA_N_DR + d_left) * n_dc + kk]
            hi = rpb_ref[(h * NA_N_DR + d_right) * n_dc + kk]
            acc = jnp.where(dc == kk, jnp.where(left, lo, hi), acc)
        side_on = jnp.where(left, left_on, right_on) > 0
        o_ref[e] = jnp.where(ok & side_on, acc, NEG_BIG)
        return carry

    lax.fori_loop(0, NA_NONE + 1, entry, 0)


def _bias_table(rpb):
    return pl.pallas_call(
        _bias_kernel,
        grid=(NA_HEADS,),
        in_specs=[pl.BlockSpec(memory_space=pltpu.SMEM)],
        out_specs=pl.BlockSpec((None, NA_NONE + 1, GRID_W, LANES), lambda h: (h, 0, 0, 0)),
        out_shape=jax.ShapeDtypeStruct((NA_HEADS, NA_NONE + 1, GRID_W, LANES), F32),
        compiler_params=_params("parallel"),
        name="na_bias",
    )(rpb.reshape(-1))


def _na_kernel(q_ref, k_ref, v_ref, ck_ref, cv_ref, bias_ref, o_ref, kb_ref, vb_ref, *, rows):
    kb_ref[...] = k_ref[...].astype(BF)
    vb_ref[...] = v_ref[...].astype(BF)
    ck = ck_ref[...]
    cv = cv_ref[...]
    kh = NA_WIN_ROWS
    nq = NA_ROWS * GRID_W
    lane = lax.broadcasted_iota(jnp.int32, (nq, LANES), 1)

    def body(g, carry):
        r0 = g * NA_ROWS
        first = jnp.clip(r0 - kh // 2, 0, rows - NA_UNION)
        qoff = pl.multiple_of(r0 * GRID_W, nq)
        q = q_ref[pl.ds(qoff, nq), :]
        koff = pl.multiple_of(first * GRID_W, GRID_W)
        kw = kb_ref[pl.ds(koff, NA_UNION * GRID_W), :]
        vw = vb_ref[pl.ds(koff, NA_UNION * GRID_W), :]
        tile_idx = []
        for j in range(NA_ROWS):
            r = r0 + j
            start = jnp.clip(r - kh // 2, 0, rows - kh)
            for ip in range(NA_UNION // 2):
                k0 = first + 2 * ip
                in0 = jnp.logical_and(k0 >= start, k0 < start + kh)
                in1 = jnp.logical_and(k0 + 1 >= start, k0 + 1 < start + kh)
                dr0 = k0 - r + (NA_WIN_ROWS - 1)
                idx = jnp.where(jnp.logical_and(in0, in1), NA_BOTH + dr0,
                                jnp.where(in0, NA_LEFT + dr0, jnp.where(in1, NA_RIGHT + dr0 + 1, NA_NONE)))
                tile_idx.append(jnp.clip(idx, 0, NA_NONE))
        outs = []
        for h in range(2):
            in_head = (lane >= NA_HD * h) & (lane < NA_HD * (h + 1))
            qh = jnp.where(in_head, q, 0.0).astype(BF)
            per_row = NA_UNION // 2
            bias = jnp.concatenate(
                [jnp.concatenate([bias_ref[h, tile_idx[j * per_row + ip]] for ip in range(per_row)], axis=1)
                 for j in range(NA_ROWS)], axis=0)
            s_loc = _dot_nt(qh, kw) + bias
            s_ctx = _dot_nt(qh, ck)
            m = jnp.maximum(jnp.max(s_loc, axis=-1, keepdims=True), jnp.max(s_ctx, axis=-1, keepdims=True))
            p_loc = jnp.exp(s_loc - m)
            p_ctx = jnp.exp(s_ctx - m)
            l = jnp.sum(p_loc, axis=-1, keepdims=True) + jnp.sum(p_ctx, axis=-1, keepdims=True)
            outs.append((_dot(p_loc.astype(BF), vw) + _dot(p_ctx.astype(BF), cv)) / l)
        o_ref[pl.ds(qoff, nq), :] = jnp.where(lane < NA_HD, outs[0], outs[1])
        return carry

    lax.fori_loop(0, rows // NA_ROWS, body, 0)


def _na_attn(q, k, v, ck, cv, bias, n_seq, seq, row_base):
    rows = seq // GRID_W
    assert rows >= NA_UNION and rows % NA_ROWS == 0
    past = ck.shape[1]
    spec = pl.BlockSpec((seq, LANES), lambda b, p: (row_base + b, p))
    cspec = pl.BlockSpec((None, past, LANES), lambda b, p: (b, 0, p))
    return pl.pallas_call(
        functools.partial(_na_kernel, rows=rows),
        grid=(n_seq, D_NA // LANES),
        in_specs=[spec, spec, spec, cspec, cspec,
                  pl.BlockSpec((2, NA_NONE + 1, GRID_W, LANES), lambda b, p: (p, 0, 0, 0))],
        out_specs=pl.BlockSpec((seq, LANES), lambda b, p: (b, p)),
        out_shape=jax.ShapeDtypeStruct((n_seq * seq, D_NA), F32),
        scratch_shapes=[pltpu.VMEM((seq, LANES), BF), pltpu.VMEM((seq, LANES), BF)],
        compiler_params=_params("parallel", "parallel"),
        name="na_attn",
    )(q, k, v, ck, cv, bias)


S5_BLK = LANES // S5_P
S5_NB = S5_G // S5_BLK
S5_ST = S5_BLK * S5_N
S5_W = S5_Q * LANES


def _cexp(zr, zi):
    e = jnp.exp(zr)
    return e * jnp.cos(zi), e * jnp.sin(zi)


def _cmul(ar, ai, br, bi):
    return ar * br - ai * bi, ar * bi + ai * br


def _s5_prep_kernel(arr_ref, ari_ref, ldr_ref, acr_ref, aci_ref, ldc_ref, btr_ref, bti_ref, ctr_ref, cti_ref,
                    v_ref, bw_ref, cw_ref, sc_ref,
                    bbr_s, bbi_s, ccr_s, cci_s, pwr_s, pwi_s, pcr_s, pci_s, p0_s, p1_s):
    t = pl.program_id(1)
    q = S5_Q
    nst = S5_ST

    @pl.when(t == 0)
    def _():
        expand = jnp.where(lax.broadcasted_iota(jnp.int32, (S5_P, 2 * LANES), 0)
                           == (lax.broadcasted_iota(jnp.int32, (S5_P, 2 * LANES), 1) & (S5_P - 1)), 1.0, 0.0)
        row_g = lax.broadcasted_iota(jnp.int32, (LANES, nst), 0) >> 4
        col_g = lax.broadcasted_iota(jnp.int32, (LANES, nst), 1) >> 6
        st_g = lax.broadcasted_iota(jnp.int32, (nst, LANES), 0) >> 6
        ch_g = lax.broadcasted_iota(jnp.int32, (nst, LANES), 1) >> 4
        lane_d = lax.broadcasted_iota(jnp.int32, (nst, LANES), 1) >> 4
        taps = []
        for d in range(2):
            ar = jnp.minimum(arr_ref[d], -S5_MIN_DECAY)
            ai = ari_ref[d]
            dt = jnp.exp(ldr_ref[d])
            abr, abi = _cexp(ar * dt, ai * dt)
            den = ar * ar + ai * ai
            cfr = ((abr - 1.0) * ar + abi * ai) / den
            cfi = (abi * ar - (abr - 1.0) * ai) / den
            bbr, bbi = _cmul(cfr, cfi, btr_ref[d], bti_ref[d])
            bbr = jnp.where(row_g == col_g, jnp.concatenate([bbr] * S5_BLK, axis=0), 0.0)
            bbi = jnp.where(row_g == col_g, jnp.concatenate([bbi] * S5_BLK, axis=0), 0.0)
            bbr_s[d] = bbr
            bbi_s[d] = bbi
            pr = jnp.ones((1, nst), F32)
            pi = jnp.zeros((1, nst), F32)
            for e in range(q + 1):
                pwr_s[d, e] = pr
                pwi_s[d, e] = pi
                pr, pi = _cmul(pr, pi, abr, abi)
            sc_ref[2 * d:2 * d + 1, :] = pwr_s[d, q]
            sc_ref[2 * d + 1:2 * d + 2, :] = pwi_s[d, q]
            arc = jnp.minimum(acr_ref[d], -S5_MIN_DECAY)
            dtc = jnp.exp(ldc_ref[d])
            acr, aci = _cexp(jnp.broadcast_to(arc * dtc, (nst, LANES)), jnp.broadcast_to(aci_ref[d] * dtc, (nst, LANES)))
            pr = jnp.ones((nst, LANES), F32)
            pi = jnp.zeros((nst, LANES), F32)
            for e in range(q + 1):
                pcr_s[d, e] = pr
                pci_s[d, e] = pi
                pr, pi = _cmul(pr, pi, acr, aci)
            cxr = _dot_hi(ctr_ref[d], expand)
            cxi = _dot_hi(cti_ref[d], expand)
            ccr_s[d] = jnp.where(st_g == ch_g, cxr[:, 0:LANES], 0.0)
            cci_s[d] = jnp.where(st_g == ch_g, cxi[:, 0:LANES], 0.0)
            pws_r, pws_i = [], []
            for k in range(2):
                sel_r = jnp.zeros((nst, LANES), F32)
                sel_i = jnp.zeros((nst, LANES), F32)
                for j in range(S5_BLK):
                    dl = S5_BLK * k + j
                    e = dl if d == 0 else q - 1 - dl
                    sel_r = jnp.where(lane_d == j, pcr_s[d, e], sel_r)
                    sel_i = jnp.where(lane_d == j, pci_s[d, e], sel_i)
                pws_r.append(sel_r)
                pws_i.append(sel_i)
            ggr, ggi = _cmul(cxr, cxi, jnp.concatenate(pws_r, axis=1), jnp.concatenate(pws_i, axis=1))
            taps.append(_dot_hi(bbr, ggr) - _dot_hi(bbi, ggi))
        lane2 = lax.broadcasted_iota(jnp.int32, (LANES, 2 * LANES), 1)
        kt0 = taps[0] + jnp.where(lane2 < S5_P, pltpu.roll(taps[1], S5_P, 1), 0.0)
        kt1 = jnp.where(lane2 >= 2 * LANES - S5_P, 0.0, taps[1])
        er = lax.broadcasted_iota(jnp.int32, (2 * LANES, S5_W), 0)
        ec = lax.broadcasted_iota(jnp.int32, (2 * LANES, S5_W), 1)
        place = jnp.where(((er >> 4) == (ec >> 7)) & ((er & (S5_P - 1)) == (ec & (S5_P - 1))), 1.0, 0.0).astype(BF)
        own = (lax.broadcasted_iota(jnp.int32, (LANES, S5_W), 0) >> 4) == (
            (lax.broadcasted_iota(jnp.int32, (LANES, S5_W), 1) >> 4) & (S5_BLK - 1))
        zeros = jnp.zeros((LANES, S5_W), BF)
        p0_s[:, 0:S5_W] = zeros
        p0_s[:, S5_W:2 * S5_W] = jnp.where(own, _dot(kt0.astype(BF), place), 0.0).astype(BF)
        p1_s[:, 0:S5_W] = jnp.where(own, _dot(kt1.astype(BF), place), 0.0).astype(BF)
        p1_s[:, S5_W:2 * S5_W] = zeros

    off0 = pl.multiple_of(S5_W - LANES * t, LANES)
    off1 = pl.multiple_of(LANES * (q - 1 - t), LANES)
    v_ref[...] = p0_s[:, pl.ds(off0, S5_W)] + p1_s[:, pl.ds(off1, S5_W)]
    for d in range(2):
        e_b = (q - 1 - t) if d == 0 else t
        br, bi = _cmul(pwr_s[d, e_b], pwi_s[d, e_b], bbr_s[d], bbi_s[d])
        bw_ref[:, 2 * nst * d:2 * nst * d + nst] = br.astype(BF)
        bw_ref[:, 2 * nst * d + nst:2 * nst * (d + 1)] = bi.astype(BF)
        e_c = (t + 1) if d == 0 else (q - t)
        gr, gi = _cmul(ccr_s[d], cci_s[d], pcr_s[d, e_c], pci_s[d, e_c])
        cw_ref[2 * nst * d:2 * nst * d + nst, :] = gr.astype(BF)
        cw_ref[2 * nst * d + nst:2 * nst * (d + 1), :] = (-gi).astype(BF)


def _s5_prep(a_re, a_im, log_dt, b_re, b_im, c_re, c_im):
    nst, nb, q = S5_ST, S5_NB, S5_Q
    row = lambda t: t.reshape(2, nb, 1, nst)
    col = lambda t: t.reshape(2, nb, nst, 1)
    ld = jnp.broadcast_to(log_dt[:, :, None], (2, S5_G, S5_N))
    bt = lambda t: t.reshape(2, nb, S5_BLK, S5_N, S5_P).transpose(0, 1, 4, 2, 3).reshape(2, nb, S5_P, nst)
    ct = lambda t: t.reshape(2, nb, S5_BLK, S5_P, S5_N).transpose(0, 1, 2, 4, 3).reshape(2, nb, nst, S5_P)
    rspec = pl.BlockSpec((2, None, 1, nst), lambda b, t: (0, b, 0, 0))
    cspec = pl.BlockSpec((2, None, nst, 1), lambda b, t: (0, b, 0, 0))
    btspec = pl.BlockSpec((2, None, S5_P, nst), lambda b, t: (0, b, 0, 0))
    ctspec = pl.BlockSpec((2, None, nst, S5_P), lambda b, t: (0, b, 0, 0))
    big = jax.ShapeDtypeStruct((nb, S5_W, S5_W), BF)
    return pl.pallas_call(
        _s5_prep_kernel,
        grid=(nb, q),
        in_specs=[rspec, rspec, rspec, cspec, cspec, cspec, btspec, btspec, ctspec, ctspec],
        out_specs=[pl.BlockSpec((None, LANES, S5_W), lambda b, t: (b, t, 0)),
                   pl.BlockSpec((None, LANES, S5_W), lambda b, t: (b, t, 0)),
                   pl.BlockSpec((None, S5_W, LANES), lambda b, t: (b, 0, t)),
                   pl.BlockSpec((None, 4, nst), lambda b, t: (b, 0, 0))],
        out_shape=[big, big, big, jax.ShapeDtypeStruct((nb, 4, nst), F32)],
        scratch_shapes=[pltpu.VMEM((2, LANES, nst), F32), pltpu.VMEM((2, LANES, nst), F32),
                        pltpu.VMEM((2, nst, LANES), F32), pltpu.VMEM((2, nst, LANES), F32),
                        pltpu.VMEM((2, q + 1, 1, nst), F32), pltpu.VMEM((2, q + 1, 1, nst), F32),
                        pltpu.VMEM((2, q + 1, nst, LANES), F32), pltpu.VMEM((2, q + 1, nst, LANES), F32),
                        pltpu.VMEM((LANES, 2 * S5_W), BF), pltpu.VMEM((LANES, 2 * S5_W), BF)],
        compiler_params=_params("parallel", "arbitrary"),
        name="s5_prep",
    )(row(a_re), row(a_im), row(ld), col(a_re), col(a_im), col(ld), bt(b_re), bt(b_im), ct(c_re), ct(c_im))


def _s5_kernel(u_ref, v_ref, bw_ref, cw_ref, sc_ref, h0_ref, y_ref, fin_ref, z_s, h_s, *, p_steps, n_sub, mp):
    s = pl.program_id(1)
    m = u_ref.shape[0] // S5_Q
    n_tiles = 4 * S5_ST // LANES
    per = S5_ST // LANES
    half = 2 * per
    chunk_rows = lambda t: pl.ds(t, m, stride=S5_Q)
    tile_rows = lambda k: pl.ds(k, m, stride=n_tiles)
    ucat = jnp.concatenate([u_ref[chunk_rows(t), :].astype(BF) for t in range(S5_Q)], axis=1)
    z = _dot(ucat, bw_ref[...])
    for k in range(n_tiles):
        z_s[tile_rows(k), :] = z[:, LANES * k:LANES * (k + 1)]
    sc = sc_ref[...]
    stack = lambda row: jnp.concatenate([row[:, LANES * j:LANES * (j + 1)] for j in range(per)], axis=0)

    def multipliers(d):
        ar, ai = stack(sc[2 * d:2 * d + 1]), stack(sc[2 * d + 1:2 * d + 2])
        return jnp.concatenate([ar, ar], axis=0), jnp.concatenate([-ai, ai], axis=0)

    a1f, a2f = multipliers(0)
    a1b, a2b = multipliers(1)

    def advance(hf, hb, cf, cb):
        rf = pl.ds(pl.multiple_of(cf * n_tiles, half), half)
        rb = pl.ds(pl.multiple_of(cb * n_tiles + half, half), half)
        h_s[rf, :] = hf
        h_s[rb, :] = hb
        hf = a1f * hf + a2f * pltpu.roll(hf, per, 0) + z_s[rf, :]
        hb = a1b * hb + a2b * pltpu.roll(hb, per, 0) + z_s[rb, :]
        return hf, hb

    unstack = lambda h4: jnp.concatenate([h4[j:j + 1, :] for j in range(per)], axis=1)

    @pl.when(s < p_steps)
    def _():
        for i in range(n_sub):
            hf = jnp.zeros((half, LANES), F32)
            hb = jnp.zeros((half, LANES), F32)
            for c in range(mp):
                hf, hb = advance(hf, hb, i * mp + c, i * mp + mp - 1 - c)
            for pi, h4 in enumerate((hf[0:per], hf[per:half], hb[0:per], hb[per:half])):
                fin_ref[i, pi:pi + 1, :] = unstack(h4)

    @pl.when(s >= p_steps)
    def _():
        h0 = h0_ref[...]
        init = (jnp.concatenate([stack(h0[0:1]), stack(h0[1:2])], axis=0),
                jnp.concatenate([stack(h0[2:3]), stack(h0[3:4])], axis=0))
        lax.fori_loop(0, m, lambda c, h: advance(h[0], h[1], c, m - 1 - c), init)

    hcat = jnp.concatenate([h_s[tile_rows(k), :] for k in range(n_tiles)], axis=1).astype(BF)
    ycat = _dot(ucat, v_ref[...]) + _dot(hcat, cw_ref[...])
    for t in range(S5_Q):
        y_ref[chunk_rows(t), :] = ycat[:, LANES * t:LANES * (t + 1)]


def _s5(u, v, bw, cw, sc, h0, n_p, seq_p, seq_s):
    total = u.shape[0]
    q = S5_Q
    m = seq_s // q
    mp = seq_p // q
    n_sub = seq_s // seq_p
    p_steps = n_p // seq_s
    n_prompt = n_p // seq_p
    wspec = pl.BlockSpec((None, S5_W, S5_W), lambda b, s: (b, 0, 0), pipeline_mode=pl.Buffered(1))
    tok = pl.BlockSpec((seq_s, LANES), lambda b, s: (s, b))
    y, fin = pl.pallas_call(
        functools.partial(_s5_kernel, p_steps=p_steps, n_sub=n_sub, mp=mp),
        grid=(S5_NB, total // seq_s),
        in_specs=[tok, wspec, wspec, wspec,
                  pl.BlockSpec((None, 4, S5_ST), lambda b, s: (b, 0, 0)),
                  pl.BlockSpec((None, 4, S5_ST), lambda b, s: (jnp.maximum(s - p_steps, 0), 0, b))],
        out_specs=[tok, pl.BlockSpec((n_sub, 4, S5_ST), lambda b, s: (jnp.minimum(s, p_steps - 1), 0, b))],
        out_shape=[jax.ShapeDtypeStruct((total, D_S5), F32),
                   jax.ShapeDtypeStruct((n_prompt, 4, S5_G * S5_N), F32)],
        scratch_shapes=[pltpu.VMEM((4 * S5_ST // LANES * m, LANES), F32)] * 2,
        compiler_params=_params("parallel", "arbitrary"),
        name="s5_scan",
    )(u, v, bw, cw, sc, h0)
    return y, fin


def _state_planes(re, im):
    b = re.shape[0]
    return jnp.stack([t[:, d].reshape(b, S5_G * S5_N) for d in range(2) for t in (re, im)], axis=1)


def _planes_state(fin):
    t = fin.reshape(fin.shape[0], 2, 2, S5_G, S5_N)
    return t[:, :, 0], t[:, :, 1]


def _gelu_tanh(x):
    return 0.5 * x * (1.0 + jnp.tanh(math.sqrt(2.0 / math.pi) * (x + 0.044715 * (x * x * x))))


def _even_out_kernel(xp_ref, xs_ref, mod_ref, y_ref, u_ref, ap_ref, as_ref, dsk_ref, wglu_ref, wout_ref,
                     gf_ref, wg_ref, wu_ref, wd_ref, o_ref, *, p_steps):
    is_p = pl.program_id(0) < p_steps
    attn = jnp.where(is_p, ap_ref[...], as_ref[...])
    y = _gelu_tanh(y_ref[...] + u_ref[...] * dsk_ref[...])
    y = y * _sigmoid(_dot(y.astype(BF), wglu_ref[...]))
    out = _dot(y.astype(BF), wout_ref[0:D_S5, :]) + _dot(attn.astype(BF), wout_ref[D_S5:D_S5 + D_NA, :])
    x1 = jnp.where(is_p, xp_ref[...], xs_ref[...]) + mod_ref[2:3, :] * out
    o_ref[...] = _ffn(x1, mod_ref, gf_ref, wg_ref, wu_ref, wd_ref)


def _even_out(st, xp, xs, mods, y, u, attn_p, attn_s, d_skip, w_glu, w_out, g_ffn, wg, wu, wd):
    return pl.pallas_call(
        functools.partial(_even_out_kernel, p_steps=st.p_steps),
        grid=(st.steps,),
        in_specs=[st.prompt_spec(D_MODEL), st.sample_spec(D_MODEL), st.mod_spec(), st.row_spec(D_S5),
                  st.row_spec(D_S5), st.prompt_spec(D_NA), st.sample_spec(D_NA),
                  _resident((1, D_S5)), _resident(w_glu.shape), _resident(w_out.shape),
                  _resident((1, D_MODEL)), _resident(wg.shape), _resident(wu.shape), _resident(wd.shape)],
        out_specs=st.row_spec(D_MODEL),
        out_shape=jax.ShapeDtypeStruct((st.total, D_MODEL), F32),
        compiler_params=_params("parallel"),
        name="even_out_ffn",
    )(xp, xs, mods, y, u, attn_p, attn_s, d_skip, w_glu, w_out, g_ffn, wg, wu, wd)


def _odd_in_kernel(x_ref, mod_ref, g_ref, w_ref, wdtt_ref, z_ref, xbc_ref, dt_ref, dtt_ref):
    h = _norm_mod(x_ref[...], g_ref[...], mod_ref[0:1, :], mod_ref[1:2, :]).astype(BF)
    z_ref[...] = _dot(h, w_ref[:, 0:D_INNER])
    xbc_ref[...] = _dot(h, w_ref[:, D_INNER:D_INNER + SSD_CONV_DIM])
    dt_ref[...] = _dot(h, w_ref[:, D_INNER + SSD_CONV_DIM:D_INNER + SSD_CONV_DIM + 2 * SSD_H])
    dtt_ref[...] = _dot_nt(wdtt_ref[...], h)


def _odd_in(st, x, mods, g, w_in):
    wdtt = w_in[:, D_INNER + SSD_CONV_DIM:].T
    return pl.pallas_call(
        _odd_in_kernel,
        grid=(st.steps,),
        in_specs=[st.row_spec(D_MODEL), st.mod_spec(), _resident((1, D_MODEL)),
                  _resident(w_in.shape), _resident(wdtt.shape)],
        out_specs=[st.row_spec(D_INNER), st.row_spec(SSD_CONV_DIM), st.row_spec(2 * SSD_H),
                   pl.BlockSpec((2 * SSD_H, st.tm), lambda i: (0, i))],
        out_shape=[jax.ShapeDtypeStruct((st.total, D_INNER), F32),
                   jax.ShapeDtypeStruct((st.total, SSD_CONV_DIM), F32),
                   jax.ShapeDtypeStruct((st.total, 2 * SSD_H), F32),
                   jax.ShapeDtypeStruct((2 * SSD_H, st.total), F32)],
        compiler_params=_params("parallel"),
        name="odd_in",
    )(x, mods, g, w_in, wdtt)


def _conv_kernel(x_ref, prev_ref, next_ref, w_ref, b_ref, xc_ref, *, lt, n_p, seq_p, seq_s):
    tok = pl.program_id(0) * lt
    in_p = tok < n_p
    pos = jnp.where(in_p, tok % seq_p, (tok - n_p) % seq_s)
    seq = jnp.where(in_p, seq_p, seq_s)
    halo = prev_ref.shape[0]
    prev = jnp.where(pos == 0, 0.0, prev_ref[...])
    nxt = jnp.where(pos + lt == seq, 0.0, next_ref[...])
    ext = jnp.concatenate([prev, x_ref[...], nxt], axis=0)
    n_ext = lt + 2 * halo
    acc = b_ref[...] + jnp.zeros((lt, x_ref.shape[1]), F32)
    for kk in range(SSD_CONV):
        shift = (SSD_CONV // 2 - kk) % n_ext
        tap = ext if shift == 0 else pltpu.roll(ext, shift, 0)
        acc = acc + w_ref[kk:kk + 1, :] * tap[halo:halo + lt, :]
    xc_ref[...] = _silu(acc)


def _conv(xbc, conv_w, conv_b, n_p, seq_p, seq_s):
    total, c = xbc.shape
    lt = math.gcd(seq_p, 256)
    halo = 8
    nblk = total // halo
    per = lt // halo
    return pl.pallas_call(
        functools.partial(_conv_kernel, lt=lt, n_p=n_p, seq_p=seq_p, seq_s=seq_s),
        grid=(total // lt,),
        in_specs=[pl.BlockSpec((lt, c), lambda i: (i, 0)),
                  pl.BlockSpec((halo, c), lambda i: (jnp.maximum(i * per - 1, 0), 0)),
                  pl.BlockSpec((halo, c), lambda i: (jnp.minimum((i + 1) * per, nblk - 1), 0)),
                  _resident(conv_w.shape), _resident((1, c))],
        out_specs=pl.BlockSpec((lt, c), lambda i: (i, 0)),
        out_shape=jax.ShapeDtypeStruct((total, c), F32),
        compiler_params=_params("parallel"),
        name="ssd_conv",
    )(xbc, xbc, xbc, conv_w, conv_b.reshape(1, c))


def _ssd_kernel(xs_ref, b_ref, c_ref, dt_ref, dtt_ref, alr_ref, alc_ref, dbr_ref, dbc_ref, dsk_ref, h0_ref,
                y_ref, fin_ref, st_ref, *, n_chunks, n_p, seq_p, seq_s):
    q = SSD_Q
    d = pl.program_id(0)
    ci = pl.program_id(1)
    blocks = D_INNER // LANES
    tok = jnp.where(d == 0, ci, n_chunks - 1 - ci) * q
    in_p = tok < n_p
    pos = jnp.where(in_p, tok % seq_p, (tok - n_p) % seq_s)
    first = pos == 0
    last = pos + q == jnp.where(in_p, seq_p, seq_s)
    start = jnp.where(d == 0, first, last)
    end = jnp.where(d == 0, last, first)

    @pl.when(jnp.logical_and(start, in_p))
    def _():
        st_ref[...] = jnp.zeros(st_ref.shape, F32)

    @pl.when(jnp.logical_and(start, jnp.logical_not(in_p)))
    def _():
        h0 = h0_ref[...].reshape(D_INNER, SSD_N)
        for kb in range(blocks):
            st_ref[:, LANES * kb:LANES * (kb + 1)] = h0[LANES * kb:LANES * (kb + 1), :].T

    skip_on = jnp.where(d == 0, 1.0, 0.0)
    sgn = 1 - 2 * d
    li = lax.broadcasted_iota(jnp.int32, (q, q), 0)
    si = lax.broadcasted_iota(jnp.int32, (q, q), 1)
    causal = (li - si) * sgn >= 0
    tri_l = jnp.where(causal, 1.0, 0.0)
    tri_r = jnp.where((si - li) * sgn >= 0, 1.0, 0.0)
    a_row = -jnp.exp(alr_ref[...])
    a_col = -jnp.exp(alc_ref[...])
    dt_both = dt_ref[...]
    dt_col = _softplus(jnp.where(d == 0, dt_both[:, 0:SSD_H], dt_both[:, SSD_H:2 * SSD_H]) + dbr_ref[...])
    dt_row = _softplus(dtt_ref[...] + dbc_ref[...])
    cs_col = _dot_hi(tri_l, dt_col * a_row)
    da_row = dt_row * a_col
    cs_row = _dot_hi(da_row, tri_r)
    tot = jnp.sum(da_row, axis=-1, keepdims=True)
    w_row = dt_row * jnp.exp(tot - cs_row)
    etot = jnp.exp(tot)
    src_row = cs_row - jnp.log(dt_row)
    w_row_bf = w_row.astype(BF)
    first_head = lax.broadcasted_iota(jnp.int32, (q, LANES), 1) < SSD_P
    first_head2 = lax.broadcasted_iota(jnp.int32, (q + SSD_N, LANES), 1) < SSD_P
    heads_per_group = SSD_H // SSD_G
    gw = heads_per_group * SSD_P
    for g in range(SSD_G):
        bg = b_ref[:, SSD_N * g:SSD_N * (g + 1)]
        cg = c_ref[:, SSD_N * g:SSD_N * (g + 1)].astype(BF)
        cb = _dot_nt(cg, bg.astype(BF)).astype(BF)
        bgt = bg.T.astype(BF)
        c_state = _dot(cg, st_ref[:, gw * g:gw * (g + 1)].astype(BF))
        for jp in range(heads_per_group // 2):
            pi = g * (heads_per_group // 2) + jp
            cols = slice(LANES * pi, LANES * (pi + 1))
            xp = xs_ref[:, cols]
            xb = xp.astype(BF)
            res, grow = [], []
            for hh in range(2):
                h = 2 * pi + hh
                csc = jnp.broadcast_to(cs_col[:, h:h + 1], (q, q))
                grow.append(jnp.exp(csc))
                m = jnp.exp(jnp.where(causal, csc - src_row[h:h + 1, :], -jnp.inf)).astype(BF) * cb
                bw = bgt * w_row_bf[h:h + 1, :]
                res.append(_dot(jnp.concatenate([m, bw], axis=0), xb))
            both = jnp.where(first_head2, res[0], res[1])
            keep = jnp.where(first_head, etot[2 * pi:2 * pi + 1, :], etot[2 * pi + 1:2 * pi + 2, :])
            st_ref[:, cols] = keep * st_ref[:, cols] + both[q:q + SSD_N, :]
            y_off = jnp.where(first_head, grow[0], grow[1]) * c_state[:, LANES * jp:LANES * (jp + 1)]
            y_ref[:, cols] = (both[0:q, :] + y_off + (skip_on * dsk_ref[:, cols]) * xp).astype(y_ref.dtype)

    @pl.when(jnp.logical_and(end, in_p))
    def _():
        for kb in range(blocks):
            t = st_ref[:, LANES * kb:LANES * (kb + 1)].T
            fin_ref[2 * kb:2 * kb + 2] = t.reshape(2, SSD_P, SSD_N)


def _ssd(xc, dt_raw, dtt_raw, dt_bias, a_log, d_skip, h0, n_p, seq_p, seq_s):
    q = SSD_Q
    total = xc.shape[0]
    n_chunks = total // q
    n_prompt = n_p // seq_p
    n_sample = h0.shape[0]

    def blk(d, c):
        return jnp.where(d == 0, c, n_chunks - 1 - c)

    def h0_idx(d, c):
        return (jnp.clip((blk(d, c) * q - n_p) // seq_s, 0, n_sample - 1), d, 0, 0, 0)

    def fin_idx(d, c):
        return (jnp.minimum(blk(d, c) * q // seq_p, n_prompt - 1), d, 0, 0, 0)

    state_block = (None, None, SSD_H, SSD_P, SSD_N)
    return pl.pallas_call(
        functools.partial(_ssd_kernel, n_chunks=n_chunks, n_p=n_p, seq_p=seq_p, seq_s=seq_s),
        grid=(2, n_chunks),
        in_specs=[pl.BlockSpec((q, D_INNER), lambda d, c: (blk(d, c), 0)),
                  pl.BlockSpec((q, SSD_GN), lambda d, c: (blk(d, c), D_INNER // SSD_GN)),
                  pl.BlockSpec((q, SSD_GN), lambda d, c: (blk(d, c), D_INNER // SSD_GN + 1)),
                  pl.BlockSpec((q, 2 * SSD_H), lambda d, c: (blk(d, c), 0)),
                  pl.BlockSpec((SSD_H, q), lambda d, c: (d, blk(d, c))),
                  pl.BlockSpec((None, 1, SSD_H), lambda d, c: (d, 0, 0)),
                  pl.BlockSpec((None, SSD_H, 1), lambda d, c: (d, 0, 0)),
                  pl.BlockSpec((None, 1, SSD_H), lambda d, c: (d, 0, 0)),
                  pl.BlockSpec((None, SSD_H, 1), lambda d, c: (d, 0, 0)),
                  _resident((1, D_INNER)),
                  pl.BlockSpec(state_block, h0_idx)],
        out_specs=[pl.BlockSpec((None, q, D_INNER), lambda d, c: (d, blk(d, c), 0)),
                   pl.BlockSpec(state_block, fin_idx)],
        out_shape=[jax.ShapeDtypeStruct((2, total, D_INNER), BF),
                   jax.ShapeDtypeStruct((n_prompt, 2, SSD_H, SSD_P, SSD_N), F32)],
        scratch_shapes=[pltpu.VMEM((SSD_N, D_INNER), F32)],
        compiler_params=_params("parallel", "arbitrary"),
        name="ssd_scan",
    )(xc, xc, xc, dt_raw, dtt_raw, a_log.reshape(2, 1, SSD_H), a_log.reshape(2, SSD_H, 1),
      dt_bias.reshape(2, 1, SSD_H), dt_bias.reshape(2, SSD_H, 1), d_skip, h0)


def _odd_out_kernel(x_ref, mod_ref, yf_ref, yb_ref, z_ref, ng_ref, wout_ref,
                    gf_ref, wg_ref, wu_ref, wd_ref, fg_ref, *o_refs, p_steps, final):
    y = (yf_ref[...].astype(F32) + yb_ref[...].astype(F32)) * _silu(z_ref[...])
    y = _rms(y, ng_ref[...])
    x1 = x_ref[...] + mod_ref[2:3, :] * _dot(y.astype(BF), wout_ref[...])
    x2 = _ffn(x1, mod_ref, gf_ref, wg_ref, wu_ref, wd_ref)
    if not final:
        o_refs[0][...] = x2
        return
    out = _rms(x2, fg_ref[...])
    is_p = pl.program_id(0) < p_steps

    @pl.when(is_p)
    def _():
        o_refs[0][...] = out

    @pl.when(jnp.logical_not(is_p))
    def _():
        o_refs[1][...] = out


def _odd_out(st, x, mods, y, z, norm_g, w_out, g_ffn, wg, wu, wd, final_g, final):
    ydir = lambda d: pl.BlockSpec((None, st.tm, D_INNER), lambda i: (d, i, 0))
    return pl.pallas_call(
        functools.partial(_odd_out_kernel, p_steps=st.p_steps, final=final),
        grid=(st.steps,),
        in_specs=[st.row_spec(D_MODEL), st.mod_spec(), ydir(0), ydir(1), st.row_spec(D_INNER),
                  _resident((1, D_INNER)), _resident(w_out.shape),
                  _resident((1, D_MODEL)), _resident(wg.shape), _resident(wu.shape), _resident(wd.shape),
                  _resident((1, D_MODEL))],
        out_specs=[st.prompt_spec(D_MODEL), st.sample_spec(D_MODEL)] if final else st.row_spec(D_MODEL),
        out_shape=([jax.ShapeDtypeStruct((st.n_p, D_MODEL), F32),
                    jax.ShapeDtypeStruct((st.total - st.n_p, D_MODEL), F32)] if final
                   else jax.ShapeDtypeStruct((st.total, D_MODEL), F32)),
        compiler_params=_params("arbitrary"),
        name="odd_out_ffn",
    )(x, mods, y, y, z, norm_g, w_out, g_ffn, wg, wu, wd, final_g)


def kernel(x_prompt, x_sample, cache_na_k, cache_na_v, state_s5_re, state_s5_im, state_ssd, c, c_ctx, norm_mix_g, norm_ffn_g, ada_w, ada_b, ffn_w_gate, ffn_w_up, ffn_w_down, ev_w_in, ev_w_out, s5_a_re, s5_a_im, s5_log_dt, s5_b_re, s5_b_im, s5_c_re, s5_c_im, s5_d, s5_w_glu, na_rpb, od_w_in, od_conv_w, od_conv_b, ssd_a_log, ssd_dt_bias, ssd_d, ssd_norm_g, od_w_out, final_norm_g):
    bp, seq_p, d = x_prompt.shape
    bs, seq_s, _ = x_sample.shape
    depth = ada_w.shape[0]
    n_p = bp * seq_p
    assert d == D_MODEL and n_p % seq_s == 0
    st = _Stream(n_p, seq_s, bs, tm=math.gcd(512, math.gcd(n_p, seq_s)))

    xp = x_prompt.reshape(n_p, d)
    xs = x_sample.reshape(bs * seq_s, d)
    cond = jnp.concatenate([c_ctx[None, :], c, jnp.zeros((8 - 1 - bs, d), F32)], axis=0)
    mods = _ada(cond, ada_w, ada_b).reshape(depth, 8, ADA_CHUNKS, d)
    row = lambda t: t.reshape(1, -1)

    new_k, new_v, new_s5_re, new_s5_im, new_ssd = [], [], [], [], []
    for layer in range(depth):
        wg = ffn_w_gate[layer].astype(BF)
        wu = ffn_w_up[layer].astype(BF)
        wd = ffn_w_down[layer].astype(BF)
        g_mix = row(norm_mix_g[layer])
        g_ffn = row(norm_ffn_g[layer])
        if layer % 2 == 0:
            e = layer // 2
            if layer > 0:
                xp, xs = x[:n_p], x[n_p:]
            u, q, k, v = _even_in(st, xp, xs, mods[layer], g_mix, ev_w_in[e].astype(BF))
            attn_p = _ctx_attn(q, k, v, bp, seq_p)
            heads = lambda t: t[:n_p].reshape(bp, seq_p, NA_HEADS, NA_HD).transpose(0, 2, 1, 3)
            new_k.append(heads(k))
            new_v.append(heads(v))
            ctx = lambda t: t[:, e].transpose(0, 2, 1, 3).reshape(bs, -1, D_NA).astype(BF)
            attn_s = _na_attn(q, k, v, ctx(cache_na_k), ctx(cache_na_v), _bias_table(na_rpb[e]),
                              bs, seq_s, n_p // seq_s)
            v_op, bw, cw, sc = _s5_prep(s5_a_re[e], s5_a_im[e], s5_log_dt[e], s5_b_re[e], s5_b_im[e],
                                        s5_c_re[e], s5_c_im[e])
            y, fin = _s5(u, v_op, bw, cw, sc, _state_planes(state_s5_re[:, e], state_s5_im[:, e]),
                         n_p, seq_p, seq_s)
            fre, fim = _planes_state(fin)
            new_s5_re.append(fre)
            new_s5_im.append(fim)
            x = _even_out(st, xp, xs, mods[layer], y, u, attn_p, attn_s, row(s5_d[e]),
                          s5_w_glu[e].astype(BF), ev_w_out[e].astype(BF), g_ffn, wg, wu, wd)
        else:
            o = layer // 2
            z, xbc, dt_raw, dtt_raw = _odd_in(st, x, mods[layer], g_mix, od_w_in[o].astype(BF))
            xc = _conv(xbc, od_conv_w[o], od_conv_b[o], n_p, seq_p, seq_s)
            y, fin = _ssd(xc, dt_raw, dtt_raw, ssd_dt_bias[o], ssd_a_log[o], row(jnp.repeat(ssd_d[o], SSD_P)),
                          state_ssd[:, o], n_p, seq_p, seq_s)
            new_ssd.append(fin)
            x = _odd_out(st, x, mods[layer], y, z, row(ssd_norm_g[o]), od_w_out[o].astype(BF),
                         g_ffn, wg, wu, wd, row(final_norm_g), layer == depth - 1)
    if depth % 2 == 1:
        raise NotImplementedError("final norm is fused into the last (odd) layer")
    y_prompt = x[0].reshape(bp, seq_p, d)
    y_sample = x[1].reshape(bs, seq_s, d)
    return (y_prompt, y_sample, jnp.stack(new_k, axis=1), jnp.stack(new_v, axis=1),
            jnp.stack(new_s5_re, axis=1), jnp.stack(new_s5_im, axis=1), jnp.stack(new_ssd, axis=1))
```

```python
import functools
import math

import jax
import jax.numpy as jnp
from jax import lax
from jax.experimental import pallas as pl
from jax.experimental.pallas import tpu as pltpu

F32 = jnp.float32
BF = jnp.bfloat16
HI = lax.Precision.HIGHEST

D_MODEL = 1024
EPS = 1e-6
ADA_CHUNKS = 6
GRID_W = 64
D_S5 = 512
S5_P = 16
S5_G = D_S5 // S5_P
S5_N = 64
S5_MIN_DECAY = 1e-4
S5_Q = 16
D_NA = 512
NA_HD = 64
NA_HEADS = D_NA // NA_HD
NA_WIN_ROWS = 8
NA_WIN_COLS = 16
NA_ROWS = 4
D_INNER = 2048
SSD_P = 64
SSD_H = D_INNER // SSD_P
SSD_G = 4
SSD_N = 128
SSD_Q = 128
SSD_CONV = 5
SSD_GN = SSD_G * SSD_N
SSD_CONV_DIM = D_INNER + 2 * SSD_GN
D_FF = 2816

V7X_VMEM_BYTES = 64 * 1024 * 1024
VMEM_LIMIT = V7X_VMEM_BYTES - 4 * 1024 * 1024
LANES = 128
NEG_BIG = -1e30


def _params(*sem):
    return pltpu.CompilerParams(dimension_semantics=sem, vmem_limit_bytes=VMEM_LIMIT)


def _resident(shape):
    nd = len(shape)
    return pl.BlockSpec(shape, lambda *_: (0,) * nd, pipeline_mode=pl.Buffered(1))


def _layer_resident(shape, layer):
    nd = len(shape)
    return pl.BlockSpec((None,) + tuple(shape[1:]), lambda *_: (layer,) + (0,) * (nd - 1), pipeline_mode=pl.Buffered(1))


def _dot(a, b):
    return jnp.dot(a, b, preferred_element_type=F32)


def _dot_nt(a, b):
    return lax.dot_general(a, b, (((1,), (1,)), ((), ())), preferred_element_type=F32)


def _dot_hi(a, b):
    return jnp.dot(a, b, preferred_element_type=F32, precision=HI)


def _sigmoid(x):
    return 1.0 / (1.0 + jnp.exp(-x))


def _silu(x):
    return x * _sigmoid(x)


def _softplus(x):
    return jnp.maximum(x, 0.0) + jnp.log1p(jnp.exp(-jnp.abs(x)))


def _rms(x, g):
    return x * lax.rsqrt(jnp.mean(x * x, axis=-1, keepdims=True) + EPS) * g


def _norm_mod(x, g, shift, scale):
    return _rms(x, g) * (1.0 + scale) + shift


def _ada_kernel(c_ref, w_ref, b_ref, o_ref):
    c = c_ref[...]
    o_ref[...] = _dot(_silu(c).astype(BF), w_ref[...].astype(BF)) + b_ref[...]


def _ada(cond, ada_w, ada_b):
    depth, d, n = ada_w.shape
    tn = 1536
    return pl.pallas_call(
        _ada_kernel,
        grid=(depth, n // tn),
        in_specs=[pl.BlockSpec((8, d), lambda l, j: (0, 0)),
                  pl.BlockSpec((None, d, tn), lambda l, j: (l, 0, j)),
                  pl.BlockSpec((None, 1, tn), lambda l, j: (l, 0, j))],
        out_specs=pl.BlockSpec((None, 8, tn), lambda l, j: (l, 0, j)),
        out_shape=jax.ShapeDtypeStruct((depth, 8, n), F32),
        compiler_params=_params("parallel", "parallel"),
        name="adaln",
    )(cond, ada_w, ada_b.reshape(depth, 1, n))


class _Stream:
    def __init__(self, n_p, len_s, n_s, tm):
        assert n_p % tm == 0 and len_s % tm == 0
        self.n_p, self.len_s, self.n_s, self.tm = n_p, len_s, n_s, tm
        self.total = n_p + len_s * n_s
        self.steps = self.total // tm
        self.p_steps = n_p // tm

    def group(self, i):
        t = i * self.tm
        return jnp.where(t < self.n_p, 0, 1 + (t - self.n_p) // self.len_s)

    def mod_spec(self):
        return pl.BlockSpec((None, ADA_CHUNKS, D_MODEL), lambda i: (self.group(i), 0, 0))

    def row_spec(self, width, col=0):
        return pl.BlockSpec((self.tm, width), lambda i: (i, col))

    def prompt_spec(self, width):
        return pl.BlockSpec((self.tm, width), lambda i: (jnp.minimum(i, self.p_steps - 1), 0))

    def sample_spec(self, width):
        return pl.BlockSpec((self.tm, width), lambda i: (jnp.maximum(i - self.p_steps, 0), 0))


def _ffn(x1, mod_ref, g_ref, wg_ref, wu_ref, wd_ref):
    h = _norm_mod(x1, g_ref[...], mod_ref[3:4, :], mod_ref[4:5, :]).astype(BF)
    hid = (_silu(_dot(h, wg_ref[...])) * _dot(h, wu_ref[...])).astype(BF)
    return x1 + mod_ref[5:6, :] * _dot(hid, wd_ref[...])


def _even_in_kernel(xp_ref, xs_ref, mod_ref, g_ref, w_ref, u_ref, q_ref, k_ref, v_ref, *, p_steps):
    x = jnp.where(pl.program_id(0) < p_steps, xp_ref[...], xs_ref[...])
    h = _norm_mod(x, g_ref[...], mod_ref[0:1, :], mod_ref[1:2, :])
    r = _dot(h.astype(BF), w_ref[...])
    u_ref[...] = r[:, 0:D_S5]
    q_ref[...] = r[:, D_S5:D_S5 + D_NA] * (NA_HD ** -0.5)
    k_ref[...] = r[:, D_S5 + D_NA:D_S5 + 2 * D_NA]
    v_ref[...] = r[:, D_S5 + 2 * D_NA:D_S5 + 3 * D_NA]


def _even_in(st, xp, xs, mods, g, w_in):
    n_out = w_in.shape[1]
    out = jax.ShapeDtypeStruct((st.total, D_S5), F32)
    return pl.pallas_call(
        functools.partial(_even_in_kernel, p_steps=st.p_steps),
        grid=(st.steps,),
        in_specs=[st.prompt_spec(D_MODEL), st.sample_spec(D_MODEL), st.mod_spec(), _resident((1, D_MODEL)),
                  _resident((D_MODEL, n_out))],
        out_specs=[st.row_spec(D_S5)] * 4,
        out_shape=[out] * 4,
        compiler_params=_params("parallel"),
        name="even_in",
    )(xp, xs, mods, g, w_in)


def _ctx_attn_kernel(q_ref, k_ref, v_ref, o_ref):
    lane = lax.broadcasted_iota(jnp.int32, (q_ref.shape[0], LANES), 1)
    for pair in range(D_NA // LANES):
        cols = slice(LANES * pair, LANES * (pair + 1))
        q = q_ref[:, cols]
        k = k_ref[:, cols].astype(BF)
        v = v_ref[:, cols].astype(BF)
        outs = []
        for h in range(2):
            in_head = (lane >= NA_HD * h) & (lane < NA_HD * (h + 1))
            qh = jnp.where(in_head, q, 0.0).astype(BF)
            s = _dot_nt(qh, k)
            p = jnp.exp(s - jnp.max(s, axis=-1, keepdims=True))
            l = jnp.sum(p, axis=-1, keepdims=True)
            outs.append(_dot(p.astype(BF), v) / l)
        o_ref[:, cols] = jnp.where(lane < NA_HD, outs[0], outs[1])


def _ctx_attn(q, k, v, n_seq, seq):
    spec = pl.BlockSpec((seq, D_NA), lambda b: (b, 0))
    return pl.pallas_call(
        _ctx_attn_kernel,
        grid=(n_seq,),
        in_specs=[spec, spec, spec],
        out_specs=spec,
        out_shape=jax.ShapeDtypeStruct((n_seq * seq, D_NA), F32),
        compiler_params=_params("parallel"),
        name="ctx_attn",
    )(q, k, v)


NA_N_DR = 2 * NA_WIN_ROWS - 1
NA_BOTH, NA_LEFT, NA_RIGHT = 0, NA_N_DR - 1, 2 * NA_N_DR - 1
NA_NONE = 3 * NA_N_DR - 1
NA_UNION = NA_WIN_ROWS + NA_ROWS


def _bias_kernel(rpb_ref, o_ref):
    h = pl.program_id(0)
    n_dc = 2 * NA_WIN_COLS - 1
    wq = lax.broadcasted_iota(jnp.int32, (GRID_W, LANES), 0)
    lane = lax.broadcasted_iota(jnp.int32, (GRID_W, LANES), 1)
    wk = lane & (GRID_W - 1)
    left = lane < GRID_W
    col_start = jnp.clip(wq - NA_WIN_COLS // 2, 0, GRID_W - NA_WIN_COLS)
    ok = (wk >= col_start) & (wk < col_start + NA_WIN_COLS)
    dc = jnp.clip(wk - wq, -(NA_WIN_COLS - 1), NA_WIN_COLS - 1) + (NA_WIN_COLS - 1)

    def entry(e, carry):
        is_both = e < NA_LEFT
        is_left = jnp.logical_and(e >= NA_LEFT, e < NA_RIGHT)
        is_right = jnp.logical_and(e >= NA_RIGHT, e < NA_NONE)
        d_left = jnp.where(is_both, e, jnp.where(is_left, e - NA_LEFT, 0))
        d_right = jnp.where(is_both, e + 1, jnp.where(is_right, e - NA_RIGHT, 0))
        left_on = jnp.where(jnp.logical_or(is_both, is_left), 1, 0)
        right_on = jnp.where(jnp.logical_or(is_both, is_right), 1, 0)
        acc = jnp.zeros((GRID_W, LANES), F32)
        for kk in range(n_dc):
            lo = rpb_ref[(h * NA_N_DR + d_left) * n_dc + kk]
            hi = rpb_ref[(h * NA_N_DR + d_right) * n_dc + kk]
            acc = jnp.where(dc == kk, jnp.where(left, lo, hi), acc)
        side_on = jnp.where(left, left_on, right_on) > 0
        o_ref[e] = jnp.where(ok & side_on, acc, NEG_BIG)
        return carry

    lax.fori_loop(0, NA_NONE + 1, entry, 0)


def _bias_table(rpb):
    return pl.pallas_call(
        _bias_kernel,
        grid=(NA_HEADS,),
        in_specs=[pl.BlockSpec(memory_space=pltpu.SMEM)],
        out_specs=pl.BlockSpec((None, NA_NONE + 1, GRID_W, LANES), lambda h: (h, 0, 0, 0)),
        out_shape=jax.ShapeDtypeStruct((NA_HEADS, NA_NONE + 1, GRID_W, LANES), F32),
        compiler_params=_params("parallel"),
        name="na_bias",
    )(rpb.reshape(-1))


def _na_kernel(q_ref, k_ref, v_ref, ck_ref, cv_ref, bias_ref, o_ref, kb_ref, vb_ref, *, rows):
    kb_ref[...] = k_ref[...].astype(BF)
    vb_ref[...] = v_ref[...].astype(BF)
    ck = ck_ref[...]
    cv = cv_ref[...]
    kh = NA_WIN_ROWS
    nq = NA_ROWS * GRID_W
    lane = lax.broadcasted_iota(jnp.int32, (nq, LANES), 1)

    def body(g, carry):
        r0 = g * NA_ROWS
        first = jnp.clip(r0 - kh // 2, 0, rows - NA_UNION)
        qoff = pl.multiple_of(r0 * GRID_W, nq)
        q = q_ref[pl.ds(qoff, nq), :]
        koff = pl.multiple_of(first * GRID_W, GRID_W)
        kw = kb_ref[pl.ds(koff, NA_UNION * GRID_W), :]
        vw = vb_ref[pl.ds(koff, NA_UNION * GRID_W), :]
        tile_idx = []
        for j in range(NA_ROWS):
            r = r0 + j
            start = jnp.clip(r - kh // 2, 0, rows - kh)
            for ip in range(NA_UNION // 2):
                k0 = first + 2 * ip
                in0 = jnp.logical_and(k0 >= start, k0 < start + kh)
                in1 = jnp.logical_and(k0 + 1 >= start, k0 + 1 < start + kh)
                dr0 = k0 - r + (NA_WIN_ROWS - 1)
                idx = jnp.where(jnp.logical_and(in0, in1), NA_BOTH + dr0,
                                jnp.where(in0, NA_LEFT + dr0, jnp.where(in1, NA_RIGHT + dr0 + 1, NA_NONE)))
                tile_idx.append(jnp.clip(idx, 0, NA_NONE))
        outs = []
        for h in range(2):
            in_head = (lane >= NA_HD * h) & (lane < NA_HD * (h + 1))
            qh = jnp.where(in_head, q, 0.0).astype(BF)
            per_row = NA_UNION // 2
            bias = jnp.concatenate(
                [jnp.concatenate([bias_ref[h, tile_idx[j * per_row + ip]] for ip in range(per_row)], axis=1)
                 for j in range(NA_ROWS)], axis=0)
            s_loc = _dot_nt(qh, kw) + bias
            s_ctx = _dot_nt(qh, ck)
            m = jnp.maximum(jnp.max(s_loc, axis=-1, keepdims=True), jnp.max(s_ctx, axis=-1, keepdims=True))
            p_loc = jnp.exp(s_loc - m)
            p_ctx = jnp.exp(s_ctx - m)
            l = jnp.sum(p_loc, axis=-1, keepdims=True) + jnp.sum(p_ctx, axis=-1, keepdims=True)
            outs.append((_dot(p_loc.astype(BF), vw) + _dot(p_ctx.astype(BF), cv)) / l)
        o_ref[pl.ds(qoff, nq), :] = jnp.where(lane < NA_HD, outs[0], outs[1])
        return carry

    lax.fori_loop(0, rows // NA_ROWS, body, 0, unroll=2)


def _na_attn(q, k, v, ck, cv, bias, n_seq, seq, row_base):
    rows = seq // GRID_W
    assert rows >= NA_UNION and rows % NA_ROWS == 0
    past = ck.shape[1]
    spec = pl.BlockSpec((seq, LANES), lambda b, p: (row_base + b, p))
    cspec = pl.BlockSpec((None, past, LANES), lambda b, p: (b, 0, p))
    return pl.pallas_call(
        functools.partial(_na_kernel, rows=rows),
        grid=(n_seq, D_NA // LANES),
        in_specs=[spec, spec, spec, cspec, cspec,
                  pl.BlockSpec((2, NA_NONE + 1, GRID_W, LANES), lambda b, p: (p, 0, 0, 0))],
        out_specs=pl.BlockSpec((seq, LANES), lambda b, p: (b, p)),
        out_shape=jax.ShapeDtypeStruct((n_seq * seq, D_NA), F32),
        scratch_shapes=[pltpu.VMEM((seq, LANES), BF), pltpu.VMEM((seq, LANES), BF)],
        compiler_params=_params("parallel", "parallel"),
        name="na_attn",
    )(q, k, v, ck, cv, bias)


S5_BLK = LANES // S5_P
S5_NB = S5_G // S5_BLK
S5_ST = S5_BLK * S5_N
S5_W = S5_Q * LANES


def _cexp(zr, zi):
    e = jnp.exp(zr)
    return e * jnp.cos(zi), e * jnp.sin(zi)


def _cmul(ar, ai, br, bi):
    return ar * br - ai * bi, ar * bi + ai * br


def _s5_prep_kernel(arr_ref, ari_ref, ldr_ref, acr_ref, aci_ref, ldc_ref, btr_ref, bti_ref, ctr_ref, cti_ref,
                    v_ref, bw_ref, cw_ref, sc_ref,
                    bbr_s, bbi_s, ccr_s, cci_s, pwr_s, pwi_s, pcr_s, pci_s, p0_s, p1_s):
    t = pl.program_id(1)
    q = S5_Q
    nst = S5_ST

    @pl.when(t == 0)
    def _():
        expand = jnp.where(lax.broadcasted_iota(jnp.int32, (S5_P, 2 * LANES), 0)
                           == (lax.broadcasted_iota(jnp.int32, (S5_P, 2 * LANES), 1) & (S5_P - 1)), 1.0, 0.0)
        row_g = lax.broadcasted_iota(jnp.int32, (LANES, nst), 0) >> 4
        col_g = lax.broadcasted_iota(jnp.int32, (LANES, nst), 1) >> 6
        st_g = lax.broadcasted_iota(jnp.int32, (nst, LANES), 0) >> 6
        ch_g = lax.broadcasted_iota(jnp.int32, (nst, LANES), 1) >> 4
        lane_d = lax.broadcasted_iota(jnp.int32, (nst, LANES), 1) >> 4
        taps = []
        for d in range(2):
            ar = jnp.minimum(arr_ref[d], -S5_MIN_DECAY)
            ai = ari_ref[d]
            dt = jnp.exp(ldr_ref[d])
            abr, abi = _cexp(ar * dt, ai * dt)
            den = ar * ar + ai * ai
            cfr = ((abr - 1.0) * ar + abi * ai) / den
            cfi = (abi * ar - (abr - 1.0) * ai) / den
            bbr, bbi = _cmul(cfr, cfi, btr_ref[d], bti_ref[d])
            bbr = jnp.where(row_g == col_g, jnp.concatenate([bbr] * S5_BLK, axis=0), 0.0)
            bbi = jnp.where(row_g == col_g, jnp.concatenate([bbi] * S5_BLK, axis=0), 0.0)
            bbr_s[d] = bbr
            bbi_s[d] = bbi
            pr = jnp.ones((1, nst), F32)
            pi = jnp.zeros((1, nst), F32)
            for e in range(q + 1):
                pwr_s[d, e] = pr
                pwi_s[d, e] = pi
                pr, pi = _cmul(pr, pi, abr, abi)
            sc_ref[2 * d:2 * d + 1, :] = pwr_s[d, q]
            sc_ref[2 * d + 1:2 * d + 2, :] = pwi_s[d, q]
            arc = jnp.minimum(acr_ref[d], -S5_MIN_DECAY)
            dtc = jnp.exp(ldc_ref[d])
            acr, aci = _cexp(jnp.broadcast_to(arc * dtc, (nst, LANES)), jnp.broadcast_to(aci_ref[d] * dtc, (nst, LANES)))
            pr = jnp.ones((nst, LANES), F32)
            pi = jnp.zeros((nst, LANES), F32)
            for e in range(q + 1):
                pcr_s[d, e] = pr
                pci_s[d, e] = pi
                pr, pi = _cmul(pr, pi, acr, aci)
            cxr = _dot_hi(ctr_ref[d], expand)
            cxi = _dot_hi(cti_ref[d], expand)
            ccr_s[d] = jnp.where(st_g == ch_g, cxr[:, 0:LANES], 0.0)
            cci_s[d] = jnp.where(st_g == ch_g, cxi[:, 0:LANES], 0.0)
            pws_r, pws_i = [], []
            for k in range(2):
                sel_r = jnp.zeros((nst, LANES), F32)
                sel_i = jnp.zeros((nst, LANES), F32)
                for j in range(S5_BLK):
                    dl = S5_BLK * k + j
                    e = dl if d == 0 else q - 1 - dl
                    sel_r = jnp.where(lane_d == j, pcr_s[d, e], sel_r)
                    sel_i = jnp.where(lane_d == j, pci_s[d, e], sel_i)
                pws_r.append(sel_r)
                pws_i.append(sel_i)
            ggr, ggi = _cmul(cxr, cxi, jnp.concatenate(pws_r, axis=1), jnp.concatenate(pws_i, axis=1))
            taps.append(_dot_hi(bbr, ggr) - _dot_hi(bbi, ggi))
        lane2 = lax.broadcasted_iota(jnp.int32, (LANES, 2 * LANES), 1)
        kt0 = taps[0] + jnp.where(lane2 < S5_P, pltpu.roll(taps[1], S5_P, 1), 0.0)
        kt1 = jnp.where(lane2 >= 2 * LANES - S5_P, 0.0, taps[1])
        er = lax.broadcasted_iota(jnp.int32, (2 * LANES, S5_W), 0)
        ec = lax.broadcasted_iota(jnp.int32, (2 * LANES, S5_W), 1)
        place = jnp.where(((er >> 4) == (ec >> 7)) & ((er & (S5_P - 1)) == (ec & (S5_P - 1))), 1.0, 0.0).astype(BF)
        own = (lax.broadcasted_iota(jnp.int32, (LANES, S5_W), 0) >> 4) == (
            (lax.broadcasted_iota(jnp.int32, (LANES, S5_W), 1) >> 4) & (S5_BLK - 1))
        zeros = jnp.zeros((LANES, S5_W), BF)
        p0_s[:, 0:S5_W] = zeros
        p0_s[:, S5_W:2 * S5_W] = jnp.where(own, _dot(kt0.astype(BF), place), 0.0).astype(BF)
        p1_s[:, 0:S5_W] = jnp.where(own, _dot(kt1.astype(BF), place), 0.0).astype(BF)
        p1_s[:, S5_W:2 * S5_W] = zeros

    off0 = pl.multiple_of(S5_W - LANES * t, LANES)
    off1 = pl.multiple_of(LANES * (q - 1 - t), LANES)
    v_ref[...] = p0_s[:, pl.ds(off0, S5_W)] + p1_s[:, pl.ds(off1, S5_W)]
    for d in range(2):
        e_b = (q - 1 - t) if d == 0 else t
        br, bi = _cmul(pwr_s[d, e_b], pwi_s[d, e_b], bbr_s[d], bbi_s[d])
        bw_ref[:, 2 * nst * d:2 * nst * d + nst] = br.astype(BF)
        bw_ref[:, 2 * nst * d + nst:2 * nst * (d + 1)] = bi.astype(BF)
        e_c = (t + 1) if d == 0 else (q - t)
        gr, gi = _cmul(ccr_s[d], cci_s[d], pcr_s[d, e_c], pci_s[d, e_c])
        cw_ref[2 * nst * d:2 * nst * d + nst, :] = gr.astype(BF)
        cw_ref[2 * nst * d + nst:2 * nst * (d + 1), :] = (-gi).astype(BF)


def _s5_prep(a_re, a_im, log_dt, b_re, b_im, c_re, c_im):
    nst, nb, q = S5_ST, S5_NB, S5_Q
    row = lambda t: t.reshape(2, nb, 1, nst)
    col = lambda t: t.reshape(2, nb, nst, 1)
    ld = jnp.broadcast_to(log_dt[:, :, None], (2, S5_G, S5_N))
    bt = lambda t: t.reshape(2, nb, S5_BLK, S5_N, S5_P).transpose(0, 1, 4, 2, 3).reshape(2, nb, S5_P, nst)
    ct = lambda t: t.reshape(2, nb, S5_BLK, S5_P, S5_N).transpose(0, 1, 2, 4, 3).reshape(2, nb, nst, S5_P)
    rspec = pl.BlockSpec((2, None, 1, nst), lambda b, t: (0, b, 0, 0))
    cspec = pl.BlockSpec((2, None, nst, 1), lambda b, t: (0, b, 0, 0))
    btspec = pl.BlockSpec((2, None, S5_P, nst), lambda b, t: (0, b, 0, 0))
    ctspec = pl.BlockSpec((2, None, nst, S5_P), lambda b, t: (0, b, 0, 0))
    big = jax.ShapeDtypeStruct((nb, S5_W, S5_W), BF)
    return pl.pallas_call(
        _s5_prep_kernel,
        grid=(nb, q),
        in_specs=[rspec, rspec, rspec, cspec, cspec, cspec, btspec, btspec, ctspec, ctspec],
        out_specs=[pl.BlockSpec((None, LANES, S5_W), lambda b, t: (b, t, 0)),
                   pl.BlockSpec((None, LANES, S5_W), lambda b, t: (b, t, 0)),
                   pl.BlockSpec((None, S5_W, LANES), lambda b, t: (b, 0, t)),
                   pl.BlockSpec((None, 4, nst), lambda b, t: (b, 0, 0))],
        out_shape=[big, big, big, jax.ShapeDtypeStruct((nb, 4, nst), F32)],
        scratch_shapes=[pltpu.VMEM((2, LANES, nst), F32), pltpu.VMEM((2, LANES, nst), F32),
                        pltpu.VMEM((2, nst, LANES), F32), pltpu.VMEM((2, nst, LANES), F32),
                        pltpu.VMEM((2, q + 1, 1, nst), F32), pltpu.VMEM((2, q + 1, 1, nst), F32),
                        pltpu.VMEM((2, q + 1, nst, LANES), F32), pltpu.VMEM((2, q + 1, nst, LANES), F32),
                        pltpu.VMEM((LANES, 2 * S5_W), BF), pltpu.VMEM((LANES, 2 * S5_W), BF)],
        compiler_params=_params("parallel", "arbitrary"),
        name="s5_prep",
    )(row(a_re), row(a_im), row(ld), col(a_re), col(a_im), col(ld), bt(b_re), bt(b_im), ct(c_re), ct(c_im))


def _s5_kernel(u_ref, v_ref, bw_ref, cw_ref, sc_ref, h0_ref, y_ref, fin_ref, z_s, h_s, *, p_steps, n_sub, mp):
    s = pl.program_id(1)
    m = u_ref.shape[0] // S5_Q
    n_tiles = 4 * S5_ST // LANES
    per = S5_ST // LANES
    half = 2 * per
    chunk_rows = lambda t: pl.ds(t, m, stride=S5_Q)
    tile_rows = lambda k: pl.ds(k, m, stride=n_tiles)
    ucat = jnp.concatenate([u_ref[chunk_rows(t), :].astype(BF) for t in range(S5_Q)], axis=1)
    z = _dot(ucat, bw_ref[...])
    for k in range(n_tiles):
        z_s[tile_rows(k), :] = z[:, LANES * k:LANES * (k + 1)]
    sc = sc_ref[...]
    stack = lambda row: jnp.concatenate([row[:, LANES * j:LANES * (j + 1)] for j in range(per)], axis=0)

    def multipliers(d):
        ar, ai = stack(sc[2 * d:2 * d + 1]), stack(sc[2 * d + 1:2 * d + 2])
        return jnp.concatenate([ar, ar], axis=0), jnp.concatenate([-ai, ai], axis=0)

    a1f, a2f = multipliers(0)
    a1b, a2b = multipliers(1)

    def advance(hf, hb, cf, cb):
        rf = pl.ds(pl.multiple_of(cf * n_tiles, half), half)
        rb = pl.ds(pl.multiple_of(cb * n_tiles + half, half), half)
        h_s[rf, :] = hf
        h_s[rb, :] = hb
        hf = a1f * hf + a2f * pltpu.roll(hf, per, 0) + z_s[rf, :]
        hb = a1b * hb + a2b * pltpu.roll(hb, per, 0) + z_s[rb, :]
        return hf, hb

    unstack = lambda h4: jnp.concatenate([h4[j:j + 1, :] for j in range(per)], axis=1)

    @pl.when(s < p_steps)
    def _():
        for i in range(n_sub):
            hf = jnp.zeros((half, LANES), F32)
            hb = jnp.zeros((half, LANES), F32)
            for c in range(mp):
                hf, hb = advance(hf, hb, i * mp + c, i * mp + mp - 1 - c)
            for pi, h4 in enumerate((hf[0:per], hf[per:half], hb[0:per], hb[per:half])):
                fin_ref[i, pi:pi + 1, :] = unstack(h4)

    @pl.when(s >= p_steps)
    def _():
        h0 = h0_ref[...]
        init = (jnp.concatenate([stack(h0[0:1]), stack(h0[1:2])], axis=0),
                jnp.concatenate([stack(h0[2:3]), stack(h0[3:4])], axis=0))
        lax.fori_loop(0, m, lambda c, h: advance(h[0], h[1], c, m - 1 - c), init)

    hcat = jnp.concatenate([h_s[tile_rows(k), :] for k in range(n_tiles)], axis=1).astype(BF)
    ycat = _dot(ucat, v_ref[...]) + _dot(hcat, cw_ref[...])
    for t in range(S5_Q):
        y_ref[chunk_rows(t), :] = ycat[:, LANES * t:LANES * (t + 1)]


def _s5(u, v, bw, cw, sc, h0, n_p, seq_p, seq_s):
    total = u.shape[0]
    q = S5_Q
    m = seq_s // q
    mp = seq_p // q
    n_sub = seq_s // seq_p
    p_steps = n_p // seq_s
    n_prompt = n_p // seq_p
    wspec = pl.BlockSpec((None, S5_W, S5_W), lambda b, s: (b, 0, 0), pipeline_mode=pl.Buffered(1))
    tok = pl.BlockSpec((seq_s, LANES), lambda b, s: (s, b))
    y, fin = pl.pallas_call(
        functools.partial(_s5_kernel, p_steps=p_steps, n_sub=n_sub, mp=mp),
        grid=(S5_NB, total // seq_s),
        in_specs=[tok, wspec, wspec, wspec,
                  pl.BlockSpec((None, 4, S5_ST), lambda b, s: (b, 0, 0)),
                  pl.BlockSpec((None, 4, S5_ST), lambda b, s: (jnp.maximum(s - p_steps, 0), 0, b))],
        out_specs=[tok, pl.BlockSpec((n_sub, 4, S5_ST), lambda b, s: (jnp.minimum(s, p_steps - 1), 0, b))],
        out_shape=[jax.ShapeDtypeStruct((total, D_S5), F32),
                   jax.ShapeDtypeStruct((n_prompt, 4, S5_G * S5_N), F32)],
        scratch_shapes=[pltpu.VMEM((4 * S5_ST // LANES * m, LANES), F32)] * 2,
        compiler_params=_params("parallel", "arbitrary"),
        name="s5_scan",
    )(u, v, bw, cw, sc, h0)
    return y, fin


def _state_planes(re, im):
    b = re.shape[0]
    return jnp.stack([t[:, d].reshape(b, S5_G * S5_N) for d in range(2) for t in (re, im)], axis=1)


def _planes_state(fin):
    t = fin.reshape(fin.shape[0], 2, 2, S5_G, S5_N)
    return t[:, :, 0], t[:, :, 1]


def _gelu_tanh(x):
    return 0.5 * x * (1.0 + jnp.tanh(math.sqrt(2.0 / math.pi) * (x + 0.044715 * (x * x * x))))


def _even_out_kernel(xp_ref, xs_ref, mod_ref, y_ref, u_ref, ap_ref, as_ref, dsk_ref, wglu_ref, wout_ref,
                     gf_ref, wg_ref, wu_ref, wd_ref, o_ref, *, p_steps):
    is_p = pl.program_id(0) < p_steps
    attn = jnp.where(is_p, ap_ref[...], as_ref[...])
    y = _gelu_tanh(y_ref[...] + u_ref[...] * dsk_ref[...])
    y = y * _sigmoid(_dot(y.astype(BF), wglu_ref[...]))
    out = _dot(y.astype(BF), wout_ref[0:D_S5, :]) + _dot(attn.astype(BF), wout_ref[D_S5:D_S5 + D_NA, :])
    x1 = jnp.where(is_p, xp_ref[...], xs_ref[...]) + mod_ref[2:3, :] * out
    o_ref[...] = _ffn(x1, mod_ref, gf_ref, wg_ref, wu_ref, wd_ref)


def _even_out(st, xp, xs, mods, y, u, attn_p, attn_s, d_skip, w_glu, w_out, g_ffn, wg, wu, wd, layer):
    return pl.pallas_call(
        functools.partial(_even_out_kernel, p_steps=st.p_steps),
        grid=(st.steps,),
        in_specs=[st.prompt_spec(D_MODEL), st.sample_spec(D_MODEL), st.mod_spec(), st.row_spec(D_S5),
                  st.row_spec(D_S5), st.prompt_spec(D_NA), st.sample_spec(D_NA),
                  _resident((1, D_S5)), _resident(w_glu.shape), _resident(w_out.shape),
                  _resident((1, D_MODEL)), _layer_resident(wg.shape, layer), _layer_resident(wu.shape, layer),
                  _layer_resident(wd.shape, layer)],
        out_specs=st.row_spec(D_MODEL),
        out_shape=jax.ShapeDtypeStruct((st.total, D_MODEL), F32),
        compiler_params=_params("parallel"),
        name="even_out_ffn",
    )(xp, xs, mods, y, u, attn_p, attn_s, d_skip, w_glu, w_out, g_ffn, wg, wu, wd)


def _odd_in_kernel(x_ref, mod_ref, g_ref, w_ref, wdtt_ref, z_ref, xbc_ref, dt_ref, dtt_ref):
    h = _norm_mod(x_ref[...], g_ref[...], mod_ref[0:1, :], mod_ref[1:2, :]).astype(BF)
    z_ref[...] = _dot(h, w_ref[:, 0:D_INNER])
    xbc_ref[...] = _dot(h, w_ref[:, D_INNER:D_INNER + SSD_CONV_DIM])
    dt_ref[...] = _dot(h, w_ref[:, D_INNER + SSD_CONV_DIM:D_INNER + SSD_CONV_DIM + 2 * SSD_H])
    dtt_ref[...] = _dot_nt(wdtt_ref[...], h)


def _odd_in(st, x, mods, g, w_in):
    wdtt = w_in[:, D_INNER + SSD_CONV_DIM:].T
    return pl.pallas_call(
        _odd_in_kernel,
        grid=(st.steps,),
        in_specs=[st.row_spec(D_MODEL), st.mod_spec(), _resident((1, D_MODEL)),
                  _resident(w_in.shape), _resident(wdtt.shape)],
        out_specs=[st.row_spec(D_INNER), st.row_spec(SSD_CONV_DIM), st.row_spec(2 * SSD_H),
                   pl.BlockSpec((2 * SSD_H, st.tm), lambda i: (0, i))],
        out_shape=[jax.ShapeDtypeStruct((st.total, D_INNER), F32),
                   jax.ShapeDtypeStruct((st.total, SSD_CONV_DIM), F32),
                   jax.ShapeDtypeStruct((st.total, 2 * SSD_H), F32),
                   jax.ShapeDtypeStruct((2 * SSD_H, st.total), F32)],
        compiler_params=_params("parallel"),
        name="odd_in",
    )(x, mods, g, w_in, wdtt)


def _conv_kernel(x_ref, prev_ref, next_ref, w_ref, b_ref, xc_ref, *, lt, n_p, seq_p, seq_s):
    tok = pl.program_id(0) * lt
    in_p = tok < n_p
    pos = jnp.where(in_p, tok % seq_p, (tok - n_p) % seq_s)
    seq = jnp.where(in_p, seq_p, seq_s)
    halo = prev_ref.shape[0]
    prev = jnp.where(pos == 0, 0.0, prev_ref[...])
    nxt = jnp.where(pos + lt == seq, 0.0, next_ref[...])
    ext = jnp.concatenate([prev, x_ref[...], nxt], axis=0)
    n_ext = lt + 2 * halo
    acc = b_ref[...] + jnp.zeros((lt, x_ref.shape[1]), F32)
    for kk in range(SSD_CONV):
        shift = (SSD_CONV // 2 - kk) % n_ext
        tap = ext if shift == 0 else pltpu.roll(ext, shift, 0)
        acc = acc + w_ref[kk:kk + 1, :] * tap[halo:halo + lt, :]
    xc_ref[...] = _silu(acc)


def _conv(xbc, conv_w, conv_b, n_p, seq_p, seq_s):
    total, c = xbc.shape
    lt = math.gcd(seq_p, 256)
    halo = 8
    nblk = total // halo
    per = lt // halo
    return pl.pallas_call(
        functools.partial(_conv_kernel, lt=lt, n_p=n_p, seq_p=seq_p, seq_s=seq_s),
        grid=(total // lt,),
        in_specs=[pl.BlockSpec((lt, c), lambda i: (i, 0)),
                  pl.BlockSpec((halo, c), lambda i: (jnp.maximum(i * per - 1, 0), 0)),
                  pl.BlockSpec((halo, c), lambda i: (jnp.minimum((i + 1) * per, nblk - 1), 0)),
                  _resident(conv_w.shape), _resident((1, c))],
        out_specs=pl.BlockSpec((lt, c), lambda i: (i, 0)),
        out_shape=jax.ShapeDtypeStruct((total, c), F32),
        compiler_params=_params("parallel"),
        name="ssd_conv",
    )(xbc, xbc, xbc, conv_w, conv_b.reshape(1, c))


def _ssd_kernel(xs_ref, b_ref, c_ref, dt_ref, dtt_ref, alr_ref, alc_ref, dbr_ref, dbc_ref, dsk_ref, h0_ref,
                y_ref, fin_ref, st_ref, *, n_chunks, n_p, seq_p, seq_s):
    q = SSD_Q
    d = pl.program_id(0)
    ci = pl.program_id(1)
    blocks = D_INNER // LANES
    tok = jnp.where(d == 0, ci, n_chunks - 1 - ci) * q
    in_p = tok < n_p
    pos = jnp.where(in_p, tok % seq_p, (tok - n_p) % seq_s)
    first = pos == 0
    last = pos + q == jnp.where(in_p, seq_p, seq_s)
    start = jnp.where(d == 0, first, last)
    end = jnp.where(d == 0, last, first)

    @pl.when(jnp.logical_and(start, in_p))
    def _():
        st_ref[...] = jnp.zeros(st_ref.shape, F32)

    @pl.when(jnp.logical_and(start, jnp.logical_not(in_p)))
    def _():
        h0 = h0_ref[...].reshape(D_INNER, SSD_N)
        for kb in range(blocks):
            st_ref[:, LANES * kb:LANES * (kb + 1)] = h0[LANES * kb:LANES * (kb + 1), :].T

    skip_on = jnp.where(d == 0, 1.0, 0.0)
    sgn = 1 - 2 * d
    li = lax.broadcasted_iota(jnp.int32, (q, q), 0)
    si = lax.broadcasted_iota(jnp.int32, (q, q), 1)
    causal = (li - si) * sgn >= 0
    tri_l = jnp.where(causal, 1.0, 0.0)
    tri_r = jnp.where((si - li) * sgn >= 0, 1.0, 0.0)
    a_row = -jnp.exp(alr_ref[...])
    a_col = -jnp.exp(alc_ref[...])
    dt_both = dt_ref[...]
    dt_col = _softplus(jnp.where(d == 0, dt_both[:, 0:SSD_H], dt_both[:, SSD_H:2 * SSD_H]) + dbr_ref[...])
    dt_row = _softplus(dtt_ref[...] + dbc_ref[...])
    cs_col = _dot_hi(tri_l, dt_col * a_row)
    da_row = dt_row * a_col
    cs_row = _dot_hi(da_row, tri_r)
    tot = jnp.sum(da_row, axis=-1, keepdims=True)
    w_row = dt_row * jnp.exp(tot - cs_row)
    etot = jnp.exp(tot)
    src_row = cs_row - jnp.log(dt_row)
    w_row_bf = w_row.astype(BF)
    first_head = lax.broadcasted_iota(jnp.int32, (q, LANES), 1) < SSD_P
    first_head2 = lax.broadcasted_iota(jnp.int32, (q + SSD_N, LANES), 1) < SSD_P
    heads_per_group = SSD_H // SSD_G
    gw = heads_per_group * SSD_P
    for g in range(SSD_G):
        bg = b_ref[:, SSD_N * g:SSD_N * (g + 1)]
        cg = c_ref[:, SSD_N * g:SSD_N * (g + 1)].astype(BF)
        cb = _dot_nt(cg, bg.astype(BF)).astype(BF)
        bgt = bg.T.astype(BF)
        c_state = _dot(cg, st_ref[:, gw * g:gw * (g + 1)].astype(BF))
        for jp in range(heads_per_group // 2):
            pi = g * (heads_per_group // 2) + jp
            cols = slice(LANES * pi, LANES * (pi + 1))
            xp = xs_ref[:, cols]
            xb = xp.astype(BF)
            res, grow = [], []
            for hh in range(2):
                h = 2 * pi + hh
                csc = jnp.broadcast_to(cs_col[:, h:h + 1], (q, q))
                grow.append(jnp.exp(csc))
                m = jnp.exp(jnp.where(causal, csc - src_row[h:h + 1, :], -jnp.inf)).astype(BF) * cb
                bw = bgt * w_row_bf[h:h + 1, :]
                res.append(_dot(jnp.concatenate([m, bw], axis=0), xb))
            both = jnp.where(first_head2, res[0], res[1])
            keep = jnp.where(first_head, etot[2 * pi:2 * pi + 1, :], etot[2 * pi + 1:2 * pi + 2, :])
            st_ref[:, cols] = keep * st_ref[:, cols] + both[q:q + SSD_N, :]
            y_off = jnp.where(first_head, grow[0], grow[1]) * c_state[:, LANES * jp:LANES * (jp + 1)]
            y_ref[:, cols] = (both[0:q, :] + y_off + (skip_on * dsk_ref[:, cols]) * xp).astype(y_ref.dtype)

    @pl.when(jnp.logical_and(end, in_p))
    def _():
        for kb in range(blocks):
            t = st_ref[:, LANES * kb:LANES * (kb + 1)].T
            fin_ref[2 * kb:2 * kb + 2] = t.reshape(2, SSD_P, SSD_N)


def _ssd(xc, dt_raw, dtt_raw, dt_bias, a_log, d_skip, h0, n_p, seq_p, seq_s):
    q = SSD_Q
    total = xc.shape[0]
    n_chunks = total // q
    n_prompt = n_p // seq_p
    n_sample = h0.shape[0]

    def blk(d, c):
        return jnp.where(d == 0, c, n_chunks - 1 - c)

    def h0_idx(d, c):
        return (jnp.clip((blk(d, c) * q - n_p) // seq_s, 0, n_sample - 1), d, 0, 0, 0)

    def fin_idx(d, c):
        return (jnp.minimum(blk(d, c) * q // seq_p, n_prompt - 1), d, 0, 0, 0)

    state_block = (None, None, SSD_H, SSD_P, SSD_N)
    return pl.pallas_call(
        functools.partial(_ssd_kernel, n_chunks=n_chunks, n_p=n_p, seq_p=seq_p, seq_s=seq_s),
        grid=(2, n_chunks),
        in_specs=[pl.BlockSpec((q, D_INNER), lambda d, c: (blk(d, c), 0)),
                  pl.BlockSpec((q, SSD_GN), lambda d, c: (blk(d, c), D_INNER // SSD_GN)),
                  pl.BlockSpec((q, SSD_GN), lambda d, c: (blk(d, c), D_INNER // SSD_GN + 1)),
                  pl.BlockSpec((q, 2 * SSD_H), lambda d, c: (blk(d, c), 0)),
                  pl.BlockSpec((SSD_H, q), lambda d, c: (d, blk(d, c))),
                  pl.BlockSpec((None, 1, SSD_H), lambda d, c: (d, 0, 0)),
                  pl.BlockSpec((None, SSD_H, 1), lambda d, c: (d, 0, 0)),
                  pl.BlockSpec((None, 1, SSD_H), lambda d, c: (d, 0, 0)),
                  pl.BlockSpec((None, SSD_H, 1), lambda d, c: (d, 0, 0)),
                  _resident((1, D_INNER)),
                  pl.BlockSpec(state_block, h0_idx)],
        out_specs=[pl.BlockSpec((None, q, D_INNER), lambda d, c: (d, blk(d, c), 0)),
                   pl.BlockSpec(state_block, fin_idx)],
        out_shape=[jax.ShapeDtypeStruct((2, total, D_INNER), BF),
                   jax.ShapeDtypeStruct((n_prompt, 2, SSD_H, SSD_P, SSD_N), F32)],
        scratch_shapes=[pltpu.VMEM((SSD_N, D_INNER), F32)],
        compiler_params=_params("parallel", "arbitrary"),
        name="ssd_scan",
    )(xc, xc, xc, dt_raw, dtt_raw, a_log.reshape(2, 1, SSD_H), a_log.reshape(2, SSD_H, 1),
      dt_bias.reshape(2, 1, SSD_H), dt_bias.reshape(2, SSD_H, 1), d_skip, h0)


def _odd_out_kernel(x_ref, mod_ref, yf_ref, yb_ref, z_ref, ng_ref, wout_ref,
                    gf_ref, wg_ref, wu_ref, wd_ref, fg_ref, *o_refs, p_steps, final):
    y = (yf_ref[...].astype(F32) + yb_ref[...].astype(F32)) * _silu(z_ref[...])
    inv = lax.rsqrt(jnp.mean(y * y, axis=-1, keepdims=True) + EPS)
    x1 = x_ref[...] + (mod_ref[2:3, :] * inv) * _dot((y * ng_ref[...]).astype(BF), wout_ref[...])
    x2 = _ffn(x1, mod_ref, gf_ref, wg_ref, wu_ref, wd_ref)
    if not final:
        o_refs[0][...] = x2
        return
    out = _rms(x2, fg_ref[...])
    is_p = pl.program_id(0) < p_steps

    @pl.when(is_p)
    def _():
        o_refs[0][...] = out

    @pl.when(jnp.logical_not(is_p))
    def _():
        o_refs[1][...] = out


def _odd_out(st, x, mods, y, z, norm_g, w_out, g_ffn, wg, wu, wd, layer, final_g, final):
    ydir = lambda d: pl.BlockSpec((None, st.tm, D_INNER), lambda i: (d, i, 0))
    return pl.pallas_call(
        functools.partial(_odd_out_kernel, p_steps=st.p_steps, final=final),
        grid=(st.steps,),
        in_specs=[st.row_spec(D_MODEL), st.mod_spec(), ydir(0), ydir(1), st.row_spec(D_INNER),
                  _resident((1, D_INNER)), _resident(w_out.shape),
                  _resident((1, D_MODEL)), _layer_resident(wg.shape, layer), _layer_resident(wu.shape, layer),
                  _layer_resident(wd.shape, layer), _resident((1, D_MODEL))],
        out_specs=[st.prompt_spec(D_MODEL), st.sample_spec(D_MODEL)] if final else st.row_spec(D_MODEL),
        out_shape=([jax.ShapeDtypeStruct((st.n_p, D_MODEL), F32),
                    jax.ShapeDtypeStruct((st.total - st.n_p, D_MODEL), F32)] if final
                   else jax.ShapeDtypeStruct((st.total, D_MODEL), F32)),
        compiler_params=_params("arbitrary"),
        name="odd_out_ffn",
    )(x, mods, y, y, z, norm_g, w_out, g_ffn, wg, wu, wd, final_g)


def kernel(x_prompt, x_sample, cache_na_k, cache_na_v, state_s5_re, state_s5_im, state_ssd, c, c_ctx, norm_mix_g, norm_ffn_g, ada_w, ada_b, ffn_w_gate, ffn_w_up, ffn_w_down, ev_w_in, ev_w_out, s5_a_re, s5_a_im, s5_log_dt, s5_b_re, s5_b_im, s5_c_re, s5_c_im, s5_d, s5_w_glu, na_rpb, od_w_in, od_conv_w, od_conv_b, ssd_a_log, ssd_dt_bias, ssd_d, ssd_norm_g, od_w_out, final_norm_g):
    bp, seq_p, d = x_prompt.shape
    bs, seq_s, _ = x_sample.shape
    depth = ada_w.shape[0]
    n_p = bp * seq_p
    assert d == D_MODEL and n_p % seq_s == 0
    st = _Stream(n_p, seq_s, bs, tm=math.gcd(512, math.gcd(n_p, seq_s)))

    xp = x_prompt.reshape(n_p, d)
    xs = x_sample.reshape(bs * seq_s, d)
    cond = jnp.concatenate([c_ctx[None, :], c, jnp.zeros((8 - 1 - bs, d), F32)], axis=0)
    mods = _ada(cond, ada_w, ada_b).reshape(depth, 8, ADA_CHUNKS, d)
    row = lambda t: t.reshape(1, -1)

    new_k, new_v, new_s5_re, new_s5_im, new_ssd = [], [], [], [], []
    wg, wu, wd = ffn_w_gate.astype(BF), ffn_w_up.astype(BF), ffn_w_down.astype(BF)
    for layer in range(depth):
        g_mix = row(norm_mix_g[layer])
        g_ffn = row(norm_ffn_g[layer])
        if layer % 2 == 0:
            e = layer // 2
            if layer > 0:
                xp, xs = x[:n_p], x[n_p:]
            u, q, k, v = _even_in(st, xp, xs, mods[layer], g_mix, ev_w_in[e].astype(BF))
            attn_p = _ctx_attn(q, k, v, bp, seq_p)
            heads = lambda t: t[:n_p].reshape(bp, seq_p, NA_HEADS, NA_HD).transpose(0, 2, 1, 3)
            new_k.append(heads(k))
            new_v.append(heads(v))
            ctx = lambda t: t[:, e].transpose(0, 2, 1, 3).reshape(bs, -1, D_NA).astype(BF)
            attn_s = _na_attn(q, k, v, ctx(cache_na_k), ctx(cache_na_v), _bias_table(na_rpb[e]),
                              bs, seq_s, n_p // seq_s)
            v_op, bw, cw, sc = _s5_prep(s5_a_re[e], s5_a_im[e], s5_log_dt[e], s5_b_re[e], s5_b_im[e],
                                        s5_c_re[e], s5_c_im[e])
            y, fin = _s5(u, v_op, bw, cw, sc, _state_planes(state_s5_re[:, e], state_s5_im[:, e]),
                         n_p, seq_p, seq_s)
            fre, fim = _planes_state(fin)
            new_s5_re.append(fre)
            new_s5_im.append(fim)
            x = _even_out(st, xp, xs, mods[layer], y, u, attn_p, attn_s, row(s5_d[e]),
                          s5_w_glu[e].astype(BF), ev_w_out[e].astype(BF), g_ffn, wg, wu, wd, layer)
        else:
            o = layer // 2
            z, xbc, dt_raw, dtt_raw = _odd_in(st, x, mods[layer], g_mix, od_w_in[o].astype(BF))
            xc = _conv(xbc, od_conv_w[o], od_conv_b[o], n_p, seq_p, seq_s)
            y, fin = _ssd(xc, dt_raw, dtt_raw, ssd_dt_bias[o], ssd_a_log[o], row(jnp.repeat(ssd_d[o], SSD_P)),
                          state_ssd[:, o], n_p, seq_p, seq_s)
            new_ssd.append(fin)
            x = _odd_out(st, x, mods[layer], y, z, row(ssd_norm_g[o]), od_w_out[o].astype(BF),
                         g_ffn, wg, wu, wd, layer, row(final_norm_g), layer == depth - 1)
    if depth % 2 == 1:
        raise NotImplementedError("final norm is fused into the last (odd) layer")
    y_prompt = x[0].reshape(bp, seq_p, d)
    y_sample = x[1].reshape(bs, seq_s, d)
    return (y_prompt, y_sample, jnp.stack(new_k, axis=1), jnp.stack(new_v, axis=1),
            jnp.stack(new_s5_re, axis=1), jnp.stack(new_s5_im, axis=1), jnp.stack(new_ssd, axis=1))
```

```python
import functools
import math

import jax
import jax.numpy as jnp
from jax import lax
from jax.experimental import pallas as pl
from jax.experimental.pallas import tpu as pltpu

F32 = jnp.float32
BF = jnp.bfloat16
HI = lax.Precision.HIGHEST

D_MODEL = 1024
EPS = 1e-6
ADA_CHUNKS = 6
GRID_W = 64
D_S5 = 512
S5_P = 16
S5_G = D_S5 // S5_P
S5_N = 64
S5_MIN_DECAY = 1e-4
S5_Q = 16
D_NA = 512
NA_HD = 64
NA_HEADS = D_NA // NA_HD
NA_WIN_ROWS = 8
NA_WIN_COLS = 16
NA_ROWS = 4
D_INNER = 2048
SSD_P = 64
SSD_H = D_INNER // SSD_P
SSD_G = 4
SSD_N = 128
SSD_Q = 128
SSD_STEP_CHUNKS = 2
SSD_CONV = 5
SSD_GN = SSD_G * SSD_N
SSD_CONV_DIM = D_INNER + 2 * SSD_GN
D_FF = 2816

V7X_VMEM_BYTES = 64 * 1024 * 1024
VMEM_LIMIT = V7X_VMEM_BYTES - 4 * 1024 * 1024
LANES = 128
NEG_BIG = -1e30


def _params(*sem):
    return pltpu.CompilerParams(dimension_semantics=sem, vmem_limit_bytes=VMEM_LIMIT)


def _resident(shape):
    nd = len(shape)
    return pl.BlockSpec(shape, lambda *_: (0,) * nd, pipeline_mode=pl.Buffered(1))


def _layer_resident(shape, layer):
    nd = len(shape)
    return pl.BlockSpec((None,) + tuple(shape[1:]), lambda *_: (layer,) + (0,) * (nd - 1), pipeline_mode=pl.Buffered(1))


def _dot(a, b):
    return jnp.dot(a, b, preferred_element_type=F32)


def _dot_nt(a, b):
    return lax.dot_general(a, b, (((1,), (1,)), ((), ())), preferred_element_type=F32)


def _dot_hi(a, b):
    return jnp.dot(a, b, preferred_element_type=F32, precision=HI)


def _sigmoid(x):
    return 1.0 / (1.0 + jnp.exp(-x))


def _silu(x):
    return x * _sigmoid(x)


def _softplus(x):
    return jnp.maximum(x, 0.0) + jnp.log1p(jnp.exp(-jnp.abs(x)))


def _rms(x, g):
    return x * lax.rsqrt(jnp.mean(x * x, axis=-1, keepdims=True) + EPS) * g


def _norm_mod(x, g, shift, scale):
    return _rms(x, g) * (1.0 + scale) + shift


def _ada_kernel(c_ref, w_ref, b_ref, o_ref):
    c = c_ref[...]
    o_ref[...] = _dot(_silu(c).astype(BF), w_ref[...].astype(BF)) + b_ref[...]


def _ada(cond, ada_w, ada_b):
    depth, d, n = ada_w.shape
    tn = 1536
    return pl.pallas_call(
        _ada_kernel,
        grid=(depth, n // tn),
        in_specs=[pl.BlockSpec((8, d), lambda l, j: (0, 0)),
                  pl.BlockSpec((None, d, tn), lambda l, j: (l, 0, j)),
                  pl.BlockSpec((None, 1, tn), lambda l, j: (l, 0, j))],
        out_specs=pl.BlockSpec((None, 8, tn), lambda l, j: (l, 0, j)),
        out_shape=jax.ShapeDtypeStruct((depth, 8, n), F32),
        compiler_params=_params("parallel", "parallel"),
        name="adaln",
    )(cond, ada_w, ada_b.reshape(depth, 1, n))


class _Stream:
    def __init__(self, n_p, len_s, n_s, tm):
        assert n_p % tm == 0 and len_s % tm == 0
        self.n_p, self.len_s, self.n_s, self.tm = n_p, len_s, n_s, tm
        self.total = n_p + len_s * n_s
        self.steps = self.total // tm
        self.p_steps = n_p // tm

    def group(self, i):
        t = i * self.tm
        return jnp.where(t < self.n_p, 0, 1 + (t - self.n_p) // self.len_s)

    def mod_spec(self):
        return pl.BlockSpec((None, ADA_CHUNKS, D_MODEL), lambda i: (self.group(i), 0, 0))

    def row_spec(self, width, col=0):
        return pl.BlockSpec((self.tm, width), lambda i: (i, col))

    def prompt_spec(self, width):
        return pl.BlockSpec((self.tm, width), lambda i: (jnp.minimum(i, self.p_steps - 1), 0))

    def sample_spec(self, width):
        return pl.BlockSpec((self.tm, width), lambda i: (jnp.maximum(i - self.p_steps, 0), 0))


def _ffn(x1, mod_ref, g_ref, wg_ref, wu_ref, wd_ref):
    h = _norm_mod(x1, g_ref[...], mod_ref[3:4, :], mod_ref[4:5, :]).astype(BF)
    hid = (_silu(_dot(h, wg_ref[...])) * _dot(h, wu_ref[...])).astype(BF)
    return x1 + mod_ref[5:6, :] * _dot(hid, wd_ref[...])


def _even_in_kernel(xp_ref, xs_ref, mod_ref, g_ref, w_ref, u_ref, q_ref, k_ref, v_ref, *, p_steps):
    x = jnp.where(pl.program_id(0) < p_steps, xp_ref[...], xs_ref[...])
    h = _norm_mod(x, g_ref[...], mod_ref[0:1, :], mod_ref[1:2, :])
    r = _dot(h.astype(BF), w_ref[...])
    u_ref[...] = r[:, 0:D_S5]
    q_ref[...] = r[:, D_S5:D_S5 + D_NA] * (NA_HD ** -0.5)
    k_ref[...] = r[:, D_S5 + D_NA:D_S5 + 2 * D_NA]
    v_ref[...] = r[:, D_S5 + 2 * D_NA:D_S5 + 3 * D_NA]


def _even_in(st, xp, xs, mods, g, w_in):
    n_out = w_in.shape[1]
    out = jax.ShapeDtypeStruct((st.total, D_S5), F32)
    return pl.pallas_call(
        functools.partial(_even_in_kernel, p_steps=st.p_steps),
        grid=(st.steps,),
        in_specs=[st.prompt_spec(D_MODEL), st.sample_spec(D_MODEL), st.mod_spec(), _resident((1, D_MODEL)),
                  _resident((D_MODEL, n_out))],
        out_specs=[st.row_spec(D_S5)] * 4,
        out_shape=[out] * 4,
        compiler_params=_params("parallel"),
        name="even_in",
    )(xp, xs, mods, g, w_in)


def _ctx_attn_kernel(q_ref, k_ref, v_ref, o_ref):
    lane = lax.broadcasted_iota(jnp.int32, (q_ref.shape[0], LANES), 1)
    for pair in range(D_NA // LANES):
        cols = slice(LANES * pair, LANES * (pair + 1))
        q = q_ref[:, cols]
        k = k_ref[:, cols].astype(BF)
        v = v_ref[:, cols].astype(BF)
        outs = []
        for h in range(2):
            in_head = (lane >= NA_HD * h) & (lane < NA_HD * (h + 1))
            qh = jnp.where(in_head, q, 0.0).astype(BF)
            s = _dot_nt(qh, k)
            p = jnp.exp(s - jnp.max(s, axis=-1, keepdims=True))
            l = jnp.sum(p, axis=-1, keepdims=True)
            outs.append(_dot(p.astype(BF), v) / l)
        o_ref[:, cols] = jnp.where(lane < NA_HD, outs[0], outs[1])


def _ctx_attn(q, k, v, n_seq, seq):
    spec = pl.BlockSpec((seq, D_NA), lambda b: (b, 0))
    return pl.pallas_call(
        _ctx_attn_kernel,
        grid=(n_seq,),
        in_specs=[spec, spec, spec],
        out_specs=spec,
        out_shape=jax.ShapeDtypeStruct((n_seq * seq, D_NA), F32),
        compiler_params=_params("parallel"),
        name="ctx_attn",
    )(q, k, v)


NA_N_DR = 2 * NA_WIN_ROWS - 1
NA_BOTH, NA_LEFT, NA_RIGHT = 0, NA_N_DR - 1, 2 * NA_N_DR - 1
NA_NONE = 3 * NA_N_DR - 1
NA_UNION = NA_WIN_ROWS + NA_ROWS


def _bias_kernel(rpb_ref, o_ref):
    h = pl.program_id(0)
    n_dc = 2 * NA_WIN_COLS - 1
    wq = lax.broadcasted_iota(jnp.int32, (GRID_W, LANES), 0)
    lane = lax.broadcasted_iota(jnp.int32, (GRID_W, LANES), 1)
    wk = lane & (GRID_W - 1)
    left = lane < GRID_W
    col_start = jnp.clip(wq - NA_WIN_COLS // 2, 0, GRID_W - NA_WIN_COLS)
    ok = (wk >= col_start) & (wk < col_start + NA_WIN_COLS)
    dc = jnp.clip(wk - wq, -(NA_WIN_COLS - 1), NA_WIN_COLS - 1) + (NA_WIN_COLS - 1)

    def entry(e, carry):
        is_both = e < NA_LEFT
        is_left = jnp.logical_and(e >= NA_LEFT, e < NA_RIGHT)
        is_right = jnp.logical_and(e >= NA_RIGHT, e < NA_NONE)
        d_left = jnp.where(is_both, e, jnp.where(is_left, e - NA_LEFT, 0))
        d_right = jnp.where(is_both, e + 1, jnp.where(is_right, e - NA_RIGHT, 0))
        left_on = jnp.where(jnp.logical_or(is_both, is_left), 1, 0)
        right_on = jnp.where(jnp.logical_or(is_both, is_right), 1, 0)
        acc = jnp.zeros((GRID_W, LANES), F32)
        for kk in range(n_dc):
            lo = rpb_ref[(h * NA_N_DR + d_left) * n_dc + kk]
            hi = rpb_ref[(h * NA_N_DR + d_right) * n_dc + kk]
            acc = jnp.where(dc == kk, jnp.where(left, lo, hi), acc)
        side_on = jnp.where(left, left_on, right_on) > 0
        o_ref[e] = jnp.where(ok & side_on, acc, NEG_BIG)
        return carry

    lax.fori_loop(0, NA_NONE + 1, entry, 0)


def _bias_table(rpb):
    return pl.pallas_call(
        _bias_kernel,
        grid=(NA_HEADS,),
        in_specs=[pl.BlockSpec(memory_space=pltpu.SMEM)],
        out_specs=pl.BlockSpec((None, NA_NONE + 1, GRID_W, LANES), lambda h: (h, 0, 0, 0)),
        out_shape=jax.ShapeDtypeStruct((NA_HEADS, NA_NONE + 1, GRID_W, LANES), F32),
        compiler_params=_params("parallel"),
        name="na_bias",
    )(rpb.reshape(-1))


def _na_kernel(q_ref, k_ref, v_ref, ck_ref, cv_ref, bias_ref, o_ref, kb_ref, vb_ref, *, rows):
    kb_ref[...] = k_ref[...].astype(BF)
    vb_ref[...] = v_ref[...].astype(BF)
    ck = ck_ref[...]
    cv = cv_ref[...]
    kh = NA_WIN_ROWS
    nq = NA_ROWS * GRID_W
    lane = lax.broadcasted_iota(jnp.int32, (nq, LANES), 1)

    def body(g, carry):
        r0 = g * NA_ROWS
        first = jnp.clip(r0 - kh // 2, 0, rows - NA_UNION)
        qoff = pl.multiple_of(r0 * GRID_W, nq)
        q = q_ref[pl.ds(qoff, nq), :]
        koff = pl.multiple_of(first * GRID_W, GRID_W)
        kw = kb_ref[pl.ds(koff, NA_UNION * GRID_W), :]
        vw = vb_ref[pl.ds(koff, NA_UNION * GRID_W), :]
        tile_idx = []
        for j in range(NA_ROWS):
            r = r0 + j
            start = jnp.clip(r - kh // 2, 0, rows - kh)
            for ip in range(NA_UNION // 2):
                k0 = first + 2 * ip
                in0 = jnp.logical_and(k0 >= start, k0 < start + kh)
                in1 = jnp.logical_and(k0 + 1 >= start, k0 + 1 < start + kh)
                dr0 = k0 - r + (NA_WIN_ROWS - 1)
                idx = jnp.where(jnp.logical_and(in0, in1), NA_BOTH + dr0,
                                jnp.where(in0, NA_LEFT + dr0, jnp.where(in1, NA_RIGHT + dr0 + 1, NA_NONE)))
                tile_idx.append(jnp.clip(idx, 0, NA_NONE))
        outs = []
        for h in range(2):
            in_head = (lane >= NA_HD * h) & (lane < NA_HD * (h + 1))
            qh = jnp.where(in_head, q, 0.0).astype(BF)
            per_row = NA_UNION // 2
            bias = jnp.concatenate(
                [jnp.concatenate([bias_ref[h, tile_idx[j * per_row + ip]] for ip in range(per_row)], axis=1)
                 for j in range(NA_ROWS)], axis=0)
            s_loc = _dot_nt(qh, kw) + bias
            s_ctx = _dot_nt(qh, ck)
            m = jnp.maximum(jnp.max(s_loc, axis=-1, keepdims=True), jnp.max(s_ctx, axis=-1, keepdims=True))
            p_loc = jnp.exp(s_loc - m)
            p_ctx = jnp.exp(s_ctx - m)
            l = jnp.sum(p_loc, axis=-1, keepdims=True) + jnp.sum(p_ctx, axis=-1, keepdims=True)
            outs.append((_dot(p_loc.astype(BF), vw) + _dot(p_ctx.astype(BF), cv)) / l)
        o_ref[pl.ds(qoff, nq), :] = jnp.where(lane < NA_HD, outs[0], outs[1])
        return carry

    lax.fori_loop(0, rows // NA_ROWS, body, 0, unroll=2)


def _na_attn(q, k, v, ck, cv, bias, n_seq, seq, row_base):
    rows = seq // GRID_W
    assert rows >= NA_UNION and rows % NA_ROWS == 0
    past = ck.shape[1]
    spec = pl.BlockSpec((seq, LANES), lambda b, p: (row_base + b, p))
    cspec = pl.BlockSpec((None, past, LANES), lambda b, p: (b, 0, p))
    return pl.pallas_call(
        functools.partial(_na_kernel, rows=rows),
        grid=(n_seq, D_NA // LANES),
        in_specs=[spec, spec, spec, cspec, cspec,
                  pl.BlockSpec((2, NA_NONE + 1, GRID_W, LANES), lambda b, p: (p, 0, 0, 0))],
        out_specs=pl.BlockSpec((seq, LANES), lambda b, p: (b, p)),
        out_shape=jax.ShapeDtypeStruct((n_seq * seq, D_NA), F32),
        scratch_shapes=[pltpu.VMEM((seq, LANES), BF), pltpu.VMEM((seq, LANES), BF)],
        compiler_params=_params("parallel", "parallel"),
        name="na_attn",
    )(q, k, v, ck, cv, bias)


S5_BLK = LANES // S5_P
S5_NB = S5_G // S5_BLK
S5_ST = S5_BLK * S5_N
S5_W = S5_Q * LANES


def _cexp(zr, zi):
    e = jnp.exp(zr)
    return e * jnp.cos(zi), e * jnp.sin(zi)


def _cmul(ar, ai, br, bi):
    return ar * br - ai * bi, ar * bi + ai * br


def _s5_prep_kernel(arr_ref, ari_ref, ldr_ref, acr_ref, aci_ref, ldc_ref, btr_ref, bti_ref, ctr_ref, cti_ref,
                    v_ref, bw_ref, cw_ref, sc_ref,
                    bbr_s, bbi_s, ccr_s, cci_s, pwr_s, pwi_s, pcr_s, pci_s, p0_s, p1_s):
    t = pl.program_id(1)
    q = S5_Q
    nst = S5_ST

    @pl.when(t == 0)
    def _():
        expand = jnp.where(lax.broadcasted_iota(jnp.int32, (S5_P, 2 * LANES), 0)
                           == (lax.broadcasted_iota(jnp.int32, (S5_P, 2 * LANES), 1) & (S5_P - 1)), 1.0, 0.0)
        row_g = lax.broadcasted_iota(jnp.int32, (LANES, nst), 0) >> 4
        col_g = lax.broadcasted_iota(jnp.int32, (LANES, nst), 1) >> 6
        st_g = lax.broadcasted_iota(jnp.int32, (nst, LANES), 0) >> 6
        ch_g = lax.broadcasted_iota(jnp.int32, (nst, LANES), 1) >> 4
        lane_d = lax.broadcasted_iota(jnp.int32, (nst, LANES), 1) >> 4
        taps = []
        for d in range(2):
            ar = jnp.minimum(arr_ref[d], -S5_MIN_DECAY)
            ai = ari_ref[d]
            dt = jnp.exp(ldr_ref[d])
            abr, abi = _cexp(ar * dt, ai * dt)
            den = ar * ar + ai * ai
            cfr = ((abr - 1.0) * ar + abi * ai) / den
            cfi = (abi * ar - (abr - 1.0) * ai) / den
            bbr, bbi = _cmul(cfr, cfi, btr_ref[d], bti_ref[d])
            bbr = jnp.where(row_g == col_g, jnp.concatenate([bbr] * S5_BLK, axis=0), 0.0)
            bbi = jnp.where(row_g == col_g, jnp.concatenate([bbi] * S5_BLK, axis=0), 0.0)
            bbr_s[d] = bbr
            bbi_s[d] = bbi
            pr = jnp.ones((1, nst), F32)
            pi = jnp.zeros((1, nst), F32)
            for e in range(q + 1):
                pwr_s[d, e] = pr
                pwi_s[d, e] = pi
                pr, pi = _cmul(pr, pi, abr, abi)
            sc_ref[2 * d:2 * d + 1, :] = pwr_s[d, q]
            sc_ref[2 * d + 1:2 * d + 2, :] = pwi_s[d, q]
            arc = jnp.minimum(acr_ref[d], -S5_MIN_DECAY)
            dtc = jnp.exp(ldc_ref[d])
            acr, aci = _cexp(jnp.broadcast_to(arc * dtc, (nst, LANES)), jnp.broadcast_to(aci_ref[d] * dtc, (nst, LANES)))
            pr = jnp.ones((nst, LANES), F32)
            pi = jnp.zeros((nst, LANES), F32)
            for e in range(q + 1):
                pcr_s[d, e] = pr
                pci_s[d, e] = pi
                pr, pi = _cmul(pr, pi, acr, aci)
            cxr = _dot_hi(ctr_ref[d], expand)
            cxi = _dot_hi(cti_ref[d], expand)
            ccr_s[d] = jnp.where(st_g == ch_g, cxr[:, 0:LANES], 0.0)
            cci_s[d] = jnp.where(st_g == ch_g, cxi[:, 0:LANES], 0.0)
            pws_r, pws_i = [], []
            for k in range(2):
                sel_r = jnp.zeros((nst, LANES), F32)
                sel_i = jnp.zeros((nst, LANES), F32)
                for j in range(S5_BLK):
                    dl = S5_BLK * k + j
                    e = dl if d == 0 else q - 1 - dl
                    sel_r = jnp.where(lane_d == j, pcr_s[d, e], sel_r)
                    sel_i = jnp.where(lane_d == j, pci_s[d, e], sel_i)
                pws_r.append(sel_r)
                pws_i.append(sel_i)
            ggr, ggi = _cmul(cxr, cxi, jnp.concatenate(pws_r, axis=1), jnp.concatenate(pws_i, axis=1))
            taps.append(_dot_hi(bbr, ggr) - _dot_hi(bbi, ggi))
        lane2 = lax.broadcasted_iota(jnp.int32, (LANES, 2 * LANES), 1)
        kt0 = taps[0] + jnp.where(lane2 < S5_P, pltpu.roll(taps[1], S5_P, 1), 0.0)
        kt1 = jnp.where(lane2 >= 2 * LANES - S5_P, 0.0, taps[1])
        er = lax.broadcasted_iota(jnp.int32, (2 * LANES, S5_W), 0)
        ec = lax.broadcasted_iota(jnp.int32, (2 * LANES, S5_W), 1)
        place = jnp.where(((er >> 4) == (ec >> 7)) & ((er & (S5_P - 1)) == (ec & (S5_P - 1))), 1.0, 0.0).astype(BF)
        own = (lax.broadcasted_iota(jnp.int32, (LANES, S5_W), 0) >> 4) == (
            (lax.broadcasted_iota(jnp.int32, (LANES, S5_W), 1) >> 4) & (S5_BLK - 1))
        zeros = jnp.zeros((LANES, S5_W), BF)
        p0_s[:, 0:S5_W] = zeros
        p0_s[:, S5_W:2 * S5_W] = jnp.where(own, _dot(kt0.astype(BF), place), 0.0).astype(BF)
        p1_s[:, 0:S5_W] = jnp.where(own, _dot(kt1.astype(BF), place), 0.0).astype(BF)
        p1_s[:, S5_W:2 * S5_W] = zeros

    off0 = pl.multiple_of(S5_W - LANES * t, LANES)
    off1 = pl.multiple_of(LANES * (q - 1 - t), LANES)
    v_ref[...] = p0_s[:, pl.ds(off0, S5_W)] + p1_s[:, pl.ds(off1, S5_W)]
    for d in range(2):
        e_b = (q - 1 - t) if d == 0 else t
        br, bi = _cmul(pwr_s[d, e_b], pwi_s[d, e_b], bbr_s[d], bbi_s[d])
        bw_ref[:, 2 * nst * d:2 * nst * d + nst] = br.astype(BF)
        bw_ref[:, 2 * nst * d + nst:2 * nst * (d + 1)] = bi.astype(BF)
        e_c = (t + 1) if d == 0 else (q - t)
        gr, gi = _cmul(ccr_s[d], cci_s[d], pcr_s[d, e_c], pci_s[d, e_c])
        cw_ref[2 * nst * d:2 * nst * d + nst, :] = gr.astype(BF)
        cw_ref[2 * nst * d + nst:2 * nst * (d + 1), :] = (-gi).astype(BF)


def _s5_prep(a_re, a_im, log_dt, b_re, b_im, c_re, c_im):
    nst, nb, q = S5_ST, S5_NB, S5_Q
    row = lambda t: t.reshape(2, nb, 1, nst)
    col = lambda t: t.reshape(2, nb, nst, 1)
    ld = jnp.broadcast_to(log_dt[:, :, None], (2, S5_G, S5_N))
    bt = lambda t: t.reshape(2, nb, S5_BLK, S5_N, S5_P).transpose(0, 1, 4, 2, 3).reshape(2, nb, S5_P, nst)
    ct = lambda t: t.reshape(2, nb, S5_BLK, S5_P, S5_N).transpose(0, 1, 2, 4, 3).reshape(2, nb, nst, S5_P)
    rspec = pl.BlockSpec((2, None, 1, nst), lambda b, t: (0, b, 0, 0))
    cspec = pl.BlockSpec((2, None, nst, 1), lambda b, t: (0, b, 0, 0))
    btspec = pl.BlockSpec((2, None, S5_P, nst), lambda b, t: (0, b, 0, 0))
    ctspec = pl.BlockSpec((2, None, nst, S5_P), lambda b, t: (0, b, 0, 0))
    big = jax.ShapeDtypeStruct((nb, S5_W, S5_W), BF)
    return pl.pallas_call(
        _s5_prep_kernel,
        grid=(nb, q),
        in_specs=[rspec, rspec, rspec, cspec, cspec, cspec, btspec, btspec, ctspec, ctspec],
        out_specs=[pl.BlockSpec((None, LANES, S5_W), lambda b, t: (b, t, 0)),
                   pl.BlockSpec((None, LANES, S5_W), lambda b, t: (b, t, 0)),
                   pl.BlockSpec((None, S5_W, LANES), lambda b, t: (b, 0, t)),
                   pl.BlockSpec((None, 4, nst), lambda b, t: (b, 0, 0))],
        out_shape=[big, big, big, jax.ShapeDtypeStruct((nb, 4, nst), F32)],
        scratch_shapes=[pltpu.VMEM((2, LANES, nst), F32), pltpu.VMEM((2, LANES, nst), F32),
                        pltpu.VMEM((2, nst, LANES), F32), pltpu.VMEM((2, nst, LANES), F32),
                        pltpu.VMEM((2, q + 1, 1, nst), F32), pltpu.VMEM((2, q + 1, 1, nst), F32),
                        pltpu.VMEM((2, q + 1, nst, LANES), F32), pltpu.VMEM((2, q + 1, nst, LANES), F32),
                        pltpu.VMEM((LANES, 2 * S5_W), BF), pltpu.VMEM((LANES, 2 * S5_W), BF)],
        compiler_params=_params("parallel", "arbitrary"),
        name="s5_prep",
    )(row(a_re), row(a_im), row(ld), col(a_re), col(a_im), col(ld), bt(b_re), bt(b_im), ct(c_re), ct(c_im))


def _s5_kernel(u_ref, v_ref, bw_ref, cw_ref, sc_ref, h0_ref, y_ref, fin_ref, z_s, h_s, *, p_steps, n_sub, mp):
    s = pl.program_id(1)
    m = u_ref.shape[0] // S5_Q
    n_tiles = 4 * S5_ST // LANES
    per = S5_ST // LANES
    half = 2 * per
    chunk_rows = lambda t: pl.ds(t, m, stride=S5_Q)
    tile_rows = lambda k: pl.ds(k, m, stride=n_tiles)
    ucat = jnp.concatenate([u_ref[chunk_rows(t), :].astype(BF) for t in range(S5_Q)], axis=1)
    z = _dot(ucat, bw_ref[...])
    for k in range(n_tiles):
        z_s[tile_rows(k), :] = z[:, LANES * k:LANES * (k + 1)]
    sc = sc_ref[...]
    stack = lambda row: jnp.concatenate([row[:, LANES * j:LANES * (j + 1)] for j in range(per)], axis=0)

    def multipliers(d):
        ar, ai = stack(sc[2 * d:2 * d + 1]), stack(sc[2 * d + 1:2 * d + 2])
        return jnp.concatenate([ar, ar], axis=0), jnp.concatenate([-ai, ai], axis=0)

    a1f, a2f = multipliers(0)
    a1b, a2b = multipliers(1)

    def advance(hf, hb, cf, cb):
        rf = pl.ds(pl.multiple_of(cf * n_tiles, half), half)
        rb = pl.ds(pl.multiple_of(cb * n_tiles + half, half), half)
        h_s[rf, :] = hf
        h_s[rb, :] = hb
        hf = a1f * hf + a2f * pltpu.roll(hf, per, 0) + z_s[rf, :]
        hb = a1b * hb + a2b * pltpu.roll(hb, per, 0) + z_s[rb, :]
        return hf, hb

    unstack = lambda h4: jnp.concatenate([h4[j:j + 1, :] for j in range(per)], axis=1)

    @pl.when(s < p_steps)
    def _():
        for i in range(n_sub):
            hf = jnp.zeros((half, LANES), F32)
            hb = jnp.zeros((half, LANES), F32)
            for c in range(mp):
                hf, hb = advance(hf, hb, i * mp + c, i * mp + mp - 1 - c)
            for pi, h4 in enumerate((hf[0:per], hf[per:half], hb[0:per], hb[per:half])):
                fin_ref[i, pi:pi + 1, :] = unstack(h4)

    @pl.when(s >= p_steps)
    def _():
        h0 = h0_ref[...]
        init = (jnp.concatenate([stack(h0[0:1]), stack(h0[1:2])], axis=0),
                jnp.concatenate([stack(h0[2:3]), stack(h0[3:4])], axis=0))
        lax.fori_loop(0, m, lambda c, h: advance(h[0], h[1], c, m - 1 - c), init)

    hcat = jnp.concatenate([h_s[tile_rows(k), :] for k in range(n_tiles)], axis=1).astype(BF)
    ycat = _dot(ucat, v_ref[...]) + _dot(hcat, cw_ref[...])
    for t in range(S5_Q):
        y_ref[chunk_rows(t), :] = ycat[:, LANES * t:LANES * (t + 1)]


def _s5(u, v, bw, cw, sc, h0, n_p, seq_p, seq_s):
    total = u.shape[0]
    q = S5_Q
    m = seq_s // q
    mp = seq_p // q
    n_sub = seq_s // seq_p
    p_steps = n_p // seq_s
    n_prompt = n_p // seq_p
    wspec = pl.BlockSpec((None, S5_W, S5_W), lambda b, s: (b, 0, 0), pipeline_mode=pl.Buffered(1))
    tok = pl.BlockSpec((seq_s, LANES), lambda b, s: (s, b))
    y, fin = pl.pallas_call(
        functools.partial(_s5_kernel, p_steps=p_steps, n_sub=n_sub, mp=mp),
        grid=(S5_NB, total // seq_s),
        in_specs=[tok, wspec, wspec, wspec,
                  pl.BlockSpec((None, 4, S5_ST), lambda b, s: (b, 0, 0)),
                  pl.BlockSpec((None, 4, S5_ST), lambda b, s: (jnp.maximum(s - p_steps, 0), 0, b))],
        out_specs=[tok, pl.BlockSpec((n_sub, 4, S5_ST), lambda b, s: (jnp.minimum(s, p_steps - 1), 0, b))],
        out_shape=[jax.ShapeDtypeStruct((total, D_S5), F32),
                   jax.ShapeDtypeStruct((n_prompt, 4, S5_G * S5_N), F32)],
        scratch_shapes=[pltpu.VMEM((4 * S5_ST // LANES * m, LANES), F32)] * 2,
        compiler_params=_params("parallel", "arbitrary"),
        name="s5_scan",
    )(u, v, bw, cw, sc, h0)
    return y, fin


def _state_planes(re, im):
    b = re.shape[0]
    return jnp.stack([t[:, d].reshape(b, S5_G * S5_N) for d in range(2) for t in (re, im)], axis=1)


def _planes_state(fin):
    t = fin.reshape(fin.shape[0], 2, 2, S5_G, S5_N)
    return t[:, :, 0], t[:, :, 1]


def _gelu_tanh(x):
    return 0.5 * x * (1.0 + jnp.tanh(math.sqrt(2.0 / math.pi) * (x + 0.044715 * (x * x * x))))


def _even_out_kernel(xp_ref, xs_ref, mod_ref, y_ref, u_ref, ap_ref, as_ref, dsk_ref, wglu_ref, wout_ref,
                     gf_ref, wg_ref, wu_ref, wd_ref, o_ref, *, p_steps):
    is_p = pl.program_id(0) < p_steps
    attn = jnp.where(is_p, ap_ref[...], as_ref[...])
    y = _gelu_tanh(y_ref[...] + u_ref[...] * dsk_ref[...])
    y = y * _sigmoid(_dot(y.astype(BF), wglu_ref[...]))
    out = _dot(y.astype(BF), wout_ref[0:D_S5, :]) + _dot(attn.astype(BF), wout_ref[D_S5:D_S5 + D_NA, :])
    x1 = jnp.where(is_p, xp_ref[...], xs_ref[...]) + mod_ref[2:3, :] * out
    o_ref[...] = _ffn(x1, mod_ref, gf_ref, wg_ref, wu_ref, wd_ref)


def _even_out(st, xp, xs, mods, y, u, attn_p, attn_s, d_skip, w_glu, w_out, g_ffn, wg, wu, wd, layer):
    return pl.pallas_call(
        functools.partial(_even_out_kernel, p_steps=st.p_steps),
        grid=(st.steps,),
        in_specs=[st.prompt_spec(D_MODEL), st.sample_spec(D_MODEL), st.mod_spec(), st.row_spec(D_S5),
                  st.row_spec(D_S5), st.prompt_spec(D_NA), st.sample_spec(D_NA),
                  _resident((1, D_S5)), _resident(w_glu.shape), _resident(w_out.shape),
                  _resident((1, D_MODEL)), _layer_resident(wg.shape, layer), _layer_resident(wu.shape, layer),
                  _layer_resident(wd.shape, layer)],
        out_specs=st.row_spec(D_MODEL),
        out_shape=jax.ShapeDtypeStruct((st.total, D_MODEL), F32),
        compiler_params=_params("parallel"),
        name="even_out_ffn",
    )(xp, xs, mods, y, u, attn_p, attn_s, d_skip, w_glu, w_out, g_ffn, wg, wu, wd)


def _odd_in_kernel(x_ref, mod_ref, g_ref, w_ref, wdtt_ref, z_ref, xbc_ref, dt_ref, dtt_ref):
    h = _norm_mod(x_ref[...], g_ref[...], mod_ref[0:1, :], mod_ref[1:2, :]).astype(BF)
    z_ref[...] = _dot(h, w_ref[:, 0:D_INNER])
    xbc_ref[...] = _dot(h, w_ref[:, D_INNER:D_INNER + SSD_CONV_DIM])
    dt_ref[...] = _dot(h, w_ref[:, D_INNER + SSD_CONV_DIM:D_INNER + SSD_CONV_DIM + 2 * SSD_H])
    dtt_ref[...] = _dot_nt(wdtt_ref[...], h)


def _odd_in(st, x, mods, g, w_in):
    wdtt = w_in[:, D_INNER + SSD_CONV_DIM:].T
    return pl.pallas_call(
        _odd_in_kernel,
        grid=(st.steps,),
        in_specs=[st.row_spec(D_MODEL), st.mod_spec(), _resident((1, D_MODEL)),
                  _resident(w_in.shape), _resident(wdtt.shape)],
        out_specs=[st.row_spec(D_INNER), st.row_spec(SSD_CONV_DIM), st.row_spec(2 * SSD_H),
                   pl.BlockSpec((2 * SSD_H, st.tm), lambda i: (0, i))],
        out_shape=[jax.ShapeDtypeStruct((st.total, D_INNER), F32),
                   jax.ShapeDtypeStruct((st.total, SSD_CONV_DIM), F32),
                   jax.ShapeDtypeStruct((st.total, 2 * SSD_H), F32),
                   jax.ShapeDtypeStruct((2 * SSD_H, st.total), F32)],
        compiler_params=_params("parallel"),
        name="odd_in",
    )(x, mods, g, w_in, wdtt)


def _conv_kernel(x_ref, prev_ref, next_ref, w_ref, b_ref, xc_ref, *, lt, n_p, seq_p, seq_s):
    tok = pl.program_id(0) * lt
    in_p = tok < n_p
    pos = jnp.where(in_p, tok % seq_p, (tok - n_p) % seq_s)
    seq = jnp.where(in_p, seq_p, seq_s)
    halo = prev_ref.shape[0]
    prev = jnp.where(pos == 0, 0.0, prev_ref[...])
    nxt = jnp.where(pos + lt == seq, 0.0, next_ref[...])
    ext = jnp.concatenate([prev, x_ref[...], nxt], axis=0)
    n_ext = lt + 2 * halo
    acc = b_ref[...] + jnp.zeros((lt, x_ref.shape[1]), F32)
    for kk in range(SSD_CONV):
        shift = (SSD_CONV // 2 - kk) % n_ext
        tap = ext if shift == 0 else pltpu.roll(ext, shift, 0)
        acc = acc + w_ref[kk:kk + 1, :] * tap[halo:halo + lt, :]
    xc_ref[...] = _silu(acc)


def _conv(xbc, conv_w, conv_b, n_p, seq_p, seq_s):
    total, c = xbc.shape
    lt = math.gcd(seq_p, 256)
    halo = 8
    nblk = total // halo
    per = lt // halo
    return pl.pallas_call(
        functools.partial(_conv_kernel, lt=lt, n_p=n_p, seq_p=seq_p, seq_s=seq_s),
        grid=(total // lt,),
        in_specs=[pl.BlockSpec((lt, c), lambda i: (i, 0)),
                  pl.BlockSpec((halo, c), lambda i: (jnp.maximum(i * per - 1, 0), 0)),
                  pl.BlockSpec((halo, c), lambda i: (jnp.minimum((i + 1) * per, nblk - 1), 0)),
                  _resident(conv_w.shape), _resident((1, c))],
        out_specs=pl.BlockSpec((lt, c), lambda i: (i, 0)),
        out_shape=jax.ShapeDtypeStruct((total, c), F32),
        compiler_params=_params("parallel"),
        name="ssd_conv",
    )(xbc, xbc, xbc, conv_w, conv_b.reshape(1, c))


def _ssd_kernel(xs_ref, b_ref, c_ref, dt_ref, dtt_ref, alr_ref, alc_ref, dbr_ref, dbc_ref, dsk_ref, h0_ref,
                y_ref, fin_ref, st_ref, *, n_chunks, n_p, seq_p, seq_s):
    q = SSD_Q
    d = pl.program_id(0)
    ci = pl.program_id(1)
    blocks = D_INNER // LANES
    n_steps = n_chunks // SSD_STEP_CHUNKS
    tok = jnp.where(d == 0, ci, n_steps - 1 - ci) * (q * SSD_STEP_CHUNKS)
    in_p = tok < n_p
    pos = jnp.where(in_p, tok % seq_p, (tok - n_p) % seq_s)
    first = pos == 0
    last = pos + q * SSD_STEP_CHUNKS == jnp.where(in_p, seq_p, seq_s)
    start = jnp.where(d == 0, first, last)
    end = jnp.where(d == 0, last, first)

    @pl.when(jnp.logical_and(start, in_p))
    def _():
        st_ref[...] = jnp.zeros(st_ref.shape, F32)

    @pl.when(jnp.logical_and(start, jnp.logical_not(in_p)))
    def _():
        h0 = h0_ref[...].reshape(D_INNER, SSD_N)
        for kb in range(blocks):
            st_ref[:, LANES * kb:LANES * (kb + 1)] = h0[LANES * kb:LANES * (kb + 1), :].T

    skip_on = jnp.where(d == 0, 1.0, 0.0)
    sgn = 1 - 2 * d
    li = lax.broadcasted_iota(jnp.int32, (q, q), 0)
    si = lax.broadcasted_iota(jnp.int32, (q, q), 1)
    causal = (li - si) * sgn >= 0
    tri_l = jnp.where(causal, 1.0, 0.0)
    tri_r = jnp.where((si - li) * sgn >= 0, 1.0, 0.0)
    a_row = -jnp.exp(alr_ref[...])
    a_col = -jnp.exp(alc_ref[...])
    first_head = lax.broadcasted_iota(jnp.int32, (q, LANES), 1) < SSD_P
    first_head2 = lax.broadcasted_iota(jnp.int32, (q + SSD_N, LANES), 1) < SSD_P
    for k in range(SSD_STEP_CHUNKS):
        off = pl.multiple_of(jnp.where(d == 0, k, SSD_STEP_CHUNKS - 1 - k) * q, q)
        rows = pl.ds(off, q)
        _ssd_chunk(xs_ref, b_ref, c_ref, dt_ref, dtt_ref, dbr_ref, dbc_ref, dsk_ref, y_ref, st_ref, rows, d,
                   a_row, a_col, causal, tri_l, tri_r, skip_on, first_head, first_head2)

    @pl.when(jnp.logical_and(end, in_p))
    def _():
        for kb in range(blocks):
            t = st_ref[:, LANES * kb:LANES * (kb + 1)].T
            fin_ref[2 * kb:2 * kb + 2] = t.reshape(2, SSD_P, SSD_N)


def _ssd_chunk(xs_ref, b_ref, c_ref, dt_ref, dtt_ref, dbr_ref, dbc_ref, dsk_ref, y_ref, st_ref, rows, d,
               a_row, a_col, causal, tri_l, tri_r, skip_on, first_head, first_head2):
    q = SSD_Q
    heads_per_group = SSD_H // SSD_G
    gw = heads_per_group * SSD_P
    dt_both = dt_ref[rows, :]
    dt_col = _softplus(jnp.where(d == 0, dt_both[:, 0:SSD_H], dt_both[:, SSD_H:2 * SSD_H]) + dbr_ref[...])
    dt_row = _softplus(dtt_ref[:, rows] + dbc_ref[...])
    cs_col = _dot_hi(tri_l, dt_col * a_row)
    da_row = dt_row * a_col
    cs_row = _dot_hi(da_row, tri_r)
    tot = jnp.sum(da_row, axis=-1, keepdims=True)
    w_row = dt_row * jnp.exp(tot - cs_row)
    etot = jnp.exp(tot)
    src_row = cs_row - jnp.log(dt_row)
    w_row_bf = w_row.astype(BF)
    for g in range(SSD_G):
        bg = b_ref[rows, SSD_N * g:SSD_N * (g + 1)]
        cg = c_ref[rows, SSD_N * g:SSD_N * (g + 1)].astype(BF)
        cb = _dot_nt(cg, bg.astype(BF)).astype(BF)
        bgt = bg.T.astype(BF)
        c_state = _dot(cg, st_ref[:, gw * g:gw * (g + 1)].astype(BF))
        for jp in range(heads_per_group // 2):
            pi = g * (heads_per_group // 2) + jp
            cols = slice(LANES * pi, LANES * (pi + 1))
            xp = xs_ref[rows, cols]
            xb = xp.astype(BF)
            res, grow = [], []
            for hh in range(2):
                h = 2 * pi + hh
                csc = jnp.broadcast_to(cs_col[:, h:h + 1], (q, q))
                grow.append(jnp.exp(csc))
                m = jnp.exp(jnp.where(causal, csc - src_row[h:h + 1, :], -jnp.inf)).astype(BF) * cb
                bw = bgt * w_row_bf[h:h + 1, :]
                res.append(_dot(jnp.concatenate([m, bw], axis=0), xb))
            both = jnp.where(first_head2, res[0], res[1])
            keep = jnp.where(first_head, etot[2 * pi:2 * pi + 1, :], etot[2 * pi + 1:2 * pi + 2, :])
            st_ref[:, cols] = keep * st_ref[:, cols] + both[q:q + SSD_N, :]
            y_off = jnp.where(first_head, grow[0], grow[1]) * c_state[:, LANES * jp:LANES * (jp + 1)]
            y_ref[rows, cols] = (both[0:q, :] + y_off + (skip_on * dsk_ref[:, cols]) * xp).astype(y_ref.dtype)


def _ssd(xc, dt_raw, dtt_raw, dt_bias, a_log, d_skip, h0, n_p, seq_p, seq_s):
    q = SSD_Q * SSD_STEP_CHUNKS
    assert seq_p % q == 0 and seq_s % q == 0
    total = xc.shape[0]
    n_steps = total // q
    n_chunks = total // SSD_Q
    n_prompt = n_p // seq_p
    n_sample = h0.shape[0]

    def blk(d, c):
        return jnp.where(d == 0, c, n_steps - 1 - c)

    def h0_idx(d, c):
        return (jnp.clip((blk(d, c) * q - n_p) // seq_s, 0, n_sample - 1), d, 0, 0, 0)

    def fin_idx(d, c):
        return (jnp.minimum(blk(d, c) * q // seq_p, n_prompt - 1), d, 0, 0, 0)

    state_block = (None, None, SSD_H, SSD_P, SSD_N)
    return pl.pallas_call(
        functools.partial(_ssd_kernel, n_chunks=n_chunks, n_p=n_p, seq_p=seq_p, seq_s=seq_s),
        grid=(2, n_steps),
        in_specs=[pl.BlockSpec((q, D_INNER), lambda d, c: (blk(d, c), 0)),
                  pl.BlockSpec((q, SSD_GN), lambda d, c: (blk(d, c), D_INNER // SSD_GN)),
                  pl.BlockSpec((q, SSD_GN), lambda d, c: (blk(d, c), D_INNER // SSD_GN + 1)),
                  pl.BlockSpec((q, 2 * SSD_H), lambda d, c: (blk(d, c), 0)),
                  pl.BlockSpec((SSD_H, q), lambda d, c: (d, blk(d, c))),
                  pl.BlockSpec((None, 1, SSD_H), lambda d, c: (d, 0, 0)),
                  pl.BlockSpec((None, SSD_H, 1), lambda d, c: (d, 0, 0)),
                  pl.BlockSpec((None, 1, SSD_H), lambda d, c: (d, 0, 0)),
                  pl.BlockSpec((None, SSD_H, 1), lambda d, c: (d, 0, 0)),
                  _resident((1, D_INNER)),
                  pl.BlockSpec(state_block, h0_idx)],
        out_specs=[pl.BlockSpec((None, q, D_INNER), lambda d, c: (d, blk(d, c), 0)),
                   pl.BlockSpec(state_block, fin_idx)],
        out_shape=[jax.ShapeDtypeStruct((2, total, D_INNER), BF),
                   jax.ShapeDtypeStruct((n_prompt, 2, SSD_H, SSD_P, SSD_N), F32)],
        scratch_shapes=[pltpu.VMEM((SSD_N, D_INNER), F32)],
        compiler_params=_params("parallel", "arbitrary"),
        name="ssd_scan",
    )(xc, xc, xc, dt_raw, dtt_raw, a_log.reshape(2, 1, SSD_H), a_log.reshape(2, SSD_H, 1),
      dt_bias.reshape(2, 1, SSD_H), dt_bias.reshape(2, SSD_H, 1), d_skip, h0)


def _odd_out_kernel(x_ref, mod_ref, yf_ref, yb_ref, z_ref, ng_ref, wout_ref,
                    gf_ref, wg_ref, wu_ref, wd_ref, fg_ref, *o_refs, p_steps, final):
    y = (yf_ref[...].astype(F32) + yb_ref[...].astype(F32)) * _silu(z_ref[...])
    inv = lax.rsqrt(jnp.mean(y * y, axis=-1, keepdims=True) + EPS)
    x1 = x_ref[...] + (mod_ref[2:3, :] * inv) * _dot((y * ng_ref[...]).astype(BF), wout_ref[...])
    x2 = _ffn(x1, mod_ref, gf_ref, wg_ref, wu_ref, wd_ref)
    if not final:
        o_refs[0][...] = x2
        return
    out = _rms(x2, fg_ref[...])
    is_p = pl.program_id(0) < p_steps

    @pl.when(is_p)
    def _():
        o_refs[0][...] = out

    @pl.when(jnp.logical_not(is_p))
    def _():
        o_refs[1][...] = out


def _odd_out(st, x, mods, y, z, norm_g, w_out, g_ffn, wg, wu, wd, layer, final_g, final):
    ydir = lambda d: pl.BlockSpec((None, st.tm, D_INNER), lambda i: (d, i, 0))
    return pl.pallas_call(
        functools.partial(_odd_out_kernel, p_steps=st.p_steps, final=final),
        grid=(st.steps,),
        in_specs=[st.row_spec(D_MODEL), st.mod_spec(), ydir(0), ydir(1), st.row_spec(D_INNER),
                  _resident((1, D_INNER)), _resident(w_out.shape),
                  _resident((1, D_MODEL)), _layer_resident(wg.shape, layer), _layer_resident(wu.shape, layer),
                  _layer_resident(wd.shape, layer), _resident((1, D_MODEL))],
        out_specs=[st.prompt_spec(D_MODEL), st.sample_spec(D_MODEL)] if final else st.row_spec(D_MODEL),
        out_shape=([jax.ShapeDtypeStruct((st.n_p, D_MODEL), F32),
                    jax.ShapeDtypeStruct((st.total - st.n_p, D_MODEL), F32)] if final
                   else jax.ShapeDtypeStruct((st.total, D_MODEL), F32)),
        compiler_params=_params("arbitrary"),
        name="odd_out_ffn",
    )(x, mods, y, y, z, norm_g, w_out, g_ffn, wg, wu, wd, final_g)


def kernel(x_prompt, x_sample, cache_na_k, cache_na_v, state_s5_re, state_s5_im, state_ssd, c, c_ctx, norm_mix_g, norm_ffn_g, ada_w, ada_b, ffn_w_gate, ffn_w_up, ffn_w_down, ev_w_in, ev_w_out, s5_a_re, s5_a_im, s5_log_dt, s5_b_re, s5_b_im, s5_c_re, s5_c_im, s5_d, s5_w_glu, na_rpb, od_w_in, od_conv_w, od_conv_b, ssd_a_log, ssd_dt_bias, ssd_d, ssd_norm_g, od_w_out, final_norm_g):
    bp, seq_p, d = x_prompt.shape
    bs, seq_s, _ = x_sample.shape
    depth = ada_w.shape[0]
    n_p = bp * seq_p
    assert d == D_MODEL and n_p % seq_s == 0
    st = _Stream(n_p, seq_s, bs, tm=math.gcd(512, math.gcd(n_p, seq_s)))

    xp = x_prompt.reshape(n_p, d)
    xs = x_sample.reshape(bs * seq_s, d)
    cond = jnp.concatenate([c_ctx[None, :], c, jnp.zeros((8 - 1 - bs, d), F32)], axis=0)
    mods = _ada(cond, ada_w, ada_b).reshape(depth, 8, ADA_CHUNKS, d)
    row = lambda t: t.reshape(1, -1)

    new_k, new_v, new_s5_re, new_s5_im, new_ssd = [], [], [], [], []
    wg, wu, wd = ffn_w_gate.astype(BF), ffn_w_up.astype(BF), ffn_w_down.astype(BF)
    for layer in range(depth):
        g_mix = row(norm_mix_g[layer])
        g_ffn = row(norm_ffn_g[layer])
        if layer % 2 == 0:
            e = layer // 2
            if layer > 0:
                xp, xs = x[:n_p], x[n_p:]
            u, q, k, v = _even_in(st, xp, xs, mods[layer], g_mix, ev_w_in[e].astype(BF))
            attn_p = _ctx_attn(q, k, v, bp, seq_p)
            heads = lambda t: t[:n_p].reshape(bp, seq_p, NA_HEADS, NA_HD).transpose(0, 2, 1, 3)
            new_k.append(heads(k))
            new_v.append(heads(v))
            ctx = lambda t: t[:, e].transpose(0, 2, 1, 3).reshape(bs, -1, D_NA).astype(BF)
            attn_s = _na_attn(q, k, v, ctx(cache_na_k), ctx(cache_na_v), _bias_table(na_rpb[e]),
                              bs, seq_s, n_p // seq_s)
            v_op, bw, cw, sc = _s5_prep(s5_a_re[e], s5_a_im[e], s5_log_dt[e], s5_b_re[e], s5_b_im[e],
                                        s5_c_re[e], s5_c_im[e])
            y, fin = _s5(u, v_op, bw, cw, sc, _state_planes(state_s5_re[:, e], state_s5_im[:, e]),
                         n_p, seq_p, seq_s)
            fre, fim = _planes_state(fin)
            new_s5_re.append(fre)
            new_s5_im.append(fim)
            x = _even_out(st, xp, xs, mods[layer], y, u, attn_p, attn_s, row(s5_d[e]),
                          s5_w_glu[e].astype(BF), ev_w_out[e].astype(BF), g_ffn, wg, wu, wd, layer)
        else:
            o = layer // 2
            z, xbc, dt_raw, dtt_raw = _odd_in(st, x, mods[layer], g_mix, od_w_in[o].astype(BF))
            xc = _conv(xbc, od_conv_w[o], od_conv_b[o], n_p, seq_p, seq_s)
            y, fin = _ssd(xc, dt_raw, dtt_raw, ssd_dt_bias[o], ssd_a_log[o], row(jnp.repeat(ssd_d[o], SSD_P)),
                          state_ssd[:, o], n_p, seq_p, seq_s)
            new_ssd.append(fin)
            x = _odd_out(st, x, mods[layer], y, z, row(ssd_norm_g[o]), od_w_out[o].astype(BF),
                         g_ffn, wg, wu, wd, layer, row(final_norm_g), layer == depth - 1)
    if depth % 2 == 1:
        raise NotImplementedError("final norm is fused into the last (odd) layer")
    y_prompt = x[0].reshape(bp, seq_p, d)
    y_sample = x[1].reshape(bs, seq_s, d)
    return (y_prompt, y_sample, jnp.stack(new_k, axis=1), jnp.stack(new_v, axis=1),
            jnp.stack(new_s5_re, axis=1), jnp.stack(new_s5_im, axis=1), jnp.stack(new_ssd, axis=1))
```

```python
import functools
import math

import jax
import jax.numpy as jnp
from jax import lax
from jax.experimental import pallas as pl
from jax.experimental.pallas import tpu as pltpu

F32 = jnp.float32
BF = jnp.bfloat16
HI = lax.Precision.HIGHEST

D_MODEL = 1024
EPS = 1e-6
ADA_CHUNKS = 6
GRID_W = 64
D_S5 = 512
S5_P = 16
S5_G = D_S5 // S5_P
S5_N = 64
S5_MIN_DECAY = 1e-4
S5_Q = 16
D_NA = 512
NA_HD = 64
NA_HEADS = D_NA // NA_HD
NA_WIN_ROWS = 8
NA_WIN_COLS = 16
NA_ROWS = 4
D_INNER = 2048
SSD_P = 64
SSD_H = D_INNER // SSD_P
SSD_G = 4
SSD_N = 128
SSD_Q = 128
SSD_STEP_CHUNKS = 2
SSD_CONV = 5
SSD_GN = SSD_G * SSD_N
SSD_CONV_DIM = D_INNER + 2 * SSD_GN
D_FF = 2816

V7X_VMEM_BYTES = 64 * 1024 * 1024
VMEM_LIMIT = V7X_VMEM_BYTES - 4 * 1024 * 1024
LANES = 128
NEG_BIG = -1e30


def _params(*sem):
    return pltpu.CompilerParams(dimension_semantics=sem, vmem_limit_bytes=VMEM_LIMIT)


def _resident(shape):
    nd = len(shape)
    return pl.BlockSpec(shape, lambda *_: (0,) * nd, pipeline_mode=pl.Buffered(1))


def _layer_resident(shape, layer):
    nd = len(shape)
    return pl.BlockSpec((None,) + tuple(shape[1:]), lambda *_: (layer,) + (0,) * (nd - 1), pipeline_mode=pl.Buffered(1))


def _dot(a, b):
    return jnp.dot(a, b, preferred_element_type=F32)


def _dot_nt(a, b):
    return lax.dot_general(a, b, (((1,), (1,)), ((), ())), preferred_element_type=F32)


def _dot_hi(a, b):
    return jnp.dot(a, b, preferred_element_type=F32, precision=HI)


def _sigmoid(x):
    return 1.0 / (1.0 + jnp.exp(-x))


def _silu(x):
    return x * _sigmoid(x)


def _softplus(x):
    return jnp.maximum(x, 0.0) + jnp.log1p(jnp.exp(-jnp.abs(x)))


def _rms(x, g):
    return x * lax.rsqrt(jnp.mean(x * x, axis=-1, keepdims=True) + EPS) * g


def _norm_mod(x, g, shift, scale):
    return _rms(x, g) * (1.0 + scale) + shift


def _ada_kernel(c_ref, w_ref, b_ref, o_ref):
    c = c_ref[...]
    o_ref[...] = _dot(_silu(c).astype(BF), w_ref[...].astype(BF)) + b_ref[...]


def _ada(cond, ada_w, ada_b):
    depth, d, n = ada_w.shape
    tn = 1536
    return pl.pallas_call(
        _ada_kernel,
        grid=(depth, n // tn),
        in_specs=[pl.BlockSpec((8, d), lambda l, j: (0, 0)),
                  pl.BlockSpec((None, d, tn), lambda l, j: (l, 0, j)),
                  pl.BlockSpec((None, 1, tn), lambda l, j: (l, 0, j))],
        out_specs=pl.BlockSpec((None, 8, tn), lambda l, j: (l, 0, j)),
        out_shape=jax.ShapeDtypeStruct((depth, 8, n), F32),
        compiler_params=_params("parallel", "parallel"),
        name="adaln",
    )(cond, ada_w, ada_b.reshape(depth, 1, n))


class _Stream:
    def __init__(self, n_p, len_s, n_s, tm):
        assert n_p % tm == 0 and len_s % tm == 0
        self.n_p, self.len_s, self.n_s, self.tm = n_p, len_s, n_s, tm
        self.total = n_p + len_s * n_s
        self.steps = self.total // tm
        self.p_steps = n_p // tm

    def group(self, i):
        t = i * self.tm
        return jnp.where(t < self.n_p, 0, 1 + (t - self.n_p) // self.len_s)

    def mod_spec(self):
        return pl.BlockSpec((None, ADA_CHUNKS, D_MODEL), lambda i: (self.group(i), 0, 0))

    def row_spec(self, width, col=0):
        return pl.BlockSpec((self.tm, width), lambda i: (i, col))

    def prompt_spec(self, width):
        return pl.BlockSpec((self.tm, width), lambda i: (jnp.minimum(i, self.p_steps - 1), 0))

    def sample_spec(self, width):
        return pl.BlockSpec((self.tm, width), lambda i: (jnp.maximum(i - self.p_steps, 0), 0))


def _ffn(x1, mod_ref, g_ref, wg_ref, wu_ref, wd_ref):
    h = _norm_mod(x1, g_ref[...], mod_ref[3:4, :], mod_ref[4:5, :]).astype(BF)
    hid = (_silu(_dot(h, wg_ref[...])) * _dot(h, wu_ref[...])).astype(BF)
    return x1 + mod_ref[5:6, :] * _dot(hid, wd_ref[...])


def _even_in_kernel(xp_ref, xs_ref, mod_ref, g_ref, w_ref, u_ref, q_ref, k_ref, v_ref, *, p_steps):
    x = jnp.where(pl.program_id(0) < p_steps, xp_ref[...], xs_ref[...])
    h = _norm_mod(x, g_ref[...], mod_ref[0:1, :], mod_ref[1:2, :])
    r = _dot(h.astype(BF), w_ref[...])
    u_ref[...] = r[:, 0:D_S5]
    q_ref[...] = r[:, D_S5:D_S5 + D_NA] * (NA_HD ** -0.5)
    k_ref[...] = r[:, D_S5 + D_NA:D_S5 + 2 * D_NA]
    v_ref[...] = r[:, D_S5 + 2 * D_NA:D_S5 + 3 * D_NA]


def _even_in(st, xp, xs, mods, g, w_in):
    n_out = w_in.shape[1]
    out = jax.ShapeDtypeStruct((st.total, D_S5), F32)
    return pl.pallas_call(
        functools.partial(_even_in_kernel, p_steps=st.p_steps),
        grid=(st.steps,),
        in_specs=[st.prompt_spec(D_MODEL), st.sample_spec(D_MODEL), st.mod_spec(), _resident((1, D_MODEL)),
                  _resident((D_MODEL, n_out))],
        out_specs=[st.row_spec(D_S5)] * 4,
        out_shape=[out] * 4,
        compiler_params=_params("parallel"),
        name="even_in",
    )(xp, xs, mods, g, w_in)


def _ctx_attn_kernel(q_ref, k_ref, v_ref, o_ref):
    lane = lax.broadcasted_iota(jnp.int32, (q_ref.shape[0], LANES), 1)
    for pair in range(D_NA // LANES):
        cols = slice(LANES * pair, LANES * (pair + 1))
        q = q_ref[:, cols]
        k = k_ref[:, cols].astype(BF)
        v = v_ref[:, cols].astype(BF)
        outs = []
        for h in range(2):
            in_head = (lane >= NA_HD * h) & (lane < NA_HD * (h + 1))
            qh = jnp.where(in_head, q, 0.0).astype(BF)
            s = _dot_nt(qh, k)
            p = jnp.exp(s - jnp.max(s, axis=-1, keepdims=True))
            l = jnp.sum(p, axis=-1, keepdims=True)
            outs.append(_dot(p.astype(BF), v) / l)
        o_ref[:, cols] = jnp.where(lane < NA_HD, outs[0], outs[1])


def _ctx_attn(q, k, v, n_seq, seq):
    spec = pl.BlockSpec((seq, D_NA), lambda b: (b, 0))
    return pl.pallas_call(
        _ctx_attn_kernel,
        grid=(n_seq,),
        in_specs=[spec, spec, spec],
        out_specs=spec,
        out_shape=jax.ShapeDtypeStruct((n_seq * seq, D_NA), F32),
        compiler_params=_params("parallel"),
        name="ctx_attn",
    )(q, k, v)


NA_N_DR = 2 * NA_WIN_ROWS - 1
NA_BOTH, NA_LEFT, NA_RIGHT = 0, NA_N_DR - 1, 2 * NA_N_DR - 1
NA_NONE = 3 * NA_N_DR - 1
NA_UNION = NA_WIN_ROWS + NA_ROWS


def _bias_kernel(rpb_ref, o_ref):
    half = LANES // 2
    wq = lax.broadcasted_iota(jnp.int32, (GRID_W, LANES), 0)
    lane = lax.broadcasted_iota(jnp.int32, (GRID_W, LANES), 1)
    wk = lane & (GRID_W - 1)
    left = lane < GRID_W
    col_start = jnp.clip(wq - NA_WIN_COLS // 2, 0, GRID_W - NA_WIN_COLS)
    ok = (wk >= col_start) & (wk < col_start + NA_WIN_COLS)
    dc = jnp.clip(wk - wq, -(NA_WIN_COLS - 1), NA_WIN_COLS - 1) + (NA_WIN_COLS - 1)
    idx = jnp.where(left, dc, dc + half)
    low_lanes = lax.broadcasted_iota(jnp.int32, (1, LANES), 1) < half

    def entry(e, carry):
        is_both = e < NA_LEFT
        is_left = jnp.logical_and(e >= NA_LEFT, e < NA_RIGHT)
        is_right = jnp.logical_and(e >= NA_RIGHT, e < NA_NONE)
        d_left = jnp.where(is_both, e, jnp.where(is_left, e - NA_LEFT, 0))
        d_right = jnp.where(is_both, e + 1, jnp.where(is_right, e - NA_RIGHT, 0))
        left_on = jnp.where(jnp.logical_or(is_both, is_left), 1, 0)
        right_on = jnp.where(jnp.logical_or(is_both, is_right), 1, 0)
        vals = jnp.where(low_lanes, rpb_ref[pl.ds(d_left, 1), :], pltpu.roll(rpb_ref[pl.ds(d_right, 1), :], half, 1))
        tile = jnp.take_along_axis(jnp.broadcast_to(vals, (GRID_W, LANES)), idx, axis=1)
        side_on = jnp.where(left, left_on, right_on) > 0
        o_ref[e] = jnp.where(ok & side_on, tile, NEG_BIG)
        return carry

    lax.fori_loop(0, NA_NONE + 1, entry, 0, unroll=9)


def _bias_table(rpb):
    n_dc = rpb.shape[-1]
    padded = jnp.pad(rpb, ((0, 0), (0, 0), (0, LANES - n_dc)))
    return pl.pallas_call(
        _bias_kernel,
        grid=(NA_HEADS,),
        in_specs=[pl.BlockSpec((None, NA_N_DR, LANES), lambda h: (h, 0, 0))],
        out_specs=pl.BlockSpec((None, NA_NONE + 1, GRID_W, LANES), lambda h: (h, 0, 0, 0)),
        out_shape=jax.ShapeDtypeStruct((NA_HEADS, NA_NONE + 1, GRID_W, LANES), F32),
        compiler_params=_params("parallel"),
        name="na_bias",
    )(padded)


def _na_kernel(q_ref, k_ref, v_ref, ck_ref, cv_ref, bias_ref, o_ref, kb_ref, vb_ref, *, rows):
    kb_ref[...] = k_ref[...].astype(BF)
    vb_ref[...] = v_ref[...].astype(BF)
    ck = ck_ref[...]
    cv = cv_ref[...]
    kh = NA_WIN_ROWS
    nq = NA_ROWS * GRID_W
    lane = lax.broadcasted_iota(jnp.int32, (nq, LANES), 1)

    def body(g, carry):
        r0 = g * NA_ROWS
        first = jnp.clip(r0 - kh // 2, 0, rows - NA_UNION)
        qoff = pl.multiple_of(r0 * GRID_W, nq)
        q = q_ref[pl.ds(qoff, nq), :]
        koff = pl.multiple_of(first * GRID_W, GRID_W)
        kw = kb_ref[pl.ds(koff, NA_UNION * GRID_W), :]
        vw = vb_ref[pl.ds(koff, NA_UNION * GRID_W), :]
        tile_idx = []
        for j in range(NA_ROWS):
            r = r0 + j
            start = jnp.clip(r - kh // 2, 0, rows - kh)
            for ip in range(NA_UNION // 2):
                k0 = first + 2 * ip
                in0 = jnp.logical_and(k0 >= start, k0 < start + kh)
                in1 = jnp.logical_and(k0 + 1 >= start, k0 + 1 < start + kh)
                dr0 = k0 - r + (NA_WIN_ROWS - 1)
                idx = jnp.where(jnp.logical_and(in0, in1), NA_BOTH + dr0,
                                jnp.where(in0, NA_LEFT + dr0, jnp.where(in1, NA_RIGHT + dr0 + 1, NA_NONE)))
                tile_idx.append(jnp.clip(idx, 0, NA_NONE))
        outs = []
        for h in range(2):
            in_head = (lane >= NA_HD * h) & (lane < NA_HD * (h + 1))
            qh = jnp.where(in_head, q, 0.0).astype(BF)
            per_row = NA_UNION // 2
            bias = jnp.concatenate(
                [jnp.concatenate([bias_ref[h, tile_idx[j * per_row + ip]] for ip in range(per_row)], axis=1)
                 for j in range(NA_ROWS)], axis=0)
            s_loc = _dot_nt(qh, kw) + bias
            s_ctx = _dot_nt(qh, ck)
            m = jnp.maximum(jnp.max(s_loc, axis=-1, keepdims=True), jnp.max(s_ctx, axis=-1, keepdims=True))
            p_loc = jnp.exp(s_loc - m)
            p_ctx = jnp.exp(s_ctx - m)
            l = jnp.sum(p_loc, axis=-1, keepdims=True) + jnp.sum(p_ctx, axis=-1, keepdims=True)
            outs.append((_dot(p_loc.astype(BF), vw) + _dot(p_ctx.astype(BF), cv)) / l)
        o_ref[pl.ds(qoff, nq), :] = jnp.where(lane < NA_HD, outs[0], outs[1])
        return carry

    lax.fori_loop(0, rows // NA_ROWS, body, 0, unroll=2)


def _na_attn(q, k, v, ck, cv, bias, n_seq, seq, row_base):
    rows = seq // GRID_W
    assert rows >= NA_UNION and rows % NA_ROWS == 0
    past = ck.shape[1]
    spec = pl.BlockSpec((seq, LANES), lambda b, p: (row_base + b, p))
    cspec = pl.BlockSpec((None, past, LANES), lambda b, p: (b, 0, p))
    return pl.pallas_call(
        functools.partial(_na_kernel, rows=rows),
        grid=(n_seq, D_NA // LANES),
        in_specs=[spec, spec, spec, cspec, cspec,
                  pl.BlockSpec((2, NA_NONE + 1, GRID_W, LANES), lambda b, p: (p, 0, 0, 0))],
        out_specs=pl.BlockSpec((seq, LANES), lambda b, p: (b, p)),
        out_shape=jax.ShapeDtypeStruct((n_seq * seq, D_NA), F32),
        scratch_shapes=[pltpu.VMEM((seq, LANES), BF), pltpu.VMEM((seq, LANES), BF)],
        compiler_params=_params("parallel", "parallel"),
        name="na_attn",
    )(q, k, v, ck, cv, bias)


S5_BLK = LANES // S5_P
S5_NB = S5_G // S5_BLK
S5_ST = S5_BLK * S5_N
S5_W = S5_Q * LANES


def _cexp(zr, zi):
    e = jnp.exp(zr)
    return e * jnp.cos(zi), e * jnp.sin(zi)


def _cmul(ar, ai, br, bi):
    return ar * br - ai * bi, ar * bi + ai * br


def _s5_prep_kernel(arr_ref, ari_ref, ldr_ref, acr_ref, aci_ref, ldc_ref, btr_ref, bti_ref, ctr_ref, cti_ref,
                    v_ref, bw_ref, cw_ref, sc_ref,
                    bbr_s, bbi_s, ccr_s, cci_s, pwr_s, pwi_s, pcr_s, pci_s, p0_s, p1_s):
    t = pl.program_id(1)
    q = S5_Q
    nst = S5_ST

    @pl.when(t == 0)
    def _():
        expand = jnp.where(lax.broadcasted_iota(jnp.int32, (S5_P, 2 * LANES), 0)
                           == (lax.broadcasted_iota(jnp.int32, (S5_P, 2 * LANES), 1) & (S5_P - 1)), 1.0, 0.0)
        row_g = lax.broadcasted_iota(jnp.int32, (LANES, nst), 0) >> 4
        col_g = lax.broadcasted_iota(jnp.int32, (LANES, nst), 1) >> 6
        st_g = lax.broadcasted_iota(jnp.int32, (nst, LANES), 0) >> 6
        ch_g = lax.broadcasted_iota(jnp.int32, (nst, LANES), 1) >> 4
        lane_d = lax.broadcasted_iota(jnp.int32, (nst, LANES), 1) >> 4
        taps = []
        for d in range(2):
            ar = jnp.minimum(arr_ref[d], -S5_MIN_DECAY)
            ai = ari_ref[d]
            dt = jnp.exp(ldr_ref[d])
            abr, abi = _cexp(ar * dt, ai * dt)
            den = ar * ar + ai * ai
            cfr = ((abr - 1.0) * ar + abi * ai) / den
            cfi = (abi * ar - (abr - 1.0) * ai) / den
            bbr, bbi = _cmul(cfr, cfi, btr_ref[d], bti_ref[d])
            bbr = jnp.where(row_g == col_g, jnp.concatenate([bbr] * S5_BLK, axis=0), 0.0)
            bbi = jnp.where(row_g == col_g, jnp.concatenate([bbi] * S5_BLK, axis=0), 0.0)
            bbr_s[d] = bbr
            bbi_s[d] = bbi
            pr = jnp.ones((1, nst), F32)
            pi = jnp.zeros((1, nst), F32)
            for e in range(q + 1):
                pwr_s[d, e] = pr
                pwi_s[d, e] = pi
                pr, pi = _cmul(pr, pi, abr, abi)
            sc_ref[2 * d:2 * d + 1, :] = pwr_s[d, q]
            sc_ref[2 * d + 1:2 * d + 2, :] = pwi_s[d, q]
            arc = jnp.minimum(acr_ref[d], -S5_MIN_DECAY)
            dtc = jnp.exp(ldc_ref[d])
            acr, aci = _cexp(jnp.broadcast_to(arc * dtc, (nst, LANES)), jnp.broadcast_to(aci_ref[d] * dtc, (nst, LANES)))
            pr = jnp.ones((nst, LANES), F32)
            pi = jnp.zeros((nst, LANES), F32)
            for e in range(q + 1):
                pcr_s[d, e] = pr
                pci_s[d, e] = pi
                pr, pi = _cmul(pr, pi, acr, aci)
            cxr = _dot_hi(ctr_ref[d], expand)
            cxi = _dot_hi(cti_ref[d], expand)
            ccr_s[d] = jnp.where(st_g == ch_g, cxr[:, 0:LANES], 0.0)
            cci_s[d] = jnp.where(st_g == ch_g, cxi[:, 0:LANES], 0.0)
            pws_r, pws_i = [], []
            for k in range(2):
                sel_r = jnp.zeros((nst, LANES), F32)
                sel_i = jnp.zeros((nst, LANES), F32)
                for j in range(S5_BLK):
                    dl = S5_BLK * k + j
                    e = dl if d == 0 else q - 1 - dl
                    sel_r = jnp.where(lane_d == j, pcr_s[d, e], sel_r)
                    sel_i = jnp.where(lane_d == j, pci_s[d, e], sel_i)
                pws_r.append(sel_r)
                pws_i.append(sel_i)
            ggr, ggi = _cmul(cxr, cxi, jnp.concatenate(pws_r, axis=1), jnp.concatenate(pws_i, axis=1))
            taps.append(_dot_hi(bbr, ggr) - _dot_hi(bbi, ggi))
        lane2 = lax.broadcasted_iota(jnp.int32, (LANES, 2 * LANES), 1)
        kt0 = taps[0] + jnp.where(lane2 < S5_P, pltpu.roll(taps[1], S5_P, 1), 0.0)
        kt1 = jnp.where(lane2 >= 2 * LANES - S5_P, 0.0, taps[1])
        er = lax.broadcasted_iota(jnp.int32, (2 * LANES, S5_W), 0)
        ec = lax.broadcasted_iota(jnp.int32, (2 * LANES, S5_W), 1)
        place = jnp.where(((er >> 4) == (ec >> 7)) & ((er & (S5_P - 1)) == (ec & (S5_P - 1))), 1.0, 0.0).astype(BF)
        own = (lax.broadcasted_iota(jnp.int32, (LANES, S5_W), 0) >> 4) == (
            (lax.broadcasted_iota(jnp.int32, (LANES, S5_W), 1) >> 4) & (S5_BLK - 1))
        zeros = jnp.zeros((LANES, S5_W), BF)
        p0_s[:, 0:S5_W] = zeros
        p0_s[:, S5_W:2 * S5_W] = jnp.where(own, _dot(kt0.astype(BF), place), 0.0).astype(BF)
        p1_s[:, 0:S5_W] = jnp.where(own, _dot(kt1.astype(BF), place), 0.0).astype(BF)
        p1_s[:, S5_W:2 * S5_W] = zeros

    off0 = pl.multiple_of(S5_W - LANES * t, LANES)
    off1 = pl.multiple_of(LANES * (q - 1 - t), LANES)
    v_ref[...] = p0_s[:, pl.ds(off0, S5_W)] + p1_s[:, pl.ds(off1, S5_W)]
    for d in range(2):
        e_b = (q - 1 - t) if d == 0 else t
        br, bi = _cmul(pwr_s[d, e_b], pwi_s[d, e_b], bbr_s[d], bbi_s[d])
        bw_ref[:, 2 * nst * d:2 * nst * d + nst] = br.astype(BF)
        bw_ref[:, 2 * nst * d + nst:2 * nst * (d + 1)] = bi.astype(BF)
        e_c = (t + 1) if d == 0 else (q - t)
        gr, gi = _cmul(ccr_s[d], cci_s[d], pcr_s[d, e_c], pci_s[d, e_c])
        cw_ref[2 * nst * d:2 * nst * d + nst, :] = gr.astype(BF)
        cw_ref[2 * nst * d + nst:2 * nst * (d + 1), :] = (-gi).astype(BF)


def _s5_prep(a_re, a_im, log_dt, b_re, b_im, c_re, c_im):
    nst, nb, q = S5_ST, S5_NB, S5_Q
    row = lambda t: t.reshape(2, nb, 1, nst)
    col = lambda t: t.reshape(2, nb, nst, 1)
    ld = jnp.broadcast_to(log_dt[:, :, None], (2, S5_G, S5_N))
    bt = lambda t: t.reshape(2, nb, S5_BLK, S5_N, S5_P).transpose(0, 1, 4, 2, 3).reshape(2, nb, S5_P, nst)
    ct = lambda t: t.reshape(2, nb, S5_BLK, S5_P, S5_N).transpose(0, 1, 2, 4, 3).reshape(2, nb, nst, S5_P)
    rspec = pl.BlockSpec((2, None, 1, nst), lambda b, t: (0, b, 0, 0))
    cspec = pl.BlockSpec((2, None, nst, 1), lambda b, t: (0, b, 0, 0))
    btspec = pl.BlockSpec((2, None, S5_P, nst), lambda b, t: (0, b, 0, 0))
    ctspec = pl.BlockSpec((2, None, nst, S5_P), lambda b, t: (0, b, 0, 0))
    big = jax.ShapeDtypeStruct((nb, S5_W, S5_W), BF)
    return pl.pallas_call(
        _s5_prep_kernel,
        grid=(nb, q),
        in_specs=[rspec, rspec, rspec, cspec, cspec, cspec, btspec, btspec, ctspec, ctspec],
        out_specs=[pl.BlockSpec((None, LANES, S5_W), lambda b, t: (b, t, 0)),
                   pl.BlockSpec((None, LANES, S5_W), lambda b, t: (b, t, 0)),
                   pl.BlockSpec((None, S5_W, LANES), lambda b, t: (b, 0, t)),
                   pl.BlockSpec((None, 4, nst), lambda b, t: (b, 0, 0))],
        out_shape=[big, big, big, jax.ShapeDtypeStruct((nb, 4, nst), F32)],
        scratch_shapes=[pltpu.VMEM((2, LANES, nst), F32), pltpu.VMEM((2, LANES, nst), F32),
                        pltpu.VMEM((2, nst, LANES), F32), pltpu.VMEM((2, nst, LANES), F32),
                        pltpu.VMEM((2, q + 1, 1, nst), F32), pltpu.VMEM((2, q + 1, 1, nst), F32),
                        pltpu.VMEM((2, q + 1, nst, LANES), F32), pltpu.VMEM((2, q + 1, nst, LANES), F32),
                        pltpu.VMEM((LANES, 2 * S5_W), BF), pltpu.VMEM((LANES, 2 * S5_W), BF)],
        compiler_params=_params("parallel", "arbitrary"),
        name="s5_prep",
    )(row(a_re), row(a_im), row(ld), col(a_re), col(a_im), col(ld), bt(b_re), bt(b_im), ct(c_re), ct(c_im))


def _s5_kernel(u_ref, v_ref, bw_ref, cw_ref, sc_ref, h0_ref, y_ref, fin_ref, z_s, h_s, *, p_steps, n_sub, mp):
    s = pl.program_id(1)
    m = u_ref.shape[0] // S5_Q
    n_tiles = 4 * S5_ST // LANES
    per = S5_ST // LANES
    half = 2 * per
    chunk_rows = lambda t: pl.ds(t, m, stride=S5_Q)
    tile_rows = lambda k: pl.ds(k, m, stride=n_tiles)
    ucat = jnp.concatenate([u_ref[chunk_rows(t), :].astype(BF) for t in range(S5_Q)], axis=1)
    z = _dot(ucat, bw_ref[...])
    for k in range(n_tiles):
        z_s[tile_rows(k), :] = z[:, LANES * k:LANES * (k + 1)]
    sc = sc_ref[...]
    stack = lambda row: jnp.concatenate([row[:, LANES * j:LANES * (j + 1)] for j in range(per)], axis=0)

    def multipliers(d):
        ar, ai = stack(sc[2 * d:2 * d + 1]), stack(sc[2 * d + 1:2 * d + 2])
        return jnp.concatenate([ar, ar], axis=0), jnp.concatenate([-ai, ai], axis=0)

    a1f, a2f = multipliers(0)
    a1b, a2b = multipliers(1)

    def advance(hf, hb, cf, cb):
        rf = pl.ds(pl.multiple_of(cf * n_tiles, half), half)
        rb = pl.ds(pl.multiple_of(cb * n_tiles + half, half), half)
        h_s[rf, :] = hf
        h_s[rb, :] = hb
        hf = a1f * hf + a2f * pltpu.roll(hf, per, 0) + z_s[rf, :]
        hb = a1b * hb + a2b * pltpu.roll(hb, per, 0) + z_s[rb, :]
        return hf, hb

    unstack = lambda h4: jnp.concatenate([h4[j:j + 1, :] for j in range(per)], axis=1)

    @pl.when(s < p_steps)
    def _():
        for i in range(n_sub):
            hf = jnp.zeros((half, LANES), F32)
            hb = jnp.zeros((half, LANES), F32)
            for c in range(mp):
                hf, hb = advance(hf, hb, i * mp + c, i * mp + mp - 1 - c)
            for pi, h4 in enumerate((hf[0:per], hf[per:half], hb[0:per], hb[per:half])):
                fin_ref[i, pi:pi + 1, :] = unstack(h4)

    @pl.when(s >= p_steps)
    def _():
        h0 = h0_ref[...]
        init = (jnp.concatenate([stack(h0[0:1]), stack(h0[1:2])], axis=0),
                jnp.concatenate([stack(h0[2:3]), stack(h0[3:4])], axis=0))
        lax.fori_loop(0, m, lambda c, h: advance(h[0], h[1], c, m - 1 - c), init)

    hcat = jnp.concatenate([h_s[tile_rows(k), :] for k in range(n_tiles)], axis=1).astype(BF)
    ycat = _dot(ucat, v_ref[...]) + _dot(hcat, cw_ref[...])
    for t in range(S5_Q):
        y_ref[chunk_rows(t), :] = ycat[:, LANES * t:LANES * (t + 1)]


def _s5(u, v, bw, cw, sc, h0, n_p, seq_p, seq_s):
    total = u.shape[0]
    q = S5_Q
    m = seq_s // q
    mp = seq_p // q
    n_sub = seq_s // seq_p
    p_steps = n_p // seq_s
    n_prompt = n_p // seq_p
    wspec = pl.BlockSpec((None, S5_W, S5_W), lambda b, s: (b, 0, 0), pipeline_mode=pl.Buffered(1))
    tok = pl.BlockSpec((seq_s, LANES), lambda b, s: (s, b))
    y, fin = pl.pallas_call(
        functools.partial(_s5_kernel, p_steps=p_steps, n_sub=n_sub, mp=mp),
        grid=(S5_NB, total // seq_s),
        in_specs=[tok, wspec, wspec, wspec,
                  pl.BlockSpec((None, 4, S5_ST), lambda b, s: (b, 0, 0)),
                  pl.BlockSpec((None, 4, S5_ST), lambda b, s: (jnp.maximum(s - p_steps, 0), 0, b))],
        out_specs=[tok, pl.BlockSpec((n_sub, 4, S5_ST), lambda b, s: (jnp.minimum(s, p_steps - 1), 0, b))],
        out_shape=[jax.ShapeDtypeStruct((total, D_S5), F32),
                   jax.ShapeDtypeStruct((n_prompt, 4, S5_G * S5_N), F32)],
        scratch_shapes=[pltpu.VMEM((4 * S5_ST // LANES * m, LANES), F32)] * 2,
        compiler_params=_params("parallel", "arbitrary"),
        name="s5_scan",
    )(u, v, bw, cw, sc, h0)
    return y, fin


def _state_planes(re, im):
    b = re.shape[0]
    return jnp.stack([t[:, d].reshape(b, S5_G * S5_N) for d in range(2) for t in (re, im)], axis=1)


def _planes_state(fin):
    t = fin.reshape(fin.shape[0], 2, 2, S5_G, S5_N)
    return t[:, :, 0], t[:, :, 1]


def _gelu_tanh(x):
    return 0.5 * x * (1.0 + jnp.tanh(math.sqrt(2.0 / math.pi) * (x + 0.044715 * (x * x * x))))


def _even_out_kernel(xp_ref, xs_ref, mod_ref, y_ref, u_ref, ap_ref, as_ref, dsk_ref, wglu_ref, wout_ref,
                     gf_ref, wg_ref, wu_ref, wd_ref, o_ref, *, p_steps):
    is_p = pl.program_id(0) < p_steps
    attn = jnp.where(is_p, ap_ref[...], as_ref[...])
    y = _gelu_tanh(y_ref[...] + u_ref[...] * dsk_ref[...])
    y = y * _sigmoid(_dot(y.astype(BF), wglu_ref[...]))
    out = _dot(y.astype(BF), wout_ref[0:D_S5, :]) + _dot(attn.astype(BF), wout_ref[D_S5:D_S5 + D_NA, :])
    x1 = jnp.where(is_p, xp_ref[...], xs_ref[...]) + mod_ref[2:3, :] * out
    o_ref[...] = _ffn(x1, mod_ref, gf_ref, wg_ref, wu_ref, wd_ref)


def _even_out(st, xp, xs, mods, y, u, attn_p, attn_s, d_skip, w_glu, w_out, g_ffn, wg, wu, wd, layer):
    return pl.pallas_call(
        functools.partial(_even_out_kernel, p_steps=st.p_steps),
        grid=(st.steps,),
        in_specs=[st.prompt_spec(D_MODEL), st.sample_spec(D_MODEL), st.mod_spec(), st.row_spec(D_S5),
                  st.row_spec(D_S5), st.prompt_spec(D_NA), st.sample_spec(D_NA),
                  _resident((1, D_S5)), _resident(w_glu.shape), _resident(w_out.shape),
                  _resident((1, D_MODEL)), _layer_resident(wg.shape, layer), _layer_resident(wu.shape, layer),
                  _layer_resident(wd.shape, layer)],
        out_specs=st.row_spec(D_MODEL),
        out_shape=jax.ShapeDtypeStruct((st.total, D_MODEL), F32),
        compiler_params=_params("parallel"),
        name="even_out_ffn",
    )(xp, xs, mods, y, u, attn_p, attn_s, d_skip, w_glu, w_out, g_ffn, wg, wu, wd)


def _odd_in_kernel(x_ref, mod_ref, g_ref, w_ref, wdtt_ref, z_ref, xbc_ref, dt_ref, dtt_ref):
    h = _norm_mod(x_ref[...], g_ref[...], mod_ref[0:1, :], mod_ref[1:2, :]).astype(BF)
    z_ref[...] = _dot(h, w_ref[:, 0:D_INNER]).astype(z_ref.dtype)
    xbc_ref[...] = _dot(h, w_ref[:, D_INNER:D_INNER + SSD_CONV_DIM])
    dt_ref[...] = _dot(h, w_ref[:, D_INNER + SSD_CONV_DIM:D_INNER + SSD_CONV_DIM + 2 * SSD_H])
    dtt_ref[...] = _dot_nt(wdtt_ref[...], h)


def _odd_in(st, x, mods, g, w_in):
    wdtt = w_in[:, D_INNER + SSD_CONV_DIM:].T
    return pl.pallas_call(
        _odd_in_kernel,
        grid=(st.steps,),
        in_specs=[st.row_spec(D_MODEL), st.mod_spec(), _resident((1, D_MODEL)),
                  _resident(w_in.shape), _resident(wdtt.shape)],
        out_specs=[st.row_spec(D_INNER), st.row_spec(SSD_CONV_DIM), st.row_spec(2 * SSD_H),
                   pl.BlockSpec((2 * SSD_H, st.tm), lambda i: (0, i))],
        out_shape=[jax.ShapeDtypeStruct((st.total, D_INNER), BF),
                   jax.ShapeDtypeStruct((st.total, SSD_CONV_DIM), F32),
                   jax.ShapeDtypeStruct((st.total, 2 * SSD_H), F32),
                   jax.ShapeDtypeStruct((2 * SSD_H, st.total), F32)],
        compiler_params=_params("parallel"),
        name="odd_in",
    )(x, mods, g, w_in, wdtt)


def _conv_kernel(x_ref, prev_ref, next_ref, w_ref, b_ref, xc_ref, *, lt, n_p, seq_p, seq_s):
    tok = pl.program_id(0) * lt
    in_p = tok < n_p
    pos = jnp.where(in_p, tok % seq_p, (tok - n_p) % seq_s)
    seq = jnp.where(in_p, seq_p, seq_s)
    halo = prev_ref.shape[0]
    prev = jnp.where(pos == 0, 0.0, prev_ref[...])
    nxt = jnp.where(pos + lt == seq, 0.0, next_ref[...])
    ext = jnp.concatenate([prev, x_ref[...], nxt], axis=0)
    n_ext = lt + 2 * halo
    acc = b_ref[...] + jnp.zeros((lt, x_ref.shape[1]), F32)
    for kk in range(SSD_CONV):
        shift = (SSD_CONV // 2 - kk) % n_ext
        tap = ext if shift == 0 else pltpu.roll(ext, shift, 0)
        acc = acc + w_ref[kk:kk + 1, :] * tap[halo:halo + lt, :]
    xc_ref[...] = _silu(acc)


def _conv(xbc, conv_w, conv_b, n_p, seq_p, seq_s):
    total, c = xbc.shape
    lt = math.gcd(seq_p, 256)
    halo = 8
    nblk = total // halo
    per = lt // halo
    return pl.pallas_call(
        functools.partial(_conv_kernel, lt=lt, n_p=n_p, seq_p=seq_p, seq_s=seq_s),
        grid=(total // lt,),
        in_specs=[pl.BlockSpec((lt, c), lambda i: (i, 0)),
                  pl.BlockSpec((halo, c), lambda i: (jnp.maximum(i * per - 1, 0), 0)),
                  pl.BlockSpec((halo, c), lambda i: (jnp.minimum((i + 1) * per, nblk - 1), 0)),
                  _resident(conv_w.shape), _resident((1, c))],
        out_specs=pl.BlockSpec((lt, c), lambda i: (i, 0)),
        out_shape=jax.ShapeDtypeStruct((total, c), F32),
        compiler_params=_params("parallel"),
        name="ssd_conv",
    )(xbc, xbc, xbc, conv_w, conv_b.reshape(1, c))


def _ssd_kernel(xs_ref, b_ref, c_ref, dt_ref, dtt_ref, alr_ref, alc_ref, dbr_ref, dbc_ref, dsk_ref, h0_ref,
                y_ref, fin_ref, st_ref, *, n_chunks, n_p, seq_p, seq_s):
    q = SSD_Q
    d = pl.program_id(0)
    ci = pl.program_id(1)
    blocks = D_INNER // LANES
    n_steps = n_chunks // SSD_STEP_CHUNKS
    tok = jnp.where(d == 0, ci, n_steps - 1 - ci) * (q * SSD_STEP_CHUNKS)
    in_p = tok < n_p
    pos = jnp.where(in_p, tok % seq_p, (tok - n_p) % seq_s)
    first = pos == 0
    last = pos + q * SSD_STEP_CHUNKS == jnp.where(in_p, seq_p, seq_s)
    start = jnp.where(d == 0, first, last)
    end = jnp.where(d == 0, last, first)

    @pl.when(jnp.logical_and(start, in_p))
    def _():
        st_ref[...] = jnp.zeros(st_ref.shape, F32)

    @pl.when(jnp.logical_and(start, jnp.logical_not(in_p)))
    def _():
        h0 = h0_ref[...].reshape(D_INNER, SSD_N)
        for kb in range(blocks):
            st_ref[:, LANES * kb:LANES * (kb + 1)] = h0[LANES * kb:LANES * (kb + 1), :].T

    skip_on = jnp.where(d == 0, 1.0, 0.0)
    sgn = 1 - 2 * d
    li = lax.broadcasted_iota(jnp.int32, (q, q), 0)
    si = lax.broadcasted_iota(jnp.int32, (q, q), 1)
    causal = (li - si) * sgn >= 0
    tri_l = jnp.where(causal, 1.0, 0.0)
    tri_r = jnp.where((si - li) * sgn >= 0, 1.0, 0.0)
    a_row = -jnp.exp(alr_ref[...])
    a_col = -jnp.exp(alc_ref[...])
    first_head = lax.broadcasted_iota(jnp.int32, (q, LANES), 1) < SSD_P
    first_head2 = lax.broadcasted_iota(jnp.int32, (q + SSD_N, LANES), 1) < SSD_P
    for k in range(SSD_STEP_CHUNKS):
        off = pl.multiple_of(jnp.where(d == 0, k, SSD_STEP_CHUNKS - 1 - k) * q, q)
        rows = pl.ds(off, q)
        _ssd_chunk(xs_ref, b_ref, c_ref, dt_ref, dtt_ref, dbr_ref, dbc_ref, dsk_ref, y_ref, st_ref, rows, d,
                   a_row, a_col, causal, tri_l, tri_r, skip_on, first_head, first_head2)

    @pl.when(jnp.logical_and(end, in_p))
    def _():
        for kb in range(blocks):
            t = st_ref[:, LANES * kb:LANES * (kb + 1)].T
            fin_ref[2 * kb:2 * kb + 2] = t.reshape(2, SSD_P, SSD_N)


def _ssd_chunk(xs_ref, b_ref, c_ref, dt_ref, dtt_ref, dbr_ref, dbc_ref, dsk_ref, y_ref, st_ref, rows, d,
               a_row, a_col, causal, tri_l, tri_r, skip_on, first_head, first_head2):
    q = SSD_Q
    heads_per_group = SSD_H // SSD_G
    gw = heads_per_group * SSD_P
    dt_both = dt_ref[rows, :]
    dt_col = _softplus(jnp.where(d == 0, dt_both[:, 0:SSD_H], dt_both[:, SSD_H:2 * SSD_H]) + dbr_ref[...])
    dt_row = _softplus(dtt_ref[:, rows] + dbc_ref[...])
    cs_col = _dot_hi(tri_l, dt_col * a_row)
    da_row = dt_row * a_col
    cs_row = _dot_hi(da_row, tri_r)
    tot = jnp.sum(da_row, axis=-1, keepdims=True)
    w_row = dt_row * jnp.exp(tot - cs_row)
    etot = jnp.exp(tot)
    src_row = cs_row - jnp.log(dt_row)
    w_row_bf = w_row.astype(BF)
    for g in range(SSD_G):
        bg = b_ref[rows, SSD_N * g:SSD_N * (g + 1)]
        cg = c_ref[rows, SSD_N * g:SSD_N * (g + 1)].astype(BF)
        cb = _dot_nt(cg, bg.astype(BF)).astype(BF)
        bgt = bg.T.astype(BF)
        c_state = _dot(cg, st_ref[:, gw * g:gw * (g + 1)].astype(BF))
        for jp in range(heads_per_group // 2):
            pi = g * (heads_per_group // 2) + jp
            cols = slice(LANES * pi, LANES * (pi + 1))
            xp = xs_ref[rows, cols]
            xb = xp.astype(BF)
            res, grow = [], []
            for hh in range(2):
                h = 2 * pi + hh
                csc = jnp.broadcast_to(cs_col[:, h:h + 1], (q, q))
                grow.append(jnp.exp(csc))
                m = jnp.exp(jnp.where(causal, csc - src_row[h:h + 1, :], -jnp.inf)).astype(BF) * cb
                bw = bgt * w_row_bf[h:h + 1, :]
                res.append(_dot(jnp.concatenate([m, bw], axis=0), xb))
            both = jnp.where(first_head2, res[0], res[1])
            keep = jnp.where(first_head, etot[2 * pi:2 * pi + 1, :], etot[2 * pi + 1:2 * pi + 2, :])
            st_ref[:, cols] = keep * st_ref[:, cols] + both[q:q + SSD_N, :]
            y_off = jnp.where(first_head, grow[0], grow[1]) * c_state[:, LANES * jp:LANES * (jp + 1)]
            y_ref[rows, cols] = (both[0:q, :] + y_off + (skip_on * dsk_ref[:, cols]) * xp).astype(y_ref.dtype)


def _ssd(xc, dt_raw, dtt_raw, dt_bias, a_log, d_skip, h0, n_p, seq_p, seq_s):
    q = SSD_Q * SSD_STEP_CHUNKS
    assert seq_p % q == 0 and seq_s % q == 0
    total = xc.shape[0]
    n_steps = total // q
    n_chunks = total // SSD_Q
    n_prompt = n_p // seq_p
    n_sample = h0.shape[0]

    def blk(d, c):
        return jnp.where(d == 0, c, n_steps - 1 - c)

    def h0_idx(d, c):
        return (jnp.clip((blk(d, c) * q - n_p) // seq_s, 0, n_sample - 1), d, 0, 0, 0)

    def fin_idx(d, c):
        return (jnp.minimum(blk(d, c) * q // seq_p, n_prompt - 1), d, 0, 0, 0)

    state_block = (None, None, SSD_H, SSD_P, SSD_N)
    return pl.pallas_call(
        functools.partial(_ssd_kernel, n_chunks=n_chunks, n_p=n_p, seq_p=seq_p, seq_s=seq_s),
        grid=(2, n_steps),
        in_specs=[pl.BlockSpec((q, D_INNER), lambda d, c: (blk(d, c), 0)),
                  pl.BlockSpec((q, SSD_GN), lambda d, c: (blk(d, c), D_INNER // SSD_GN)),
                  pl.BlockSpec((q, SSD_GN), lambda d, c: (blk(d, c), D_INNER // SSD_GN + 1)),
                  pl.BlockSpec((q, 2 * SSD_H), lambda d, c: (blk(d, c), 0)),
                  pl.BlockSpec((SSD_H, q), lambda d, c: (d, blk(d, c))),
                  pl.BlockSpec((None, 1, SSD_H), lambda d, c: (d, 0, 0)),
                  pl.BlockSpec((None, SSD_H, 1), lambda d, c: (d, 0, 0)),
                  pl.BlockSpec((None, 1, SSD_H), lambda d, c: (d, 0, 0)),
                  pl.BlockSpec((None, SSD_H, 1), lambda d, c: (d, 0, 0)),
                  _resident((1, D_INNER)),
                  pl.BlockSpec(state_block, h0_idx)],
        out_specs=[pl.BlockSpec((None, q, D_INNER), lambda d, c: (d, blk(d, c), 0)),
                   pl.BlockSpec(state_block, fin_idx)],
        out_shape=[jax.ShapeDtypeStruct((2, total, D_INNER), BF),
                   jax.ShapeDtypeStruct((n_prompt, 2, SSD_H, SSD_P, SSD_N), F32)],
        scratch_shapes=[pltpu.VMEM((SSD_N, D_INNER), F32)],
        compiler_params=_params("parallel", "arbitrary"),
        name="ssd_scan",
    )(xc, xc, xc, dt_raw, dtt_raw, a_log.reshape(2, 1, SSD_H), a_log.reshape(2, SSD_H, 1),
      dt_bias.reshape(2, 1, SSD_H), dt_bias.reshape(2, SSD_H, 1), d_skip, h0)


def _odd_out_kernel(x_ref, mod_ref, yf_ref, yb_ref, z_ref, ng_ref, wout_ref,
                    gf_ref, wg_ref, wu_ref, wd_ref, fg_ref, *o_refs, p_steps, final):
    y = (yf_ref[...].astype(F32) + yb_ref[...].astype(F32)) * _silu(z_ref[...].astype(F32))
    inv = lax.rsqrt(jnp.mean(y * y, axis=-1, keepdims=True) + EPS)
    x1 = x_ref[...] + (mod_ref[2:3, :] * inv) * _dot((y * ng_ref[...]).astype(BF), wout_ref[...])
    x2 = _ffn(x1, mod_ref, gf_ref, wg_ref, wu_ref, wd_ref)
    if not final:
        o_refs[0][...] = x2
        return
    out = _rms(x2, fg_ref[...])
    is_p = pl.program_id(0) < p_steps

    @pl.when(is_p)
    def _():
        o_refs[0][...] = out

    @pl.when(jnp.logical_not(is_p))
    def _():
        o_refs[1][...] = out


def _odd_out(st, x, mods, y, z, norm_g, w_out, g_ffn, wg, wu, wd, layer, final_g, final):
    ydir = lambda d: pl.BlockSpec((None, st.tm, D_INNER), lambda i: (d, i, 0))
    return pl.pallas_call(
        functools.partial(_odd_out_kernel, p_steps=st.p_steps, final=final),
        grid=(st.steps,),
        in_specs=[st.row_spec(D_MODEL), st.mod_spec(), ydir(0), ydir(1), st.row_spec(D_INNER),
                  _resident((1, D_INNER)), _resident(w_out.shape),
                  _resident((1, D_MODEL)), _layer_resident(wg.shape, layer), _layer_resident(wu.shape, layer),
                  _layer_resident(wd.shape, layer), _resident((1, D_MODEL))],
        out_specs=[st.prompt_spec(D_MODEL), st.sample_spec(D_MODEL)] if final else st.row_spec(D_MODEL),
        out_shape=([jax.ShapeDtypeStruct((st.n_p, D_MODEL), F32),
                    jax.ShapeDtypeStruct((st.total - st.n_p, D_MODEL), F32)] if final
                   else jax.ShapeDtypeStruct((st.total, D_MODEL), F32)),
        compiler_params=_params("arbitrary"),
        name="odd_out_ffn",
    )(x, mods, y, y, z, norm_g, w_out, g_ffn, wg, wu, wd, final_g)


def kernel(x_prompt, x_sample, cache_na_k, cache_na_v, state_s5_re, state_s5_im, state_ssd, c, c_ctx, norm_mix_g, norm_ffn_g, ada_w, ada_b, ffn_w_gate, ffn_w_up, ffn_w_down, ev_w_in, ev_w_out, s5_a_re, s5_a_im, s5_log_dt, s5_b_re, s5_b_im, s5_c_re, s5_c_im, s5_d, s5_w_glu, na_rpb, od_w_in, od_conv_w, od_conv_b, ssd_a_log, ssd_dt_bias, ssd_d, ssd_norm_g, od_w_out, final_norm_g):
    bp, seq_p, d = x_prompt.shape
    bs, seq_s, _ = x_sample.shape
    depth = ada_w.shape[0]
    n_p = bp * seq_p
    assert d == D_MODEL and n_p % seq_s == 0
    st = _Stream(n_p, seq_s, bs, tm=math.gcd(512, math.gcd(n_p, seq_s)))

    xp = x_prompt.reshape(n_p, d)
    xs = x_sample.reshape(bs * seq_s, d)
    cond = jnp.concatenate([c_ctx[None, :], c, jnp.zeros((8 - 1 - bs, d), F32)], axis=0)
    mods = _ada(cond, ada_w, ada_b).reshape(depth, 8, ADA_CHUNKS, d)
    row = lambda t: t.reshape(1, -1)

    new_k, new_v, new_s5_re, new_s5_im, new_ssd = [], [], [], [], []
    wg, wu, wd = ffn_w_gate.astype(BF), ffn_w_up.astype(BF), ffn_w_down.astype(BF)
    for layer in range(depth):
        g_mix = row(norm_mix_g[layer])
        g_ffn = row(norm_ffn_g[layer])
        if layer % 2 == 0:
            e = layer // 2
            if layer > 0:
                xp, xs = x[:n_p], x[n_p:]
            u, q, k, v = _even_in(st, xp, xs, mods[layer], g_mix, ev_w_in[e].astype(BF))
            attn_p = _ctx_attn(q, k, v, bp, seq_p)
            heads = lambda t: t[:n_p].reshape(bp, seq_p, NA_HEADS, NA_HD).transpose(0, 2, 1, 3)
            new_k.append(heads(k))
            new_v.append(heads(v))
            ctx = lambda t: t[:, e].transpose(0, 2, 1, 3).reshape(bs, -1, D_NA).astype(BF)
            attn_s = _na_attn(q, k, v, ctx(cache_na_k), ctx(cache_na_v), _bias_table(na_rpb[e]),
                              bs, seq_s, n_p // seq_s)
            v_op, bw, cw, sc = _s5_prep(s5_a_re[e], s5_a_im[e], s5_log_dt[e], s5_b_re[e], s5_b_im[e],
                                        s5_c_re[e], s5_c_im[e])
            y, fin = _s5(u, v_op, bw, cw, sc, _state_planes(state_s5_re[:, e], state_s5_im[:, e]),
                         n_p, seq_p, seq_s)
            fre, fim = _planes_state(fin)
            new_s5_re.append(fre)
            new_s5_im.append(fim)
            x = _even_out(st, xp, xs, mods[layer], y, u, attn_p, attn_s, row(s5_d[e]),
                          s5_w_glu[e].astype(BF), ev_w_out[e].astype(BF), g_ffn, wg, wu, wd, layer)
        else:
            o = layer // 2
            z, xbc, dt_raw, dtt_raw = _odd_in(st, x, mods[layer], g_mix, od_w_in[o].astype(BF))
            xc = _conv(xbc, od_conv_w[o], od_conv_b[o], n_p, seq_p, seq_s)
            y, fin = _ssd(xc, dt_raw, dtt_raw, ssd_dt_bias[o], ssd_a_log[o], row(jnp.repeat(ssd_d[o], SSD_P)),
                          state_ssd[:, o], n_p, seq_p, seq_s)
            new_ssd.append(fin)
            x = _odd_out(st, x, mods[layer], y, z, row(ssd_norm_g[o]), od_w_out[o].astype(BF),
                         g_ffn, wg, wu, wd, layer, row(final_norm_g), layer == depth - 1)
    if depth % 2 == 1:
        raise NotImplementedError("final norm is fused into the last (odd) layer")
    y_prompt = x[0].reshape(bp, seq_p, d)
    y_sample = x[1].reshape(bs, seq_s, d)
    return (y_prompt, y_sample, jnp.stack(new_k, axis=1), jnp.stack(new_v, axis=1),
            jnp.stack(new_s5_re, axis=1), jnp.stack(new_s5_im, axis=1), jnp.stack(new_ssd, axis=1))
```

```python
import functools
import math

import jax
import jax.numpy as jnp
from jax import lax
from jax.experimental import pallas as pl
from jax.experimental.pallas import tpu as pltpu

F32 = jnp.float32
BF = jnp.bfloat16
HI = lax.Precision.HIGHEST

D_MODEL = 1024
EPS = 1e-6
ADA_CHUNKS = 6
GRID_W = 64
D_S5 = 512
S5_P = 16
S5_G = D_S5 // S5_P
S5_N = 64
S5_MIN_DECAY = 1e-4
S5_Q = 16
D_NA = 512
NA_HD = 64
NA_HEADS = D_NA // NA_HD
NA_WIN_ROWS = 8
NA_WIN_COLS = 16
NA_ROWS = 4
D_INNER = 2048
SSD_P = 64
SSD_H = D_INNER // SSD_P
SSD_G = 4
SSD_N = 128
SSD_Q = 128
SSD_STEP_CHUNKS = 2
SSD_CONV = 5
SSD_GN = SSD_G * SSD_N
SSD_CONV_DIM = D_INNER + 2 * SSD_GN
D_FF = 2816

V7X_VMEM_BYTES = 64 * 1024 * 1024
VMEM_LIMIT = V7X_VMEM_BYTES - 4 * 1024 * 1024
LANES = 128
NEG_BIG = -1e30


def _params(*sem):
    return pltpu.CompilerParams(dimension_semantics=sem, vmem_limit_bytes=VMEM_LIMIT)


def _resident(shape):
    nd = len(shape)
    return pl.BlockSpec(shape, lambda *_: (0,) * nd, pipeline_mode=pl.Buffered(1))


def _layer_resident(shape, layer):
    nd = len(shape)
    return pl.BlockSpec((None,) + tuple(shape[1:]), lambda *_: (layer,) + (0,) * (nd - 1), pipeline_mode=pl.Buffered(1))


def _dot(a, b):
    return jnp.dot(a, b, preferred_element_type=F32)


def _dot_nt(a, b):
    return lax.dot_general(a, b, (((1,), (1,)), ((), ())), preferred_element_type=F32)


def _dot_hi(a, b):
    return jnp.dot(a, b, preferred_element_type=F32, precision=HI)


def _sigmoid(x):
    return 1.0 / (1.0 + jnp.exp(-x))


def _silu(x):
    return x * _sigmoid(x)


def _softplus(x):
    return jnp.maximum(x, 0.0) + jnp.log1p(jnp.exp(-jnp.abs(x)))


def _rms(x, g):
    return x * lax.rsqrt(jnp.mean(x * x, axis=-1, keepdims=True) + EPS) * g


def _norm_mod(x, g, shift, scale):
    return _rms(x, g) * (1.0 + scale) + shift


def _ada_kernel(c_ref, w_ref, b_ref, o_ref):
    c = c_ref[...]
    o_ref[...] = _dot(_silu(c).astype(BF), w_ref[...].astype(BF)) + b_ref[...]


def _ada(cond, ada_w, ada_b):
    depth, d, n = ada_w.shape
    tn = 1536
    return pl.pallas_call(
        _ada_kernel,
        grid=(depth, n // tn),
        in_specs=[pl.BlockSpec((8, d), lambda l, j: (0, 0)),
                  pl.BlockSpec((None, d, tn), lambda l, j: (l, 0, j)),
                  pl.BlockSpec((None, 1, tn), lambda l, j: (l, 0, j))],
        out_specs=pl.BlockSpec((None, 8, tn), lambda l, j: (l, 0, j)),
        out_shape=jax.ShapeDtypeStruct((depth, 8, n), F32),
        compiler_params=_params("parallel", "parallel"),
        name="adaln",
    )(cond, ada_w, ada_b.reshape(depth, 1, n))


class _Stream:
    def __init__(self, n_p, len_s, n_s, tm):
        assert n_p % tm == 0 and len_s % tm == 0
        self.n_p, self.len_s, self.n_s, self.tm = n_p, len_s, n_s, tm
        self.total = n_p + len_s * n_s
        self.steps = self.total // tm
        self.p_steps = n_p // tm

    def group(self, i):
        t = i * self.tm
        return jnp.where(t < self.n_p, 0, 1 + (t - self.n_p) // self.len_s)

    def mod_spec(self):
        return pl.BlockSpec((None, ADA_CHUNKS, D_MODEL), lambda i: (self.group(i), 0, 0))

    def row_spec(self, width, col=0):
        return pl.BlockSpec((self.tm, width), lambda i: (i, col))

    def prompt_spec(self, width):
        return pl.BlockSpec((self.tm, width), lambda i: (jnp.minimum(i, self.p_steps - 1), 0))

    def sample_spec(self, width):
        return pl.BlockSpec((self.tm, width), lambda i: (jnp.maximum(i - self.p_steps, 0), 0))


def _ffn(x1, mod_ref, g_ref, wg_ref, wu_ref, wd_ref):
    h = _norm_mod(x1, g_ref[...], mod_ref[3:4, :], mod_ref[4:5, :]).astype(BF)
    hid = (_silu(_dot(h, wg_ref[...])) * _dot(h, wu_ref[...])).astype(BF)
    return x1 + mod_ref[5:6, :] * _dot(hid, wd_ref[...])


def _even_in_kernel(xp_ref, xs_ref, mod_ref, g_ref, w_ref, u_ref, q_ref, k_ref, v_ref, *, p_steps):
    x = jnp.where(pl.program_id(0) < p_steps, xp_ref[...], xs_ref[...])
    h = _norm_mod(x, g_ref[...], mod_ref[0:1, :], mod_ref[1:2, :])
    r = _dot(h.astype(BF), w_ref[...])
    u_ref[...] = r[:, 0:D_S5]
    q_ref[...] = r[:, D_S5:D_S5 + D_NA] * (NA_HD ** -0.5)
    k_ref[...] = r[:, D_S5 + D_NA:D_S5 + 2 * D_NA]
    v_ref[...] = r[:, D_S5 + 2 * D_NA:D_S5 + 3 * D_NA]


def _even_in(st, xp, xs, mods, g, w_in):
    n_out = w_in.shape[1]
    out = jax.ShapeDtypeStruct((st.total, D_S5), F32)
    return pl.pallas_call(
        functools.partial(_even_in_kernel, p_steps=st.p_steps),
        grid=(st.steps,),
        in_specs=[st.prompt_spec(D_MODEL), st.sample_spec(D_MODEL), st.mod_spec(), _resident((1, D_MODEL)),
                  _resident((D_MODEL, n_out))],
        out_specs=[st.row_spec(D_S5)] * 4,
        out_shape=[out] * 4,
        compiler_params=_params("parallel"),
        name="even_in",
    )(xp, xs, mods, g, w_in)


def _ctx_attn_kernel(q_ref, k_ref, v_ref, o_ref):
    lane = lax.broadcasted_iota(jnp.int32, (q_ref.shape[0], LANES), 1)
    for pair in range(D_NA // LANES):
        cols = slice(LANES * pair, LANES * (pair + 1))
        q = q_ref[:, cols]
        k = k_ref[:, cols].astype(BF)
        v = v_ref[:, cols].astype(BF)
        outs = []
        for h in range(2):
            in_head = (lane >= NA_HD * h) & (lane < NA_HD * (h + 1))
            qh = jnp.where(in_head, q, 0.0).astype(BF)
            s = _dot_nt(qh, k)
            p = jnp.exp(s - jnp.max(s, axis=-1, keepdims=True))
            l = jnp.sum(p, axis=-1, keepdims=True)
            outs.append(_dot(p.astype(BF), v) / l)
        o_ref[:, cols] = jnp.where(lane < NA_HD, outs[0], outs[1])


def _ctx_attn(q, k, v, n_seq, seq):
    spec = pl.BlockSpec((seq, D_NA), lambda b: (b, 0))
    return pl.pallas_call(
        _ctx_attn_kernel,
        grid=(n_seq,),
        in_specs=[spec, spec, spec],
        out_specs=spec,
        out_shape=jax.ShapeDtypeStruct((n_seq * seq, D_NA), F32),
        compiler_params=_params("parallel"),
        name="ctx_attn",
    )(q, k, v)


NA_N_DR = 2 * NA_WIN_ROWS - 1
NA_BOTH, NA_LEFT, NA_RIGHT = 0, NA_N_DR - 1, 2 * NA_N_DR - 1
NA_NONE = 3 * NA_N_DR - 1
NA_UNION = NA_WIN_ROWS + NA_ROWS


def _bias_kernel(rpb_ref, o_ref):
    half = LANES // 2
    wq = lax.broadcasted_iota(jnp.int32, (GRID_W, LANES), 0)
    lane = lax.broadcasted_iota(jnp.int32, (GRID_W, LANES), 1)
    wk = lane & (GRID_W - 1)
    left = lane < GRID_W
    col_start = jnp.clip(wq - NA_WIN_COLS // 2, 0, GRID_W - NA_WIN_COLS)
    ok = (wk >= col_start) & (wk < col_start + NA_WIN_COLS)
    dc = jnp.clip(wk - wq, -(NA_WIN_COLS - 1), NA_WIN_COLS - 1) + (NA_WIN_COLS - 1)
    idx = jnp.where(left, dc, dc + half)
    low_lanes = lax.broadcasted_iota(jnp.int32, (1, LANES), 1) < half

    def entry(e, carry):
        is_both = e < NA_LEFT
        is_left = jnp.logical_and(e >= NA_LEFT, e < NA_RIGHT)
        is_right = jnp.logical_and(e >= NA_RIGHT, e < NA_NONE)
        d_left = jnp.where(is_both, e, jnp.where(is_left, e - NA_LEFT, 0))
        d_right = jnp.where(is_both, e + 1, jnp.where(is_right, e - NA_RIGHT, 0))
        left_on = jnp.where(jnp.logical_or(is_both, is_left), 1, 0)
        right_on = jnp.where(jnp.logical_or(is_both, is_right), 1, 0)
        vals = jnp.where(low_lanes, rpb_ref[pl.ds(d_left, 1), :], pltpu.roll(rpb_ref[pl.ds(d_right, 1), :], half, 1))
        tile = jnp.take_along_axis(jnp.broadcast_to(vals, (GRID_W, LANES)), idx, axis=1)
        side_on = jnp.where(left, left_on, right_on) > 0
        o_ref[e] = jnp.where(ok & side_on, tile, NEG_BIG)
        return carry

    lax.fori_loop(0, NA_NONE + 1, entry, 0, unroll=9)


def _bias_table(rpb):
    n_dc = rpb.shape[-1]
    padded = jnp.pad(rpb, ((0, 0), (0, 0), (0, LANES - n_dc)))
    return pl.pallas_call(
        _bias_kernel,
        grid=(NA_HEADS,),
        in_specs=[pl.BlockSpec((None, NA_N_DR, LANES), lambda h: (h, 0, 0))],
        out_specs=pl.BlockSpec((None, NA_NONE + 1, GRID_W, LANES), lambda h: (h, 0, 0, 0)),
        out_shape=jax.ShapeDtypeStruct((NA_HEADS, NA_NONE + 1, GRID_W, LANES), F32),
        compiler_params=_params("parallel"),
        name="na_bias",
    )(padded)


def _na_kernel(q_ref, k_ref, v_ref, ck_ref, cv_ref, bias_ref, o_ref, kb_ref, vb_ref, *, rows):
    kb_ref[...] = k_ref[...].astype(BF)
    vb_ref[...] = v_ref[...].astype(BF)
    ck = ck_ref[...]
    cv = cv_ref[...]
    kh = NA_WIN_ROWS
    nq = NA_ROWS * GRID_W
    lane = lax.broadcasted_iota(jnp.int32, (nq, LANES), 1)

    def body(g, carry):
        r0 = g * NA_ROWS
        first = jnp.clip(r0 - kh // 2, 0, rows - NA_UNION)
        qoff = pl.multiple_of(r0 * GRID_W, nq)
        q = q_ref[pl.ds(qoff, nq), :]
        koff = pl.multiple_of(first * GRID_W, GRID_W)
        kw = kb_ref[pl.ds(koff, NA_UNION * GRID_W), :]
        vw = vb_ref[pl.ds(koff, NA_UNION * GRID_W), :]
        tile_idx = []
        for j in range(NA_ROWS):
            r = r0 + j
            start = jnp.clip(r - kh // 2, 0, rows - kh)
            for ip in range(NA_UNION // 2):
                k0 = first + 2 * ip
                in0 = jnp.logical_and(k0 >= start, k0 < start + kh)
                in1 = jnp.logical_and(k0 + 1 >= start, k0 + 1 < start + kh)
                dr0 = k0 - r + (NA_WIN_ROWS - 1)
                idx = jnp.where(jnp.logical_and(in0, in1), NA_BOTH + dr0,
                                jnp.where(in0, NA_LEFT + dr0, jnp.where(in1, NA_RIGHT + dr0 + 1, NA_NONE)))
                tile_idx.append(jnp.clip(idx, 0, NA_NONE))
        outs = []
        for h in range(2):
            in_head = (lane >= NA_HD * h) & (lane < NA_HD * (h + 1))
            qh = jnp.where(in_head, q, 0.0).astype(BF)
            per_row = NA_UNION // 2
            bias = jnp.concatenate(
                [jnp.concatenate([bias_ref[h, tile_idx[j * per_row + ip]] for ip in range(per_row)], axis=1)
                 for j in range(NA_ROWS)], axis=0)
            s_loc = _dot_nt(qh, kw) + bias
            s_ctx = _dot_nt(qh, ck)
            m = jnp.maximum(jnp.max(s_loc, axis=-1, keepdims=True), jnp.max(s_ctx, axis=-1, keepdims=True))
            p_loc = jnp.exp(s_loc - m)
            p_ctx = jnp.exp(s_ctx - m)
            l = jnp.sum(p_loc, axis=-1, keepdims=True) + jnp.sum(p_ctx, axis=-1, keepdims=True)
            outs.append((_dot(p_loc.astype(BF), vw) + _dot(p_ctx.astype(BF), cv)) / l)
        o_ref[pl.ds(qoff, nq), :] = jnp.where(lane < NA_HD, outs[0], outs[1])
        return carry

    lax.fori_loop(0, rows // NA_ROWS, body, 0, unroll=2)


def _na_attn(q, k, v, ck, cv, bias, n_seq, seq, row_base):
    rows = seq // GRID_W
    assert rows >= NA_UNION and rows % NA_ROWS == 0
    past = ck.shape[1]
    spec = pl.BlockSpec((seq, LANES), lambda b, p: (row_base + b, p))
    cspec = pl.BlockSpec((None, past, LANES), lambda b, p: (b, 0, p))
    return pl.pallas_call(
        functools.partial(_na_kernel, rows=rows),
        grid=(n_seq, D_NA // LANES),
        in_specs=[spec, spec, spec, cspec, cspec,
                  pl.BlockSpec((2, NA_NONE + 1, GRID_W, LANES), lambda b, p: (p, 0, 0, 0))],
        out_specs=pl.BlockSpec((seq, LANES), lambda b, p: (b, p)),
        out_shape=jax.ShapeDtypeStruct((n_seq * seq, D_NA), F32),
        scratch_shapes=[pltpu.VMEM((seq, LANES), BF), pltpu.VMEM((seq, LANES), BF)],
        compiler_params=_params("parallel", "parallel"),
        name="na_attn",
    )(q, k, v, ck, cv, bias)


S5_BLK = LANES // S5_P
S5_NB = S5_G // S5_BLK
S5_ST = S5_BLK * S5_N
S5_W = S5_Q * LANES


def _cexp(zr, zi):
    e = jnp.exp(zr)
    return e * jnp.cos(zi), e * jnp.sin(zi)


def _cmul(ar, ai, br, bi):
    return ar * br - ai * bi, ar * bi + ai * br


def _s5_prep_kernel(arr_ref, ari_ref, ldr_ref, acr_ref, aci_ref, ldc_ref, btr_ref, bti_ref, ctr_ref, cti_ref,
                    v_ref, bw_ref, cw_ref, sc_ref,
                    bbr_s, bbi_s, ccr_s, cci_s, pwr_s, pwi_s, pcr_s, pci_s, p0_s, p1_s):
    t = pl.program_id(1)
    q = S5_Q
    nst = S5_ST

    @pl.when(t == 0)
    def _():
        expand = jnp.where(lax.broadcasted_iota(jnp.int32, (S5_P, 2 * LANES), 0)
                           == (lax.broadcasted_iota(jnp.int32, (S5_P, 2 * LANES), 1) & (S5_P - 1)), 1.0, 0.0)
        row_g = lax.broadcasted_iota(jnp.int32, (LANES, nst), 0) >> 4
        col_g = lax.broadcasted_iota(jnp.int32, (LANES, nst), 1) >> 6
        st_g = lax.broadcasted_iota(jnp.int32, (nst, LANES), 0) >> 6
        ch_g = lax.broadcasted_iota(jnp.int32, (nst, LANES), 1) >> 4
        lane_d = lax.broadcasted_iota(jnp.int32, (nst, LANES), 1) >> 4
        taps = []
        for d in range(2):
            ar = jnp.minimum(arr_ref[d], -S5_MIN_DECAY)
            ai = ari_ref[d]
            dt = jnp.exp(ldr_ref[d])
            abr, abi = _cexp(ar * dt, ai * dt)
            den = ar * ar + ai * ai
            cfr = ((abr - 1.0) * ar + abi * ai) / den
            cfi = (abi * ar - (abr - 1.0) * ai) / den
            bbr, bbi = _cmul(cfr, cfi, btr_ref[d], bti_ref[d])
            bbr = jnp.where(row_g == col_g, jnp.concatenate([bbr] * S5_BLK, axis=0), 0.0)
            bbi = jnp.where(row_g == col_g, jnp.concatenate([bbi] * S5_BLK, axis=0), 0.0)
            bbr_s[d] = bbr
            bbi_s[d] = bbi
            pr = jnp.ones((1, nst), F32)
            pi = jnp.zeros((1, nst), F32)
            for e in range(q + 1):
                pwr_s[d, e] = pr
                pwi_s[d, e] = pi
                pr, pi = _cmul(pr, pi, abr, abi)
            sc_ref[2 * d:2 * d + 1, :] = pwr_s[d, q]
            sc_ref[2 * d + 1:2 * d + 2, :] = pwi_s[d, q]
            arc = jnp.minimum(acr_ref[d], -S5_MIN_DECAY)
            dtc = jnp.exp(ldc_ref[d])
            acr, aci = _cexp(jnp.broadcast_to(arc * dtc, (nst, LANES)), jnp.broadcast_to(aci_ref[d] * dtc, (nst, LANES)))
            pr = jnp.ones((nst, LANES), F32)
            pi = jnp.zeros((nst, LANES), F32)
            for e in range(q + 1):
                pcr_s[d, e] = pr
                pci_s[d, e] = pi
                pr, pi = _cmul(pr, pi, acr, aci)
            cxr = _dot_hi(ctr_ref[d], expand)
            cxi = _dot_hi(cti_ref[d], expand)
            ccr_s[d] = jnp.where(st_g == ch_g, cxr[:, 0:LANES], 0.0)
            cci_s[d] = jnp.where(st_g == ch_g, cxi[:, 0:LANES], 0.0)
            pws_r, pws_i = [], []
            for k in range(2):
                sel_r = jnp.zeros((nst, LANES), F32)
                sel_i = jnp.zeros((nst, LANES), F32)
                for j in range(S5_BLK):
                    dl = S5_BLK * k + j
                    e = dl if d == 0 else q - 1 - dl
                    sel_r = jnp.where(lane_d == j, pcr_s[d, e], sel_r)
                    sel_i = jnp.where(lane_d == j, pci_s[d, e], sel_i)
                pws_r.append(sel_r)
                pws_i.append(sel_i)
            ggr, ggi = _cmul(cxr, cxi, jnp.concatenate(pws_r, axis=1), jnp.concatenate(pws_i, axis=1))
            taps.append(_dot_hi(bbr, ggr) - _dot_hi(bbi, ggi))
        lane2 = lax.broadcasted_iota(jnp.int32, (LANES, 2 * LANES), 1)
        kt0 = taps[0] + jnp.where(lane2 < S5_P, pltpu.roll(taps[1], S5_P, 1), 0.0)
        kt1 = jnp.where(lane2 >= 2 * LANES - S5_P, 0.0, taps[1])
        er = lax.broadcasted_iota(jnp.int32, (2 * LANES, S5_W), 0)
        ec = lax.broadcasted_iota(jnp.int32, (2 * LANES, S5_W), 1)
        place = jnp.where(((er >> 4) == (ec >> 7)) & ((er & (S5_P - 1)) == (ec & (S5_P - 1))), 1.0, 0.0).astype(BF)
        own = (lax.broadcasted_iota(jnp.int32, (LANES, S5_W), 0) >> 4) == (
            (lax.broadcasted_iota(jnp.int32, (LANES, S5_W), 1) >> 4) & (S5_BLK - 1))
        zeros = jnp.zeros((LANES, S5_W), BF)
        p0_s[:, 0:S5_W] = zeros
        p0_s[:, S5_W:2 * S5_W] = jnp.where(own, _dot(kt0.astype(BF), place), 0.0).astype(BF)
        p1_s[:, 0:S5_W] = jnp.where(own, _dot(kt1.astype(BF), place), 0.0).astype(BF)
        p1_s[:, S5_W:2 * S5_W] = zeros

    off0 = pl.multiple_of(S5_W - LANES * t, LANES)
    off1 = pl.multiple_of(LANES * (q - 1 - t), LANES)
    v_ref[...] = p0_s[:, pl.ds(off0, S5_W)] + p1_s[:, pl.ds(off1, S5_W)]
    for d in range(2):
        e_b = (q - 1 - t) if d == 0 else t
        br, bi = _cmul(pwr_s[d, e_b], pwi_s[d, e_b], bbr_s[d], bbi_s[d])
        bw_ref[:, 2 * nst * d:2 * nst * d + nst] = br.astype(BF)
        bw_ref[:, 2 * nst * d + nst:2 * nst * (d + 1)] = bi.astype(BF)
        e_c = (t + 1) if d == 0 else (q - t)
        gr, gi = _cmul(ccr_s[d], cci_s[d], pcr_s[d, e_c], pci_s[d, e_c])
        cw_ref[2 * nst * d:2 * nst * d + nst, :] = gr.astype(BF)
        cw_ref[2 * nst * d + nst:2 * nst * (d + 1), :] = (-gi).astype(BF)


def _s5_prep(a_re, a_im, log_dt, b_re, b_im, c_re, c_im):
    nst, nb, q = S5_ST, S5_NB, S5_Q
    row = lambda t: t.reshape(2, nb, 1, nst)
    col = lambda t: t.reshape(2, nb, nst, 1)
    ld = jnp.broadcast_to(log_dt[:, :, None], (2, S5_G, S5_N))
    bt = lambda t: t.reshape(2, nb, S5_BLK, S5_N, S5_P).transpose(0, 1, 4, 2, 3).reshape(2, nb, S5_P, nst)
    ct = lambda t: t.reshape(2, nb, S5_BLK, S5_P, S5_N).transpose(0, 1, 2, 4, 3).reshape(2, nb, nst, S5_P)
    rspec = pl.BlockSpec((2, None, 1, nst), lambda b, t: (0, b, 0, 0))
    cspec = pl.BlockSpec((2, None, nst, 1), lambda b, t: (0, b, 0, 0))
    btspec = pl.BlockSpec((2, None, S5_P, nst), lambda b, t: (0, b, 0, 0))
    ctspec = pl.BlockSpec((2, None, nst, S5_P), lambda b, t: (0, b, 0, 0))
    big = jax.ShapeDtypeStruct((nb, S5_W, S5_W), BF)
    return pl.pallas_call(
        _s5_prep_kernel,
        grid=(nb, q),
        in_specs=[rspec, rspec, rspec, cspec, cspec, cspec, btspec, btspec, ctspec, ctspec],
        out_specs=[pl.BlockSpec((None, LANES, S5_W), lambda b, t: (b, t, 0)),
                   pl.BlockSpec((None, LANES, S5_W), lambda b, t: (b, t, 0)),
                   pl.BlockSpec((None, S5_W, LANES), lambda b, t: (b, 0, t)),
                   pl.BlockSpec((None, 4, nst), lambda b, t: (b, 0, 0))],
        out_shape=[big, big, big, jax.ShapeDtypeStruct((nb, 4, nst), F32)],
        scratch_shapes=[pltpu.VMEM((2, LANES, nst), F32), pltpu.VMEM((2, LANES, nst), F32),
                        pltpu.VMEM((2, nst, LANES), F32), pltpu.VMEM((2, nst, LANES), F32),
                        pltpu.VMEM((2, q + 1, 1, nst), F32), pltpu.VMEM((2, q + 1, 1, nst), F32),
                        pltpu.VMEM((2, q + 1, nst, LANES), F32), pltpu.VMEM((2, q + 1, nst, LANES), F32),
                        pltpu.VMEM((LANES, 2 * S5_W), BF), pltpu.VMEM((LANES, 2 * S5_W), BF)],
        compiler_params=_params("parallel", "arbitrary"),
        name="s5_prep",
    )(row(a_re), row(a_im), row(ld), col(a_re), col(a_im), col(ld), bt(b_re), bt(b_im), ct(c_re), ct(c_im))


def _s5_kernel(u_ref, v_ref, bw_ref, cw_ref, sc_ref, h0_ref, y_ref, fin_ref, z_s, h_s, *, p_steps, n_sub, mp):
    s = pl.program_id(1)
    m = u_ref.shape[0] // S5_Q
    n_tiles = 4 * S5_ST // LANES
    per = S5_ST // LANES
    half = 2 * per
    chunk_rows = lambda t: pl.ds(t, m, stride=S5_Q)
    tile_rows = lambda k: pl.ds(k, m, stride=n_tiles)
    ucat = jnp.concatenate([u_ref[chunk_rows(t), :].astype(BF) for t in range(S5_Q)], axis=1)
    z = _dot(ucat, bw_ref[...])
    for k in range(n_tiles):
        z_s[tile_rows(k), :] = z[:, LANES * k:LANES * (k + 1)]
    sc = sc_ref[...]
    stack = lambda row: jnp.concatenate([row[:, LANES * j:LANES * (j + 1)] for j in range(per)], axis=0)

    def multipliers(d):
        ar, ai = stack(sc[2 * d:2 * d + 1]), stack(sc[2 * d + 1:2 * d + 2])
        return jnp.concatenate([ar, ar], axis=0), jnp.concatenate([-ai, ai], axis=0)

    a1f, a2f = multipliers(0)
    a1b, a2b = multipliers(1)

    def advance(hf, hb, cf, cb):
        rf = pl.ds(pl.multiple_of(cf * n_tiles, half), half)
        rb = pl.ds(pl.multiple_of(cb * n_tiles + half, half), half)
        h_s[rf, :] = hf
        h_s[rb, :] = hb
        hf = a1f * hf + a2f * pltpu.roll(hf, per, 0) + z_s[rf, :]
        hb = a1b * hb + a2b * pltpu.roll(hb, per, 0) + z_s[rb, :]
        return hf, hb

    unstack = lambda h4: jnp.concatenate([h4[j:j + 1, :] for j in range(per)], axis=1)

    @pl.when(s < p_steps)
    def _():
        for i in range(n_sub):
            hf = jnp.zeros((half, LANES), F32)
            hb = jnp.zeros((half, LANES), F32)
            for c in range(mp):
                hf, hb = advance(hf, hb, i * mp + c, i * mp + mp - 1 - c)
            for pi, h4 in enumerate((hf[0:per], hf[per:half], hb[0:per], hb[per:half])):
                fin_ref[i, pi:pi + 1, :] = unstack(h4)

    @pl.when(s >= p_steps)
    def _():
        h0 = h0_ref[...]
        init = (jnp.concatenate([stack(h0[0:1]), stack(h0[1:2])], axis=0),
                jnp.concatenate([stack(h0[2:3]), stack(h0[3:4])], axis=0))
        lax.fori_loop(0, m, lambda c, h: advance(h[0], h[1], c, m - 1 - c), init)

    hcat = jnp.concatenate([h_s[tile_rows(k), :] for k in range(n_tiles)], axis=1).astype(BF)
    ycat = _dot(ucat, v_ref[...]) + _dot(hcat, cw_ref[...])
    for t in range(S5_Q):
        y_ref[chunk_rows(t), :] = ycat[:, LANES * t:LANES * (t + 1)]


def _s5(u, v, bw, cw, sc, h0, n_p, seq_p, seq_s):
    total = u.shape[0]
    q = S5_Q
    m = seq_s // q
    mp = seq_p // q
    n_sub = seq_s // seq_p
    p_steps = n_p // seq_s
    n_prompt = n_p // seq_p
    wspec = pl.BlockSpec((None, S5_W, S5_W), lambda b, s: (b, 0, 0), pipeline_mode=pl.Buffered(1))
    tok = pl.BlockSpec((seq_s, LANES), lambda b, s: (s, b))
    y, fin = pl.pallas_call(
        functools.partial(_s5_kernel, p_steps=p_steps, n_sub=n_sub, mp=mp),
        grid=(S5_NB, total // seq_s),
        in_specs=[tok, wspec, wspec, wspec,
                  pl.BlockSpec((None, 4, S5_ST), lambda b, s: (b, 0, 0)),
                  pl.BlockSpec((None, 4, S5_ST), lambda b, s: (jnp.maximum(s - p_steps, 0), 0, b))],
        out_specs=[tok, pl.BlockSpec((n_sub, 4, S5_ST), lambda b, s: (jnp.minimum(s, p_steps - 1), 0, b))],
        out_shape=[jax.ShapeDtypeStruct((total, D_S5), F32),
                   jax.ShapeDtypeStruct((n_prompt, 4, S5_G * S5_N), F32)],
        scratch_shapes=[pltpu.VMEM((4 * S5_ST // LANES * m, LANES), F32)] * 2,
        compiler_params=_params("parallel", "arbitrary"),
        name="s5_scan",
    )(u, v, bw, cw, sc, h0)
    return y, fin


def _state_planes(re, im):
    b = re.shape[0]
    return jnp.stack([t[:, d].reshape(b, S5_G * S5_N) for d in range(2) for t in (re, im)], axis=1)


def _planes_state(fin):
    t = fin.reshape(fin.shape[0], 2, 2, S5_G, S5_N)
    return t[:, :, 0], t[:, :, 1]


def _gelu_tanh(x):
    return 0.5 * x * (1.0 + jnp.tanh(math.sqrt(2.0 / math.pi) * (x + 0.044715 * (x * x * x))))


def _even_out_kernel(xp_ref, xs_ref, mod_ref, y_ref, u_ref, ap_ref, as_ref, dsk_ref, wglu_ref, wout_ref,
                     gf_ref, wg_ref, wu_ref, wd_ref, o_ref, *, p_steps):
    is_p = pl.program_id(0) < p_steps
    attn = jnp.where(is_p, ap_ref[...], as_ref[...])
    y = _gelu_tanh(y_ref[...] + u_ref[...] * dsk_ref[...])
    y = y * _sigmoid(_dot(y.astype(BF), wglu_ref[...]))
    out = _dot(y.astype(BF), wout_ref[0:D_S5, :]) + _dot(attn.astype(BF), wout_ref[D_S5:D_S5 + D_NA, :])
    x1 = jnp.where(is_p, xp_ref[...], xs_ref[...]) + mod_ref[2:3, :] * out
    o_ref[...] = _ffn(x1, mod_ref, gf_ref, wg_ref, wu_ref, wd_ref)


def _even_out(st, xp, xs, mods, y, u, attn_p, attn_s, d_skip, w_glu, w_out, g_ffn, wg, wu, wd, layer):
    return pl.pallas_call(
        functools.partial(_even_out_kernel, p_steps=st.p_steps),
        grid=(st.steps,),
        in_specs=[st.prompt_spec(D_MODEL), st.sample_spec(D_MODEL), st.mod_spec(), st.row_spec(D_S5),
                  st.row_spec(D_S5), st.prompt_spec(D_NA), st.sample_spec(D_NA),
                  _resident((1, D_S5)), _resident(w_glu.shape), _resident(w_out.shape),
                  _resident((1, D_MODEL)), _layer_resident(wg.shape, layer), _layer_resident(wu.shape, layer),
                  _layer_resident(wd.shape, layer)],
        out_specs=st.row_spec(D_MODEL),
        out_shape=jax.ShapeDtypeStruct((st.total, D_MODEL), F32),
        compiler_params=_params("parallel"),
        name="even_out_ffn",
    )(xp, xs, mods, y, u, attn_p, attn_s, d_skip, w_glu, w_out, g_ffn, wg, wu, wd)


def _odd_in_kernel(x_ref, mod_ref, g_ref, w_ref, wdtt_ref, z_ref, xbc_ref, dt_ref, dtt_ref):
    h = _norm_mod(x_ref[...], g_ref[...], mod_ref[0:1, :], mod_ref[1:2, :]).astype(BF)
    z_ref[...] = _dot(h, w_ref[:, 0:D_INNER]).astype(z_ref.dtype)
    xbc_ref[...] = _dot(h, w_ref[:, D_INNER:D_INNER + SSD_CONV_DIM])
    dt_ref[...] = _dot(h, w_ref[:, D_INNER + SSD_CONV_DIM:D_INNER + SSD_CONV_DIM + 2 * SSD_H])
    dtt_ref[...] = _dot_nt(wdtt_ref[...], h)


def _odd_in(st, x, mods, g, w_in):
    wdtt = w_in[:, D_INNER + SSD_CONV_DIM:].T
    return pl.pallas_call(
        _odd_in_kernel,
        grid=(st.steps,),
        in_specs=[st.row_spec(D_MODEL), st.mod_spec(), _resident((1, D_MODEL)),
                  _resident(w_in.shape), _resident(wdtt.shape)],
        out_specs=[st.row_spec(D_INNER), st.row_spec(SSD_CONV_DIM), st.row_spec(2 * SSD_H),
                   pl.BlockSpec((2 * SSD_H, st.tm), lambda i: (0, i))],
        out_shape=[jax.ShapeDtypeStruct((st.total, D_INNER), BF),
                   jax.ShapeDtypeStruct((st.total, SSD_CONV_DIM), F32),
                   jax.ShapeDtypeStruct((st.total, 2 * SSD_H), F32),
                   jax.ShapeDtypeStruct((2 * SSD_H, st.total), F32)],
        compiler_params=_params("parallel"),
        name="odd_in",
    )(x, mods, g, w_in, wdtt)


def _conv_kernel(x_ref, prev_ref, next_ref, w_ref, b_ref, xc_ref, *, lt, n_p, seq_p, seq_s):
    tok = pl.program_id(0) * lt
    in_p = tok < n_p
    pos = jnp.where(in_p, tok % seq_p, (tok - n_p) % seq_s)
    seq = jnp.where(in_p, seq_p, seq_s)
    halo = prev_ref.shape[0]
    prev = jnp.where(pos == 0, 0.0, prev_ref[...])
    nxt = jnp.where(pos + lt == seq, 0.0, next_ref[...])
    ext = jnp.concatenate([prev, x_ref[...], nxt], axis=0)
    n_ext = lt + 2 * halo
    acc = b_ref[...] + jnp.zeros((lt, x_ref.shape[1]), F32)
    for kk in range(SSD_CONV):
        shift = (SSD_CONV // 2 - kk) % n_ext
        tap = ext if shift == 0 else pltpu.roll(ext, shift, 0)
        acc = acc + w_ref[kk:kk + 1, :] * tap[halo:halo + lt, :]
    xc_ref[...] = _silu(acc)


def _conv(xbc, conv_w, conv_b, n_p, seq_p, seq_s):
    total, c = xbc.shape
    lt = math.gcd(seq_p, 256)
    halo = 8
    nblk = total // halo
    per = lt // halo
    return pl.pallas_call(
        functools.partial(_conv_kernel, lt=lt, n_p=n_p, seq_p=seq_p, seq_s=seq_s),
        grid=(total // lt,),
        in_specs=[pl.BlockSpec((lt, c), lambda i: (i, 0)),
                  pl.BlockSpec((halo, c), lambda i: (jnp.maximum(i * per - 1, 0), 0)),
                  pl.BlockSpec((halo, c), lambda i: (jnp.minimum((i + 1) * per, nblk - 1), 0)),
                  _resident(conv_w.shape), _resident((1, c))],
        out_specs=pl.BlockSpec((lt, c), lambda i: (i, 0)),
        out_shape=jax.ShapeDtypeStruct((total, c), F32),
        compiler_params=_params("parallel"),
        name="ssd_conv",
    )(xbc, xbc, xbc, conv_w, conv_b.reshape(1, c))


def _ssd_kernel(xs0, b0, c0, dt0, dtt0, xs1, b1, c1, dt1, dtt1, alr_ref, alc_ref, dbr_ref, dbc_ref, dsk_ref, h00, h01,
                y0, y1, fin0, fin1, st0, st1, *, n_chunks, n_p, seq_p, seq_s):
    ci = pl.program_id(0)
    geometry = dict(n_chunks=n_chunks, n_p=n_p, seq_p=seq_p, seq_s=seq_s)
    _ssd_direction(0, ci, xs0, b0, c0, dt0, dtt0, alr_ref[0], alc_ref[0], dbr_ref[0], dbc_ref[0], dsk_ref, h00,
                   y0, fin0, st0, **geometry)
    _ssd_direction(1, ci, xs1, b1, c1, dt1, dtt1, alr_ref[1], alc_ref[1], dbr_ref[1], dbc_ref[1], dsk_ref, h01,
                   y1, fin1, st1, **geometry)


def _ssd_direction(d, ci, xs_ref, b_ref, c_ref, dt_ref, dtt_ref, a_log_row, a_log_col, dt_bias_row, dt_bias_col, dsk_ref,
                   h0_ref, y_ref, fin_ref, st_ref, *, n_chunks, n_p, seq_p, seq_s):
    q = SSD_Q
    blocks = D_INNER // LANES
    n_steps = n_chunks // SSD_STEP_CHUNKS
    tok = (ci if d == 0 else n_steps - 1 - ci) * (q * SSD_STEP_CHUNKS)
    in_p = tok < n_p
    pos = jnp.where(in_p, tok % seq_p, (tok - n_p) % seq_s)
    first = pos == 0
    last = pos + q * SSD_STEP_CHUNKS == jnp.where(in_p, seq_p, seq_s)
    start, end = (first, last) if d == 0 else (last, first)

    @pl.when(jnp.logical_and(start, in_p))
    def _():
        st_ref[...] = jnp.zeros(st_ref.shape, F32)

    @pl.when(jnp.logical_and(start, jnp.logical_not(in_p)))
    def _():
        h0 = h0_ref[...].reshape(D_INNER, SSD_N)
        for kb in range(blocks):
            st_ref[:, LANES * kb:LANES * (kb + 1)] = h0[LANES * kb:LANES * (kb + 1), :].T

    li = lax.broadcasted_iota(jnp.int32, (q, q), 0)
    si = lax.broadcasted_iota(jnp.int32, (q, q), 1)
    causal = (li >= si) if d == 0 else (li <= si)
    tri_l = jnp.where(causal, 1.0, 0.0)
    tri_r = jnp.where((li <= si) if d == 0 else (li >= si), 1.0, 0.0)
    a_row = -jnp.exp(a_log_row)
    a_col = -jnp.exp(a_log_col)
    first_head = lax.broadcasted_iota(jnp.int32, (q, LANES), 1) < SSD_P
    first_head2 = lax.broadcasted_iota(jnp.int32, (q + SSD_N, LANES), 1) < SSD_P
    for k in range(SSD_STEP_CHUNKS):
        rows = pl.ds((k if d == 0 else SSD_STEP_CHUNKS - 1 - k) * q, q)
        _ssd_chunk(xs_ref, b_ref, c_ref, dt_ref, dtt_ref, dt_bias_row, dt_bias_col, dsk_ref, y_ref, st_ref, rows, d,
                   a_row, a_col, causal, tri_l, tri_r, first_head, first_head2)

    @pl.when(jnp.logical_and(end, in_p))
    def _():
        for kb in range(blocks):
            t = st_ref[:, LANES * kb:LANES * (kb + 1)].T
            fin_ref[2 * kb:2 * kb + 2] = t.reshape(2, SSD_P, SSD_N)


def _ssd_chunk(xs_ref, b_ref, c_ref, dt_ref, dtt_ref, dt_bias_row, dt_bias_col, dsk_ref, y_ref, st_ref, rows, d,
               a_row, a_col, causal, tri_l, tri_r, first_head, first_head2):
    q = SSD_Q
    heads_per_group = SSD_H // SSD_G
    gw = heads_per_group * SSD_P
    dt_both = dt_ref[rows, :]
    dt_col = _softplus(dt_both[:, SSD_H * d:SSD_H * (d + 1)] + dt_bias_row)
    dt_row = _softplus(dtt_ref[:, rows] + dt_bias_col)
    cs_col = _dot_hi(tri_l, dt_col * a_row)
    da_row = dt_row * a_col
    cs_row = _dot_hi(da_row, tri_r)
    tot = jnp.sum(da_row, axis=-1, keepdims=True)
    w_row = dt_row * jnp.exp(tot - cs_row)
    etot = jnp.exp(tot)
    src_row = cs_row - jnp.log(dt_row)
    w_row_bf = w_row.astype(BF)
    for g in range(SSD_G):
        bg = b_ref[rows, SSD_N * g:SSD_N * (g + 1)]
        cg = c_ref[rows, SSD_N * g:SSD_N * (g + 1)].astype(BF)
        cb = _dot_nt(cg, bg.astype(BF)).astype(BF)
        bgt = bg.T.astype(BF)
        c_state = _dot(cg, st_ref[:, gw * g:gw * (g + 1)].astype(BF))
        for jp in range(heads_per_group // 2):
            pi = g * (heads_per_group // 2) + jp
            cols = slice(LANES * pi, LANES * (pi + 1))
            xp = xs_ref[rows, cols]
            xb = xp.astype(BF)
            res, grow = [], []
            for hh in range(2):
                h = 2 * pi + hh
                csc = jnp.broadcast_to(cs_col[:, h:h + 1], (q, q))
                grow.append(jnp.exp(csc))
                m = jnp.exp(jnp.where(causal, csc - src_row[h:h + 1, :], -jnp.inf)).astype(BF) * cb
                bw = bgt * w_row_bf[h:h + 1, :]
                res.append(_dot(jnp.concatenate([m, bw], axis=0), xb))
            both = jnp.where(first_head2, res[0], res[1])
            keep = jnp.where(first_head, etot[2 * pi:2 * pi + 1, :], etot[2 * pi + 1:2 * pi + 2, :])
            st_ref[:, cols] = keep * st_ref[:, cols] + both[q:q + SSD_N, :]
            y_off = jnp.where(first_head, grow[0], grow[1]) * c_state[:, LANES * jp:LANES * (jp + 1)]
            y = both[0:q, :] + y_off
            if d == 0:
                y = y + dsk_ref[:, cols] * xp
            y_ref[rows, cols] = y.astype(y_ref.dtype)


def _ssd(xc, dt_raw, dtt_raw, dt_bias, a_log, d_skip, h0, n_p, seq_p, seq_s):
    q = SSD_Q * SSD_STEP_CHUNKS
    assert seq_p % q == 0 and seq_s % q == 0
    total = xc.shape[0]
    n_steps = total // q
    n_chunks = total // SSD_Q
    n_prompt = n_p // seq_p
    n_sample = h0.shape[0]

    blk = (lambda i: i, lambda i: n_steps - 1 - i)
    state_block = (None, None, SSD_H, SSD_P, SSD_N)
    in_specs, args = [], []
    for d in range(2):
        in_specs += [pl.BlockSpec((q, D_INNER), lambda i, d=d: (blk[d](i), 0)),
                     pl.BlockSpec((q, SSD_GN), lambda i, d=d: (blk[d](i), D_INNER // SSD_GN)),
                     pl.BlockSpec((q, SSD_GN), lambda i, d=d: (blk[d](i), D_INNER // SSD_GN + 1)),
                     pl.BlockSpec((q, 2 * SSD_H), lambda i, d=d: (blk[d](i), 0)),
                     pl.BlockSpec((SSD_H, q), lambda i, d=d: (d, blk[d](i)))]
        args += [xc, xc, xc, dt_raw, dtt_raw]
    in_specs += [_resident((2, 1, SSD_H)), _resident((2, SSD_H, 1)), _resident((2, 1, SSD_H)), _resident((2, SSD_H, 1)),
                 _resident((1, D_INNER))]
    args += [a_log.reshape(2, 1, SSD_H), a_log.reshape(2, SSD_H, 1), dt_bias.reshape(2, 1, SSD_H),
             dt_bias.reshape(2, SSD_H, 1), d_skip]
    for d in range(2):
        in_specs.append(pl.BlockSpec(state_block,
                                     lambda i, d=d: (jnp.clip((blk[d](i) * q - n_p) // seq_s, 0, n_sample - 1), d, 0, 0, 0)))
        args.append(h0)
    fin_block = (None, SSD_H, SSD_P, SSD_N)
    y_f, y_b, fin_f, fin_b = pl.pallas_call(
        functools.partial(_ssd_kernel, n_chunks=n_chunks, n_p=n_p, seq_p=seq_p, seq_s=seq_s),
        grid=(n_steps,),
        in_specs=in_specs,
        out_specs=[pl.BlockSpec((q, D_INNER), lambda i: (blk[0](i), 0)),
                   pl.BlockSpec((q, D_INNER), lambda i: (blk[1](i), 0)),
                   pl.BlockSpec(fin_block, lambda i: (jnp.minimum(blk[0](i) * q // seq_p, n_prompt - 1), 0, 0, 0)),
                   pl.BlockSpec(fin_block, lambda i: (jnp.minimum(blk[1](i) * q // seq_p, n_prompt - 1), 0, 0, 0))],
        out_shape=[jax.ShapeDtypeStruct((total, D_INNER), BF), jax.ShapeDtypeStruct((total, D_INNER), BF),
                   jax.ShapeDtypeStruct((n_prompt, SSD_H, SSD_P, SSD_N), F32),
                   jax.ShapeDtypeStruct((n_prompt, SSD_H, SSD_P, SSD_N), F32)],
        scratch_shapes=[pltpu.VMEM((SSD_N, D_INNER), F32), pltpu.VMEM((SSD_N, D_INNER), F32)],
        compiler_params=_params("arbitrary"),
        name="ssd_scan",
    )(*args)
    return y_f, y_b, jnp.stack([fin_f, fin_b], axis=1)


def _odd_out_kernel(x_ref, mod_ref, yf_ref, yb_ref, z_ref, ng_ref, wout_ref,
                    gf_ref, wg_ref, wu_ref, wd_ref, fg_ref, *o_refs, p_steps, final):
    y = (yf_ref[...].astype(F32) + yb_ref[...].astype(F32)) * _silu(z_ref[...].astype(F32))
    inv = lax.rsqrt(jnp.mean(y * y, axis=-1, keepdims=True) + EPS)
    x1 = x_ref[...] + (mod_ref[2:3, :] * inv) * _dot((y * ng_ref[...]).astype(BF), wout_ref[...])
    x2 = _ffn(x1, mod_ref, gf_ref, wg_ref, wu_ref, wd_ref)
    if not final:
        o_refs[0][...] = x2
        return
    out = _rms(x2, fg_ref[...])
    is_p = pl.program_id(0) < p_steps

    @pl.when(is_p)
    def _():
        o_refs[0][...] = out

    @pl.when(jnp.logical_not(is_p))
    def _():
        o_refs[1][...] = out


def _odd_out(st, x, mods, y_f, y_b, z, norm_g, w_out, g_ffn, wg, wu, wd, layer, final_g, final):
    return pl.pallas_call(
        functools.partial(_odd_out_kernel, p_steps=st.p_steps, final=final),
        grid=(st.steps,),
        in_specs=[st.row_spec(D_MODEL), st.mod_spec(), st.row_spec(D_INNER), st.row_spec(D_INNER), st.row_spec(D_INNER),
                  _resident((1, D_INNER)), _resident(w_out.shape),
                  _resident((1, D_MODEL)), _layer_resident(wg.shape, layer), _layer_resident(wu.shape, layer),
                  _layer_resident(wd.shape, layer), _resident((1, D_MODEL))],
        out_specs=[st.prompt_spec(D_MODEL), st.sample_spec(D_MODEL)] if final else st.row_spec(D_MODEL),
        out_shape=([jax.ShapeDtypeStruct((st.n_p, D_MODEL), F32),
                    jax.ShapeDtypeStruct((st.total - st.n_p, D_MODEL), F32)] if final
                   else jax.ShapeDtypeStruct((st.total, D_MODEL), F32)),
        compiler_params=_params("arbitrary"),
        name="odd_out_ffn",
    )(x, mods, y_f, y_b, z, norm_g, w_out, g_ffn, wg, wu, wd, final_g)


def kernel(x_prompt, x_sample, cache_na_k, cache_na_v, state_s5_re, state_s5_im, state_ssd, c, c_ctx, norm_mix_g, norm_ffn_g, ada_w, ada_b, ffn_w_gate, ffn_w_up, ffn_w_down, ev_w_in, ev_w_out, s5_a_re, s5_a_im, s5_log_dt, s5_b_re, s5_b_im, s5_c_re, s5_c_im, s5_d, s5_w_glu, na_rpb, od_w_in, od_conv_w, od_conv_b, ssd_a_log, ssd_dt_bias, ssd_d, ssd_norm_g, od_w_out, final_norm_g):
    bp, seq_p, d = x_prompt.shape
    bs, seq_s, _ = x_sample.shape
    depth = ada_w.shape[0]
    n_p = bp * seq_p
    assert d == D_MODEL and n_p % seq_s == 0
    st = _Stream(n_p, seq_s, bs, tm=math.gcd(512, math.gcd(n_p, seq_s)))

    xp = x_prompt.reshape(n_p, d)
    xs = x_sample.reshape(bs * seq_s, d)
    cond = jnp.concatenate([c_ctx[None, :], c, jnp.zeros((8 - 1 - bs, d), F32)], axis=0)
    mods = _ada(cond, ada_w, ada_b).reshape(depth, 8, ADA_CHUNKS, d)
    row = lambda t: t.reshape(1, -1)

    new_k, new_v, new_s5_re, new_s5_im, new_ssd = [], [], [], [], []
    wg, wu, wd = ffn_w_gate.astype(BF), ffn_w_up.astype(BF), ffn_w_down.astype(BF)
    for layer in range(depth):
        g_mix = row(norm_mix_g[layer])
        g_ffn = row(norm_ffn_g[layer])
        if layer % 2 == 0:
            e = layer // 2
            if layer > 0:
                xp, xs = x[:n_p], x[n_p:]
            u, q, k, v = _even_in(st, xp, xs, mods[layer], g_mix, ev_w_in[e].astype(BF))
            attn_p = _ctx_attn(q, k, v, bp, seq_p)
            heads = lambda t: t[:n_p].reshape(bp, seq_p, NA_HEADS, NA_HD).transpose(0, 2, 1, 3)
            new_k.append(heads(k))
            new_v.append(heads(v))
            ctx = lambda t: t[:, e].transpose(0, 2, 1, 3).reshape(bs, -1, D_NA).astype(BF)
            attn_s = _na_attn(q, k, v, ctx(cache_na_k), ctx(cache_na_v), _bias_table(na_rpb[e]),
                              bs, seq_s, n_p // seq_s)
            v_op, bw, cw, sc = _s5_prep(s5_a_re[e], s5_a_im[e], s5_log_dt[e], s5_b_re[e], s5_b_im[e],
                                        s5_c_re[e], s5_c_im[e])
            y, fin = _s5(u, v_op, bw, cw, sc, _state_planes(state_s5_re[:, e], state_s5_im[:, e]),
                         n_p, seq_p, seq_s)
            fre, fim = _planes_state(fin)
            new_s5_re.append(fre)
            new_s5_im.append(fim)
            x = _even_out(st, xp, xs, mods[layer], y, u, attn_p, attn_s, row(s5_d[e]),
                          s5_w_glu[e].astype(BF), ev_w_out[e].astype(BF), g_ffn, wg, wu, wd, layer)
        else:
            o = layer // 2
            z, xbc, dt_raw, dtt_raw = _odd_in(st, x, mods[layer], g_mix, od_w_in[o].astype(BF))
            xc = _conv(xbc, od_conv_w[o], od_conv_b[o], n_p, seq_p, seq_s)
            y_f, y_b, fin = _ssd(xc, dt_raw, dtt_raw, ssd_dt_bias[o], ssd_a_log[o], row(jnp.repeat(ssd_d[o], SSD_P)),
                                 state_ssd[:, o], n_p, seq_p, seq_s)
            new_ssd.append(fin)
            x = _odd_out(st, x, mods[layer], y_f, y_b, z, row(ssd_norm_g[o]), od_w_out[o].astype(BF),
                         g_ffn, wg, wu, wd, layer, row(final_norm_g), layer == depth - 1)
    if depth % 2 == 1:
        raise NotImplementedError("final norm is fused into the last (odd) layer")
    y_prompt = x[0].reshape(bp, seq_p, d)
    y_sample = x[1].reshape(bs, seq_s, d)
    return (y_prompt, y_sample, jnp.stack(new_k, axis=1), jnp.stack(new_v, axis=1),
            jnp.stack(new_s5_re, axis=1), jnp.stack(new_s5_im, axis=1), jnp.stack(new_ssd, axis=1))
```

```python
import functools
import math

import jax
import jax.numpy as jnp
from jax import lax
from jax.experimental import pallas as pl
from jax.experimental.pallas import tpu as pltpu

F32 = jnp.float32
BF = jnp.bfloat16
HI = lax.Precision.HIGHEST

D_MODEL = 1024
EPS = 1e-6
ADA_CHUNKS = 6
GRID_W = 64
D_S5 = 512
S5_P = 16
S5_G = D_S5 // S5_P
S5_N = 64
S5_MIN_DECAY = 1e-4
S5_Q = 16
D_NA = 512
NA_HD = 64
NA_HEADS = D_NA // NA_HD
NA_WIN_ROWS = 8
NA_WIN_COLS = 16
NA_ROWS = 4
D_INNER = 2048
SSD_P = 64
SSD_H = D_INNER // SSD_P
SSD_G = 4
SSD_N = 128
SSD_Q = 128
SSD_STEP_CHUNKS = 2
SSD_CONV = 5
SSD_GN = SSD_G * SSD_N
SSD_CONV_DIM = D_INNER + 2 * SSD_GN
D_FF = 2816

V7X_VMEM_BYTES = 64 * 1024 * 1024
VMEM_LIMIT = V7X_VMEM_BYTES - 4 * 1024 * 1024
LANES = 128
NEG_BIG = -1e30


def _params(*sem):
    return pltpu.CompilerParams(dimension_semantics=sem, vmem_limit_bytes=VMEM_LIMIT)


def _resident(shape):
    nd = len(shape)
    return pl.BlockSpec(shape, lambda *_: (0,) * nd, pipeline_mode=pl.Buffered(1))


def _layer_resident(shape, layer):
    nd = len(shape)
    return pl.BlockSpec((None,) + tuple(shape[1:]), lambda *_: (layer,) + (0,) * (nd - 1), pipeline_mode=pl.Buffered(1))


def _dot(a, b):
    return jnp.dot(a, b, preferred_element_type=F32)


def _dot_nt(a, b):
    return lax.dot_general(a, b, (((1,), (1,)), ((), ())), preferred_element_type=F32)


def _dot_hi(a, b):
    return jnp.dot(a, b, preferred_element_type=F32, precision=HI)


def _sigmoid(x):
    return 1.0 / (1.0 + jnp.exp(-x))


def _silu(x):
    return x * _sigmoid(x)


def _softplus(x):
    return jnp.maximum(x, 0.0) + jnp.log1p(jnp.exp(-jnp.abs(x)))


def _rms(x, g):
    return x * lax.rsqrt(jnp.mean(x * x, axis=-1, keepdims=True) + EPS) * g


def _norm_mod(x, g, shift, scale):
    return _rms(x, g) * (1.0 + scale) + shift


def _ada_kernel(c_ref, w_ref, b_ref, o_ref):
    c = c_ref[...]
    o_ref[...] = _dot(_silu(c).astype(BF), w_ref[...].astype(BF)) + b_ref[...]


def _ada(cond, ada_w, ada_b):
    depth, d, n = ada_w.shape
    tn = 1536
    return pl.pallas_call(
        _ada_kernel,
        grid=(depth, n // tn),
        in_specs=[pl.BlockSpec((8, d), lambda l, j: (0, 0)),
                  pl.BlockSpec((None, d, tn), lambda l, j: (l, 0, j)),
                  pl.BlockSpec((None, 1, tn), lambda l, j: (l, 0, j))],
        out_specs=pl.BlockSpec((None, 8, tn), lambda l, j: (l, 0, j)),
        out_shape=jax.ShapeDtypeStruct((depth, 8, n), F32),
        compiler_params=_params("parallel", "parallel"),
        name="adaln",
    )(cond, ada_w, ada_b.reshape(depth, 1, n))


class _Stream:
    def __init__(self, n_p, len_s, n_s, tm):
        assert n_p % tm == 0 and len_s % tm == 0
        self.n_p, self.len_s, self.n_s, self.tm = n_p, len_s, n_s, tm
        self.total = n_p + len_s * n_s
        self.steps = self.total // tm
        self.p_steps = n_p // tm

    def group(self, i):
        t = i * self.tm
        return jnp.where(t < self.n_p, 0, 1 + (t - self.n_p) // self.len_s)

    def mod_spec(self):
        return pl.BlockSpec((None, ADA_CHUNKS, D_MODEL), lambda i: (self.group(i), 0, 0))

    def row_spec(self, width, col=0):
        return pl.BlockSpec((self.tm, width), lambda i: (i, col))

    def prompt_spec(self, width):
        return pl.BlockSpec((self.tm, width), lambda i: (jnp.minimum(i, self.p_steps - 1), 0))

    def sample_spec(self, width):
        return pl.BlockSpec((self.tm, width), lambda i: (jnp.maximum(i - self.p_steps, 0), 0))


def _ffn(x1, mod_ref, g_ref, wg_ref, wu_ref, wd_ref):
    h = _norm_mod(x1, g_ref[...], mod_ref[3:4, :], mod_ref[4:5, :]).astype(BF)
    hid = (_silu(_dot(h, wg_ref[...])) * _dot(h, wu_ref[...])).astype(BF)
    return x1 + mod_ref[5:6, :] * _dot(hid, wd_ref[...])


def _even_in_kernel(xp_ref, xs_ref, mod_ref, g_ref, w_ref, u_ref, q_ref, k_ref, v_ref, *, p_steps):
    x = jnp.where(pl.program_id(0) < p_steps, xp_ref[...], xs_ref[...])
    h = _norm_mod(x, g_ref[...], mod_ref[0:1, :], mod_ref[1:2, :])
    r = _dot(h.astype(BF), w_ref[...])
    u_ref[...] = r[:, 0:D_S5]
    q_ref[...] = r[:, D_S5:D_S5 + D_NA] * (NA_HD ** -0.5)
    k_ref[...] = r[:, D_S5 + D_NA:D_S5 + 2 * D_NA]
    v_ref[...] = r[:, D_S5 + 2 * D_NA:D_S5 + 3 * D_NA]


def _even_in(st, xp, xs, mods, g, w_in):
    n_out = w_in.shape[1]
    out = jax.ShapeDtypeStruct((st.total, D_S5), F32)
    return pl.pallas_call(
        functools.partial(_even_in_kernel, p_steps=st.p_steps),
        grid=(st.steps,),
        in_specs=[st.prompt_spec(D_MODEL), st.sample_spec(D_MODEL), st.mod_spec(), _resident((1, D_MODEL)),
                  _resident((D_MODEL, n_out))],
        out_specs=[st.row_spec(D_S5)] * 4,
        out_shape=[out] * 4,
        compiler_params=_params("parallel"),
        name="even_in",
    )(xp, xs, mods, g, w_in)


def _ctx_attn_kernel(q_ref, k_ref, v_ref, o_ref):
    lane = lax.broadcasted_iota(jnp.int32, (q_ref.shape[0], LANES), 1)
    for pair in range(D_NA // LANES):
        cols = slice(LANES * pair, LANES * (pair + 1))
        q = q_ref[:, cols]
        k = k_ref[:, cols].astype(BF)
        v = v_ref[:, cols].astype(BF)
        outs = []
        for h in range(2):
            in_head = (lane >= NA_HD * h) & (lane < NA_HD * (h + 1))
            qh = jnp.where(in_head, q, 0.0).astype(BF)
            s = _dot_nt(qh, k)
            p = jnp.exp(s - jnp.max(s, axis=-1, keepdims=True))
            l = jnp.sum(p, axis=-1, keepdims=True)
            outs.append(_dot(p.astype(BF), v) / l)
        o_ref[:, cols] = jnp.where(lane < NA_HD, outs[0], outs[1])


def _ctx_attn(q, k, v, n_seq, seq):
    spec = pl.BlockSpec((seq, D_NA), lambda b: (b, 0))
    return pl.pallas_call(
        _ctx_attn_kernel,
        grid=(n_seq,),
        in_specs=[spec, spec, spec],
        out_specs=spec,
        out_shape=jax.ShapeDtypeStruct((n_seq * seq, D_NA), F32),
        compiler_params=_params("parallel"),
        name="ctx_attn",
    )(q, k, v)


NA_N_DR = 2 * NA_WIN_ROWS - 1
NA_BOTH, NA_LEFT, NA_RIGHT = 0, NA_N_DR - 1, 2 * NA_N_DR - 1
NA_NONE = 3 * NA_N_DR - 1
NA_UNION = NA_WIN_ROWS + NA_ROWS


def _bias_kernel(rpb_ref, o_ref):
    half = LANES // 2
    wq = lax.broadcasted_iota(jnp.int32, (GRID_W, LANES), 0)
    lane = lax.broadcasted_iota(jnp.int32, (GRID_W, LANES), 1)
    wk = lane & (GRID_W - 1)
    left = lane < GRID_W
    col_start = jnp.clip(wq - NA_WIN_COLS // 2, 0, GRID_W - NA_WIN_COLS)
    ok = (wk >= col_start) & (wk < col_start + NA_WIN_COLS)
    dc = jnp.clip(wk - wq, -(NA_WIN_COLS - 1), NA_WIN_COLS - 1) + (NA_WIN_COLS - 1)
    idx = jnp.where(left, dc, dc + half)
    low_lanes = lax.broadcasted_iota(jnp.int32, (1, LANES), 1) < half

    def entry(e, carry):
        is_both = e < NA_LEFT
        is_left = jnp.logical_and(e >= NA_LEFT, e < NA_RIGHT)
        is_right = jnp.logical_and(e >= NA_RIGHT, e < NA_NONE)
        d_left = jnp.where(is_both, e, jnp.where(is_left, e - NA_LEFT, 0))
        d_right = jnp.where(is_both, e + 1, jnp.where(is_right, e - NA_RIGHT, 0))
        left_on = jnp.where(jnp.logical_or(is_both, is_left), 1, 0)
        right_on = jnp.where(jnp.logical_or(is_both, is_right), 1, 0)
        vals = jnp.where(low_lanes, rpb_ref[pl.ds(d_left, 1), :], pltpu.roll(rpb_ref[pl.ds(d_right, 1), :], half, 1))
        tile = jnp.take_along_axis(jnp.broadcast_to(vals, (GRID_W, LANES)), idx, axis=1)
        side_on = jnp.where(left, left_on, right_on) > 0
        o_ref[e] = jnp.where(ok & side_on, tile, NEG_BIG)
        return carry

    lax.fori_loop(0, NA_NONE + 1, entry, 0, unroll=9)


def _bias_table(rpb):
    n_dc = rpb.shape[-1]
    padded = jnp.pad(rpb, ((0, 0), (0, 0), (0, LANES - n_dc)))
    return pl.pallas_call(
        _bias_kernel,
        grid=(NA_HEADS,),
        in_specs=[pl.BlockSpec((None, NA_N_DR, LANES), lambda h: (h, 0, 0))],
        out_specs=pl.BlockSpec((None, NA_NONE + 1, GRID_W, LANES), lambda h: (h, 0, 0, 0)),
        out_shape=jax.ShapeDtypeStruct((NA_HEADS, NA_NONE + 1, GRID_W, LANES), F32),
        compiler_params=_params("parallel"),
        name="na_bias",
    )(padded)


def _na_kernel(q_ref, k_ref, v_ref, ck_ref, cv_ref, bias_ref, o_ref, kb_ref, vb_ref, *, rows):
    kb_ref[...] = k_ref[...].astype(BF)
    vb_ref[...] = v_ref[...].astype(BF)
    ck = ck_ref[...]
    cv = cv_ref[...]
    kh = NA_WIN_ROWS
    nq = NA_ROWS * GRID_W
    lane = lax.broadcasted_iota(jnp.int32, (nq, LANES), 1)

    def body(g, carry):
        r0 = g * NA_ROWS
        first = jnp.clip(r0 - kh // 2, 0, rows - NA_UNION)
        qoff = pl.multiple_of(r0 * GRID_W, nq)
        q = q_ref[pl.ds(qoff, nq), :]
        koff = pl.multiple_of(first * GRID_W, GRID_W)
        kw = kb_ref[pl.ds(koff, NA_UNION * GRID_W), :]
        vw = vb_ref[pl.ds(koff, NA_UNION * GRID_W), :]
        tile_idx = []
        for j in range(NA_ROWS):
            r = r0 + j
            start = jnp.clip(r - kh // 2, 0, rows - kh)
            for ip in range(NA_UNION // 2):
                k0 = first + 2 * ip
                in0 = jnp.logical_and(k0 >= start, k0 < start + kh)
                in1 = jnp.logical_and(k0 + 1 >= start, k0 + 1 < start + kh)
                dr0 = k0 - r + (NA_WIN_ROWS - 1)
                idx = jnp.where(jnp.logical_and(in0, in1), NA_BOTH + dr0,
                                jnp.where(in0, NA_LEFT + dr0, jnp.where(in1, NA_RIGHT + dr0 + 1, NA_NONE)))
                tile_idx.append(jnp.clip(idx, 0, NA_NONE))
        outs = []
        for h in range(2):
            in_head = (lane >= NA_HD * h) & (lane < NA_HD * (h + 1))
            qh = jnp.where(in_head, q, 0.0).astype(BF)
            per_row = NA_UNION // 2
            bias = jnp.concatenate(
                [jnp.concatenate([bias_ref[h, tile_idx[j * per_row + ip]] for ip in range(per_row)], axis=1)
                 for j in range(NA_ROWS)], axis=0)
            s_loc = _dot_nt(qh, kw) + bias
            s_ctx = _dot_nt(qh, ck)
            m = jnp.maximum(jnp.max(s_loc, axis=-1, keepdims=True), jnp.max(s_ctx, axis=-1, keepdims=True))
            p_loc = jnp.exp(s_loc - m)
            p_ctx = jnp.exp(s_ctx - m)
            l = jnp.sum(p_loc, axis=-1, keepdims=True) + jnp.sum(p_ctx, axis=-1, keepdims=True)
            outs.append((_dot(p_loc.astype(BF), vw) + _dot(p_ctx.astype(BF), cv)) / l)
        o_ref[pl.ds(qoff, nq), :] = jnp.where(lane < NA_HD, outs[0], outs[1])
        return carry

    lax.fori_loop(0, rows // NA_ROWS, body, 0, unroll=2)


def _na_attn(q, k, v, ck, cv, bias, n_seq, seq, row_base):
    rows = seq // GRID_W
    assert rows >= NA_UNION and rows % NA_ROWS == 0
    past = ck.shape[1]
    spec = pl.BlockSpec((seq, LANES), lambda b, p: (row_base + b, p))
    cspec = pl.BlockSpec((None, past, LANES), lambda b, p: (b, 0, p))
    return pl.pallas_call(
        functools.partial(_na_kernel, rows=rows),
        grid=(n_seq, D_NA // LANES),
        in_specs=[spec, spec, spec, cspec, cspec,
                  pl.BlockSpec((2, NA_NONE + 1, GRID_W, LANES), lambda b, p: (p, 0, 0, 0))],
        out_specs=pl.BlockSpec((seq, LANES), lambda b, p: (b, p)),
        out_shape=jax.ShapeDtypeStruct((n_seq * seq, D_NA), F32),
        scratch_shapes=[pltpu.VMEM((seq, LANES), BF), pltpu.VMEM((seq, LANES), BF)],
        compiler_params=_params("parallel", "parallel"),
        name="na_attn",
    )(q, k, v, ck, cv, bias)


S5_BLK = LANES // S5_P
S5_NB = S5_G // S5_BLK
S5_ST = S5_BLK * S5_N
S5_W = S5_Q * LANES


def _cexp(zr, zi):
    e = jnp.exp(zr)
    return e * jnp.cos(zi), e * jnp.sin(zi)


def _cmul(ar, ai, br, bi):
    return ar * br - ai * bi, ar * bi + ai * br


def _s5_prep_kernel(arr_ref, ari_ref, ldr_ref, acr_ref, aci_ref, ldc_ref, btr_ref, bti_ref, ctr_ref, cti_ref,
                    v_ref, bw_ref, cw_ref, sc_ref,
                    bbr_s, bbi_s, ccr_s, cci_s, pwr_s, pwi_s, pcr_s, pci_s, p0_s, p1_s):
    t = pl.program_id(1)
    q = S5_Q
    nst = S5_ST

    @pl.when(t == 0)
    def _():
        expand = jnp.where(lax.broadcasted_iota(jnp.int32, (S5_P, 2 * LANES), 0)
                           == (lax.broadcasted_iota(jnp.int32, (S5_P, 2 * LANES), 1) & (S5_P - 1)), 1.0, 0.0)
        row_g = lax.broadcasted_iota(jnp.int32, (LANES, nst), 0) >> 4
        col_g = lax.broadcasted_iota(jnp.int32, (LANES, nst), 1) >> 6
        st_g = lax.broadcasted_iota(jnp.int32, (nst, LANES), 0) >> 6
        ch_g = lax.broadcasted_iota(jnp.int32, (nst, LANES), 1) >> 4
        lane_d = lax.broadcasted_iota(jnp.int32, (nst, LANES), 1) >> 4
        taps = []
        for d in range(2):
            ar = jnp.minimum(arr_ref[d], -S5_MIN_DECAY)
            ai = ari_ref[d]
            dt = jnp.exp(ldr_ref[d])
            abr, abi = _cexp(ar * dt, ai * dt)
            den = ar * ar + ai * ai
            cfr = ((abr - 1.0) * ar + abi * ai) / den
            cfi = (abi * ar - (abr - 1.0) * ai) / den
            bbr, bbi = _cmul(cfr, cfi, btr_ref[d], bti_ref[d])
            bbr = jnp.where(row_g == col_g, jnp.concatenate([bbr] * S5_BLK, axis=0), 0.0)
            bbi = jnp.where(row_g == col_g, jnp.concatenate([bbi] * S5_BLK, axis=0), 0.0)
            bbr_s[d] = bbr
            bbi_s[d] = bbi
            pr = jnp.ones((1, nst), F32)
            pi = jnp.zeros((1, nst), F32)
            for e in range(q + 1):
                pwr_s[d, e] = pr
                pwi_s[d, e] = pi
                pr, pi = _cmul(pr, pi, abr, abi)
            sc_ref[2 * d:2 * d + 1, :] = pwr_s[d, q]
            sc_ref[2 * d + 1:2 * d + 2, :] = pwi_s[d, q]
            arc = jnp.minimum(acr_ref[d], -S5_MIN_DECAY)
            dtc = jnp.exp(ldc_ref[d])
            acr, aci = _cexp(jnp.broadcast_to(arc * dtc, (nst, LANES)), jnp.broadcast_to(aci_ref[d] * dtc, (nst, LANES)))
            pr = jnp.ones((nst, LANES), F32)
            pi = jnp.zeros((nst, LANES), F32)
            for e in range(q + 1):
                pcr_s[d, e] = pr
                pci_s[d, e] = pi
                pr, pi = _cmul(pr, pi, acr, aci)
            cxr = _dot_hi(ctr_ref[d], expand)
            cxi = _dot_hi(cti_ref[d], expand)
            ccr_s[d] = jnp.where(st_g == ch_g, cxr[:, 0:LANES], 0.0)
            cci_s[d] = jnp.where(st_g == ch_g, cxi[:, 0:LANES], 0.0)
            pws_r, pws_i = [], []
            for k in range(2):
                sel_r = jnp.zeros((nst, LANES), F32)
                sel_i = jnp.zeros((nst, LANES), F32)
                for j in range(S5_BLK):
                    dl = S5_BLK * k + j
                    e = dl if d == 0 else q - 1 - dl
                    sel_r = jnp.where(lane_d == j, pcr_s[d, e], sel_r)
                    sel_i = jnp.where(lane_d == j, pci_s[d, e], sel_i)
                pws_r.append(sel_r)
                pws_i.append(sel_i)
            ggr, ggi = _cmul(cxr, cxi, jnp.concatenate(pws_r, axis=1), jnp.concatenate(pws_i, axis=1))
            taps.append(_dot_hi(bbr, ggr) - _dot_hi(bbi, ggi))
        lane2 = lax.broadcasted_iota(jnp.int32, (LANES, 2 * LANES), 1)
        kt0 = taps[0] + jnp.where(lane2 < S5_P, pltpu.roll(taps[1], S5_P, 1), 0.0)
        kt1 = jnp.where(lane2 >= 2 * LANES - S5_P, 0.0, taps[1])
        er = lax.broadcasted_iota(jnp.int32, (2 * LANES, S5_W), 0)
        ec = lax.broadcasted_iota(jnp.int32, (2 * LANES, S5_W), 1)
        place = jnp.where(((er >> 4) == (ec >> 7)) & ((er & (S5_P - 1)) == (ec & (S5_P - 1))), 1.0, 0.0).astype(BF)
        own = (lax.broadcasted_iota(jnp.int32, (LANES, S5_W), 0) >> 4) == (
            (lax.broadcasted_iota(jnp.int32, (LANES, S5_W), 1) >> 4) & (S5_BLK - 1))
        zeros = jnp.zeros((LANES, S5_W), BF)
        p0_s[:, 0:S5_W] = zeros
        p0_s[:, S5_W:2 * S5_W] = jnp.where(own, _dot(kt0.astype(BF), place), 0.0).astype(BF)
        p1_s[:, 0:S5_W] = jnp.where(own, _dot(kt1.astype(BF), place), 0.0).astype(BF)
        p1_s[:, S5_W:2 * S5_W] = zeros

    off0 = pl.multiple_of(S5_W - LANES * t, LANES)
    off1 = pl.multiple_of(LANES * (q - 1 - t), LANES)
    v_ref[...] = p0_s[:, pl.ds(off0, S5_W)] + p1_s[:, pl.ds(off1, S5_W)]
    for d in range(2):
        e_b = (q - 1 - t) if d == 0 else t
        br, bi = _cmul(pwr_s[d, e_b], pwi_s[d, e_b], bbr_s[d], bbi_s[d])
        bw_ref[:, 2 * nst * d:2 * nst * d + nst] = br.astype(BF)
        bw_ref[:, 2 * nst * d + nst:2 * nst * (d + 1)] = bi.astype(BF)
        e_c = (t + 1) if d == 0 else (q - t)
        gr, gi = _cmul(ccr_s[d], cci_s[d], pcr_s[d, e_c], pci_s[d, e_c])
        cw_ref[2 * nst * d:2 * nst * d + nst, :] = gr.astype(BF)
        cw_ref[2 * nst * d + nst:2 * nst * (d + 1), :] = (-gi).astype(BF)


def _s5_prep(a_re, a_im, log_dt, b_re, b_im, c_re, c_im):
    nst, nb, q = S5_ST, S5_NB, S5_Q
    row = lambda t: t.reshape(2, nb, 1, nst)
    col = lambda t: t.reshape(2, nb, nst, 1)
    ld = jnp.broadcast_to(log_dt[:, :, None], (2, S5_G, S5_N))
    bt = lambda t: t.reshape(2, nb, S5_BLK, S5_N, S5_P).transpose(0, 1, 4, 2, 3).reshape(2, nb, S5_P, nst)
    ct = lambda t: t.reshape(2, nb, S5_BLK, S5_P, S5_N).transpose(0, 1, 2, 4, 3).reshape(2, nb, nst, S5_P)
    rspec = pl.BlockSpec((2, None, 1, nst), lambda b, t: (0, b, 0, 0))
    cspec = pl.BlockSpec((2, None, nst, 1), lambda b, t: (0, b, 0, 0))
    btspec = pl.BlockSpec((2, None, S5_P, nst), lambda b, t: (0, b, 0, 0))
    ctspec = pl.BlockSpec((2, None, nst, S5_P), lambda b, t: (0, b, 0, 0))
    big = jax.ShapeDtypeStruct((nb, S5_W, S5_W), BF)
    return pl.pallas_call(
        _s5_prep_kernel,
        grid=(nb, q),
        in_specs=[rspec, rspec, rspec, cspec, cspec, cspec, btspec, btspec, ctspec, ctspec],
        out_specs=[pl.BlockSpec((None, LANES, S5_W), lambda b, t: (b, t, 0)),
                   pl.BlockSpec((None, LANES, S5_W), lambda b, t: (b, t, 0)),
                   pl.BlockSpec((None, S5_W, LANES), lambda b, t: (b, 0, t)),
                   pl.BlockSpec((None, 4, nst), lambda b, t: (b, 0, 0))],
        out_shape=[big, big, big, jax.ShapeDtypeStruct((nb, 4, nst), F32)],
        scratch_shapes=[pltpu.VMEM((2, LANES, nst), F32), pltpu.VMEM((2, LANES, nst), F32),
                        pltpu.VMEM((2, nst, LANES), F32), pltpu.VMEM((2, nst, LANES), F32),
                        pltpu.VMEM((2, q + 1, 1, nst), F32), pltpu.VMEM((2, q + 1, 1, nst), F32),
                        pltpu.VMEM((2, q + 1, nst, LANES), F32), pltpu.VMEM((2, q + 1, nst, LANES), F32),
                        pltpu.VMEM((LANES, 2 * S5_W), BF), pltpu.VMEM((LANES, 2 * S5_W), BF)],
        compiler_params=_params("parallel", "arbitrary"),
        name="s5_prep",
    )(row(a_re), row(a_im), row(ld), col(a_re), col(a_im), col(ld), bt(b_re), bt(b_im), ct(c_re), ct(c_im))


def _s5_kernel(u_ref, v_ref, bw_ref, cw_ref, sc_ref, h0_ref, y_ref, fin_ref, z_s, h_s, *, p_steps, n_sub, mp):
    s = pl.program_id(1)
    m = u_ref.shape[0] // S5_Q
    n_tiles = 4 * S5_ST // LANES
    per = S5_ST // LANES
    half = 2 * per
    chunk_rows = lambda t: pl.ds(t, m, stride=S5_Q)
    tile_rows = lambda k: pl.ds(k, m, stride=n_tiles)
    ucat = jnp.concatenate([u_ref[chunk_rows(t), :].astype(BF) for t in range(S5_Q)], axis=1)
    z = _dot(ucat, bw_ref[...])
    for k in range(n_tiles):
        z_s[tile_rows(k), :] = z[:, LANES * k:LANES * (k + 1)]
    sc = sc_ref[...]
    stack = lambda row: jnp.concatenate([row[:, LANES * j:LANES * (j + 1)] for j in range(per)], axis=0)

    def multipliers(d):
        ar, ai = stack(sc[2 * d:2 * d + 1]), stack(sc[2 * d + 1:2 * d + 2])
        return jnp.concatenate([ar, ar], axis=0), jnp.concatenate([-ai, ai], axis=0)

    a1f, a2f = multipliers(0)
    a1b, a2b = multipliers(1)

    def advance(hf, hb, cf, cb):
        rf = pl.ds(pl.multiple_of(cf * n_tiles, half), half)
        rb = pl.ds(pl.multiple_of(cb * n_tiles + half, half), half)
        h_s[rf, :] = hf
        h_s[rb, :] = hb
        hf = a1f * hf + a2f * pltpu.roll(hf, per, 0) + z_s[rf, :]
        hb = a1b * hb + a2b * pltpu.roll(hb, per, 0) + z_s[rb, :]
        return hf, hb

    unstack = lambda h4: jnp.concatenate([h4[j:j + 1, :] for j in range(per)], axis=1)

    @pl.when(s < p_steps)
    def _():
        for i in range(n_sub):
            hf = jnp.zeros((half, LANES), F32)
            hb = jnp.zeros((half, LANES), F32)
            for c in range(mp):
                hf, hb = advance(hf, hb, i * mp + c, i * mp + mp - 1 - c)
            for pi, h4 in enumerate((hf[0:per], hf[per:half], hb[0:per], hb[per:half])):
                fin_ref[i, pi:pi + 1, :] = unstack(h4)

    @pl.when(s >= p_steps)
    def _():
        h0 = h0_ref[...]
        init = (jnp.concatenate([stack(h0[0:1]), stack(h0[1:2])], axis=0),
                jnp.concatenate([stack(h0[2:3]), stack(h0[3:4])], axis=0))
        lax.fori_loop(0, m, lambda c, h: advance(h[0], h[1], c, m - 1 - c), init)

    hcat = jnp.concatenate([h_s[tile_rows(k), :] for k in range(n_tiles)], axis=1).astype(BF)
    ycat = _dot(ucat, v_ref[...]) + _dot(hcat, cw_ref[...])
    for t in range(S5_Q):
        y_ref[chunk_rows(t), :] = ycat[:, LANES * t:LANES * (t + 1)]


def _s5(u, v, bw, cw, sc, h0, n_p, seq_p, seq_s):
    total = u.shape[0]
    q = S5_Q
    m = seq_s // q
    mp = seq_p // q
    n_sub = seq_s // seq_p
    p_steps = n_p // seq_s
    n_prompt = n_p // seq_p
    wspec = pl.BlockSpec((None, S5_W, S5_W), lambda b, s: (b, 0, 0), pipeline_mode=pl.Buffered(1))
    tok = pl.BlockSpec((seq_s, LANES), lambda b, s: (s, b))
    y, fin = pl.pallas_call(
        functools.partial(_s5_kernel, p_steps=p_steps, n_sub=n_sub, mp=mp),
        grid=(S5_NB, total // seq_s),
        in_specs=[tok, wspec, wspec, wspec,
                  pl.BlockSpec((None, 4, S5_ST), lambda b, s: (b, 0, 0)),
                  pl.BlockSpec((None, 4, S5_ST), lambda b, s: (jnp.maximum(s - p_steps, 0), 0, b))],
        out_specs=[tok, pl.BlockSpec((n_sub, 4, S5_ST), lambda b, s: (jnp.minimum(s, p_steps - 1), 0, b))],
        out_shape=[jax.ShapeDtypeStruct((total, D_S5), F32),
                   jax.ShapeDtypeStruct((n_prompt, 4, S5_G * S5_N), F32)],
        scratch_shapes=[pltpu.VMEM((4 * S5_ST // LANES * m, LANES), F32)] * 2,
        compiler_params=_params("parallel", "arbitrary"),
        name="s5_scan",
    )(u, v, bw, cw, sc, h0)
    return y, fin


def _state_planes(re, im):
    b = re.shape[0]
    return jnp.stack([t[:, d].reshape(b, S5_G * S5_N) for d in range(2) for t in (re, im)], axis=1)


def _planes_state(fin):
    t = fin.reshape(fin.shape[0], 2, 2, S5_G, S5_N)
    return t[:, :, 0], t[:, :, 1]


def _gelu_tanh(x):
    return 0.5 * x * (1.0 + jnp.tanh(math.sqrt(2.0 / math.pi) * (x + 0.044715 * (x * x * x))))


def _even_out_kernel(xp_ref, xs_ref, mod_ref, y_ref, u_ref, ap_ref, as_ref, dsk_ref, wglu_ref, wout_ref,
                     gf_ref, wg_ref, wu_ref, wd_ref, o_ref, *, p_steps):
    is_p = pl.program_id(0) < p_steps
    attn = jnp.where(is_p, ap_ref[...], as_ref[...])
    y = _gelu_tanh(y_ref[...] + u_ref[...] * dsk_ref[...])
    y = y * _sigmoid(_dot(y.astype(BF), wglu_ref[...]))
    out = _dot(y.astype(BF), wout_ref[0:D_S5, :]) + _dot(attn.astype(BF), wout_ref[D_S5:D_S5 + D_NA, :])
    x1 = jnp.where(is_p, xp_ref[...], xs_ref[...]) + mod_ref[2:3, :] * out
    o_ref[...] = _ffn(x1, mod_ref, gf_ref, wg_ref, wu_ref, wd_ref)


def _even_out(st, xp, xs, mods, y, u, attn_p, attn_s, d_skip, w_glu, w_out, g_ffn, wg, wu, wd, layer):
    return pl.pallas_call(
        functools.partial(_even_out_kernel, p_steps=st.p_steps),
        grid=(st.steps,),
        in_specs=[st.prompt_spec(D_MODEL), st.sample_spec(D_MODEL), st.mod_spec(), st.row_spec(D_S5),
                  st.row_spec(D_S5), st.prompt_spec(D_NA), st.sample_spec(D_NA),
                  _resident((1, D_S5)), _resident(w_glu.shape), _resident(w_out.shape),
                  _resident((1, D_MODEL)), _layer_resident(wg.shape, layer), _layer_resident(wu.shape, layer),
                  _layer_resident(wd.shape, layer)],
        out_specs=st.row_spec(D_MODEL),
        out_shape=jax.ShapeDtypeStruct((st.total, D_MODEL), F32),
        compiler_params=_params("parallel"),
        name="even_out_ffn",
    )(xp, xs, mods, y, u, attn_p, attn_s, d_skip, w_glu, w_out, g_ffn, wg, wu, wd)


def _odd_in_kernel(x_ref, mod_ref, g_ref, w_ref, wdtt_ref, z_ref, xbc_ref, dt_ref, dtt_ref):
    h = _norm_mod(x_ref[...], g_ref[...], mod_ref[0:1, :], mod_ref[1:2, :]).astype(BF)
    z_ref[...] = _dot(h, w_ref[:, 0:D_INNER])
    xbc_ref[...] = _dot(h, w_ref[:, D_INNER:D_INNER + SSD_CONV_DIM])
    dt_ref[...] = _dot(h, w_ref[:, D_INNER + SSD_CONV_DIM:D_INNER + SSD_CONV_DIM + 2 * SSD_H])
    dtt_ref[...] = _dot_nt(wdtt_ref[...], h)


def _odd_in(st, x, mods, g, w_in):
    wdtt = w_in[:, D_INNER + SSD_CONV_DIM:].T
    return pl.pallas_call(
        _odd_in_kernel,
        grid=(st.steps,),
        in_specs=[st.row_spec(D_MODEL), st.mod_spec(), _resident((1, D_MODEL)),
                  _resident(w_in.shape), _resident(wdtt.shape)],
        out_specs=[st.row_spec(D_INNER), st.row_spec(SSD_CONV_DIM), st.row_spec(2 * SSD_H),
                   pl.BlockSpec((2 * SSD_H, st.tm), lambda i: (0, i))],
        out_shape=[jax.ShapeDtypeStruct((st.total, D_INNER), F32),
                   jax.ShapeDtypeStruct((st.total, SSD_CONV_DIM), F32),
                   jax.ShapeDtypeStruct((st.total, 2 * SSD_H), F32),
                   jax.ShapeDtypeStruct((2 * SSD_H, st.total), F32)],
        compiler_params=_params("parallel"),
        name="odd_in",
    )(x, mods, g, w_in, wdtt)


def _conv_kernel(x_ref, prev_ref, next_ref, w_ref, b_ref, xc_ref, *, lt, n_p, seq_p, seq_s):
    tok = pl.program_id(0) * lt
    in_p = tok < n_p
    pos = jnp.where(in_p, tok % seq_p, (tok - n_p) % seq_s)
    seq = jnp.where(in_p, seq_p, seq_s)
    halo = prev_ref.shape[0]
    prev = jnp.where(pos == 0, 0.0, prev_ref[...])
    nxt = jnp.where(pos + lt == seq, 0.0, next_ref[...])
    ext = jnp.concatenate([prev, x_ref[...], nxt], axis=0)
    n_ext = lt + 2 * halo
    acc = b_ref[...] + jnp.zeros((lt, x_ref.shape[1]), F32)
    for kk in range(SSD_CONV):
        shift = (SSD_CONV // 2 - kk) % n_ext
        tap = ext if shift == 0 else pltpu.roll(ext, shift, 0)
        acc = acc + w_ref[kk:kk + 1, :] * tap[halo:halo + lt, :]
    xc_ref[...] = _silu(acc)


def _conv(xbc, conv_w, conv_b, n_p, seq_p, seq_s):
    total, c = xbc.shape
    lt = math.gcd(seq_p, 256)
    halo = 8
    nblk = total // halo
    per = lt // halo
    return pl.pallas_call(
        functools.partial(_conv_kernel, lt=lt, n_p=n_p, seq_p=seq_p, seq_s=seq_s),
        grid=(total // lt,),
        in_specs=[pl.BlockSpec((lt, c), lambda i: (i, 0)),
                  pl.BlockSpec((halo, c), lambda i: (jnp.maximum(i * per - 1, 0), 0)),
                  pl.BlockSpec((halo, c), lambda i: (jnp.minimum((i + 1) * per, nblk - 1), 0)),
                  _resident(conv_w.shape), _resident((1, c))],
        out_specs=pl.BlockSpec((lt, c), lambda i: (i, 0)),
        out_shape=jax.ShapeDtypeStruct((total, c), F32),
        compiler_params=_params("parallel"),
        name="ssd_conv",
    )(xbc, xbc, xbc, conv_w, conv_b.reshape(1, c))


def _ssd_kernel(xs_ref, b_ref, c_ref, dt_ref, dtt_ref, alr_ref, alc_ref, dbr_ref, dbc_ref, dsk_ref, h0_ref,
                y_ref, fin_ref, st_ref, *, n_chunks, n_p, seq_p, seq_s):
    q = SSD_Q
    d = pl.program_id(0)
    ci = pl.program_id(1)
    blocks = D_INNER // LANES
    n_steps = n_chunks // SSD_STEP_CHUNKS
    tok = jnp.where(d == 0, ci, n_steps - 1 - ci) * (q * SSD_STEP_CHUNKS)
    in_p = tok < n_p
    pos = jnp.where(in_p, tok % seq_p, (tok - n_p) % seq_s)
    first = pos == 0
    last = pos + q * SSD_STEP_CHUNKS == jnp.where(in_p, seq_p, seq_s)
    start = jnp.where(d == 0, first, last)
    end = jnp.where(d == 0, last, first)

    @pl.when(jnp.logical_and(start, in_p))
    def _():
        st_ref[...] = jnp.zeros(st_ref.shape, F32)

    @pl.when(jnp.logical_and(start, jnp.logical_not(in_p)))
    def _():
        h0 = h0_ref[...].reshape(D_INNER, SSD_N)
        for kb in range(blocks):
            st_ref[:, LANES * kb:LANES * (kb + 1)] = h0[LANES * kb:LANES * (kb + 1), :].T

    skip_on = jnp.where(d == 0, 1.0, 0.0)
    sgn = 1 - 2 * d
    li = lax.broadcasted_iota(jnp.int32, (q, q), 0)
    si = lax.broadcasted_iota(jnp.int32, (q, q), 1)
    causal = (li - si) * sgn >= 0
    tri_l = jnp.where(causal, 1.0, 0.0)
    tri_r = jnp.where((si - li) * sgn >= 0, 1.0, 0.0)
    a_row = -jnp.exp(alr_ref[...])
    a_col = -jnp.exp(alc_ref[...])
    first_head = lax.broadcasted_iota(jnp.int32, (q, LANES), 1) < SSD_P
    first_head2 = lax.broadcasted_iota(jnp.int32, (q + SSD_N, LANES), 1) < SSD_P
    for k in range(SSD_STEP_CHUNKS):
        off = pl.multiple_of(jnp.where(d == 0, k, SSD_STEP_CHUNKS - 1 - k) * q, q)
        rows = pl.ds(off, q)
        _ssd_chunk(xs_ref, b_ref, c_ref, dt_ref, dtt_ref, dbr_ref, dbc_ref, dsk_ref, y_ref, st_ref, rows, d,
                   a_row, a_col, causal, tri_l, tri_r, skip_on, first_head, first_head2)

    @pl.when(jnp.logical_and(end, in_p))
    def _():
        for kb in range(blocks):
            t = st_ref[:, LANES * kb:LANES * (kb + 1)].T
            fin_ref[2 * kb:2 * kb + 2] = t.reshape(2, SSD_P, SSD_N)


def _ssd_chunk(xs_ref, b_ref, c_ref, dt_ref, dtt_ref, dbr_ref, dbc_ref, dsk_ref, y_ref, st_ref, rows, d,
               a_row, a_col, causal, tri_l, tri_r, skip_on, first_head, first_head2):
    q = SSD_Q
    heads_per_group = SSD_H // SSD_G
    gw = heads_per_group * SSD_P
    dt_both = dt_ref[rows, :]
    dt_col = _softplus(jnp.where(d == 0, dt_both[:, 0:SSD_H], dt_both[:, SSD_H:2 * SSD_H]) + dbr_ref[...])
    dt_row = _softplus(dtt_ref[:, rows] + dbc_ref[...])
    cs_col = _dot_hi(tri_l, dt_col * a_row)
    da_row = dt_row * a_col
    cs_row = _dot_hi(da_row, tri_r)
    tot = jnp.sum(da_row, axis=-1, keepdims=True)
    w_row = dt_row * jnp.exp(tot - cs_row)
    etot = jnp.exp(tot)
    src_row = cs_row - jnp.log(dt_row)
    w_row_bf = w_row.astype(BF)
    for g in range(SSD_G):
        bg = b_ref[rows, SSD_N * g:SSD_N * (g + 1)]
        cg = c_ref[rows, SSD_N * g:SSD_N * (g + 1)].astype(BF)
        cb = _dot_nt(cg, bg.astype(BF)).astype(BF)
        bgt = bg.T.astype(BF)
        c_state = _dot(cg, st_ref[:, gw * g:gw * (g + 1)].astype(BF))
        for jp in range(heads_per_group // 2):
            pi = g * (heads_per_group // 2) + jp
            cols = slice(LANES * pi, LANES * (pi + 1))
            xp = xs_ref[rows, cols]
            xb = xp.astype(BF)
            res, grow = [], []
            for hh in range(2):
                h = 2 * pi + hh
                csc = jnp.broadcast_to(cs_col[:, h:h + 1], (q, q))
                grow.append(jnp.exp(csc))
                m = jnp.exp(jnp.where(causal, csc - src_row[h:h + 1, :], -jnp.inf)).astype(BF) * cb
                bw = bgt * w_row_bf[h:h + 1, :]
                res.append(_dot(jnp.concatenate([m, bw], axis=0), xb))
            both = jnp.where(first_head2, res[0], res[1])
            keep = jnp.where(first_head, etot[2 * pi:2 * pi + 1, :], etot[2 * pi + 1:2 * pi + 2, :])
            st_ref[:, cols] = keep * st_ref[:, cols] + both[q:q + SSD_N, :]
            y_off = jnp.where(first_head, grow[0], grow[1]) * c_state[:, LANES * jp:LANES * (jp + 1)]
            y_ref[rows, cols] = (both[0:q, :] + y_off + (skip_on * dsk_ref[:, cols]) * xp).astype(y_ref.dtype)


def _ssd(xc, dt_raw, dtt_raw, dt_bias, a_log, d_skip, h0, n_p, seq_p, seq_s):
    q = SSD_Q * SSD_STEP_CHUNKS
    assert seq_p % q == 0 and seq_s % q == 0
    total = xc.shape[0]
    n_steps = total // q
    n_chunks = total // SSD_Q
    n_prompt = n_p // seq_p
    n_sample = h0.shape[0]

    def blk(d, c):
        return jnp.where(d == 0, c, n_steps - 1 - c)

    def h0_idx(d, c):
        return (jnp.clip((blk(d, c) * q - n_p) // seq_s, 0, n_sample - 1), d, 0, 0, 0)

    def fin_idx(d, c):
        return (jnp.minimum(blk(d, c) * q // seq_p, n_prompt - 1), d, 0, 0, 0)

    state_block = (None, None, SSD_H, SSD_P, SSD_N)
    return pl.pallas_call(
        functools.partial(_ssd_kernel, n_chunks=n_chunks, n_p=n_p, seq_p=seq_p, seq_s=seq_s),
        grid=(2, n_steps),
        in_specs=[pl.BlockSpec((q, D_INNER), lambda d, c: (blk(d, c), 0)),
                  pl.BlockSpec((q, SSD_GN), lambda d, c: (blk(d, c), D_INNER // SSD_GN)),
                  pl.BlockSpec((q, SSD_GN), lambda d, c: (blk(d, c), D_INNER // SSD_GN + 1)),
                  pl.BlockSpec((q, 2 * SSD_H), lambda d, c: (blk(d, c), 0)),
                  pl.BlockSpec((SSD_H, q), lambda d, c: (d, blk(d, c))),
                  pl.BlockSpec((None, 1, SSD_H), lambda d, c: (d, 0, 0)),
                  pl.BlockSpec((None, SSD_H, 1), lambda d, c: (d, 0, 0)),
                  pl.BlockSpec((None, 1, SSD_H), lambda d, c: (d, 0, 0)),
                  pl.BlockSpec((None, SSD_H, 1), lambda d, c: (d, 0, 0)),
                  _resident((1, D_INNER)),
                  pl.BlockSpec(state_block, h0_idx)],
        out_specs=[pl.BlockSpec((None, q, D_INNER), lambda d, c: (d, blk(d, c), 0)),
                   pl.BlockSpec(state_block, fin_idx)],
        out_shape=[jax.ShapeDtypeStruct((2, total, D_INNER), BF),
                   jax.ShapeDtypeStruct((n_prompt, 2, SSD_H, SSD_P, SSD_N), F32)],
        scratch_shapes=[pltpu.VMEM((SSD_N, D_INNER), F32)],
        compiler_params=_params("parallel", "arbitrary"),
        name="ssd_scan",
    )(xc, xc, xc, dt_raw, dtt_raw, a_log.reshape(2, 1, SSD_H), a_log.reshape(2, SSD_H, 1),
      dt_bias.reshape(2, 1, SSD_H), dt_bias.reshape(2, SSD_H, 1), d_skip, h0)


def _odd_out_kernel(x_ref, mod_ref, yf_ref, yb_ref, z_ref, ng_ref, wout_ref,
                    gf_ref, wg_ref, wu_ref, wd_ref, fg_ref, *o_refs, p_steps, final):
    y = (yf_ref[...].astype(F32) + yb_ref[...].astype(F32)) * _silu(z_ref[...])
    inv = lax.rsqrt(jnp.mean(y * y, axis=-1, keepdims=True) + EPS)
    x1 = x_ref[...] + (mod_ref[2:3, :] * inv) * _dot((y * ng_ref[...]).astype(BF), wout_ref[...])
    x2 = _ffn(x1, mod_ref, gf_ref, wg_ref, wu_ref, wd_ref)
    if not final:
        o_refs[0][...] = x2
        return
    out = _rms(x2, fg_ref[...])
    is_p = pl.program_id(0) < p_steps

    @pl.when(is_p)
    def _():
        o_refs[0][...] = out

    @pl.when(jnp.logical_not(is_p))
    def _():
        o_refs[1][...] = out


def _odd_out(st, x, mods, y, z, norm_g, w_out, g_ffn, wg, wu, wd, layer, final_g, final):
    ydir = lambda d: pl.BlockSpec((None, st.tm, D_INNER), lambda i: (d, i, 0))
    return pl.pallas_call(
        functools.partial(_odd_out_kernel, p_steps=st.p_steps, final=final),
        grid=(st.steps,),
        in_specs=[st.row_spec(D_MODEL), st.mod_spec(), ydir(0), ydir(1), st.row_spec(D_INNER),
                  _resident((1, D_INNER)), _resident(w_out.shape),
                  _resident((1, D_MODEL)), _layer_resident(wg.shape, layer), _layer_resident(wu.shape, layer),
                  _layer_resident(wd.shape, layer), _resident((1, D_MODEL))],
        out_specs=[st.prompt_spec(D_MODEL), st.sample_spec(D_MODEL)] if final else st.row_spec(D_MODEL),
        out_shape=([jax.ShapeDtypeStruct((st.n_p, D_MODEL), F32),
                    jax.ShapeDtypeStruct((st.total - st.n_p, D_MODEL), F32)] if final
                   else jax.ShapeDtypeStruct((st.total, D_MODEL), F32)),
        compiler_params=_params("arbitrary"),
        name="odd_out_ffn",
    )(x, mods, y, y, z, norm_g, w_out, g_ffn, wg, wu, wd, final_g)


def kernel(x_prompt, x_sample, cache_na_k, cache_na_v, state_s5_re, state_s5_im, state_ssd, c, c_ctx, norm_mix_g, norm_ffn_g, ada_w, ada_b, ffn_w_gate, ffn_w_up, ffn_w_down, ev_w_in, ev_w_out, s5_a_re, s5_a_im, s5_log_dt, s5_b_re, s5_b_im, s5_c_re, s5_c_im, s5_d, s5_w_glu, na_rpb, od_w_in, od_conv_w, od_conv_b, ssd_a_log, ssd_dt_bias, ssd_d, ssd_norm_g, od_w_out, final_norm_g):
    bp, seq_p, d = x_prompt.shape
    bs, seq_s, _ = x_sample.shape
    depth = ada_w.shape[0]
    n_p = bp * seq_p
    assert d == D_MODEL and n_p % seq_s == 0
    st = _Stream(n_p, seq_s, bs, tm=math.gcd(512, math.gcd(n_p, seq_s)))

    xp = x_prompt.reshape(n_p, d)
    xs = x_sample.reshape(bs * seq_s, d)
    cond = jnp.concatenate([c_ctx[None, :], c, jnp.zeros((8 - 1 - bs, d), F32)], axis=0)
    mods = _ada(cond, ada_w, ada_b).reshape(depth, 8, ADA_CHUNKS, d)
    row = lambda t: t.reshape(1, -1)

    new_k, new_v, new_s5_re, new_s5_im, new_ssd = [], [], [], [], []
    wg, wu, wd = ffn_w_gate.astype(BF), ffn_w_up.astype(BF), ffn_w_down.astype(BF)
    for layer in range(depth):
        g_mix = row(norm_mix_g[layer])
        g_ffn = row(norm_ffn_g[layer])
        if layer % 2 == 0:
            e = layer // 2
            if layer > 0:
                xp, xs = x[:n_p], x[n_p:]
            u, q, k, v = _even_in(st, xp, xs, mods[layer], g_mix, ev_w_in[e].astype(BF))
            attn_p = _ctx_attn(q, k, v, bp, seq_p)
            heads = lambda t: t[:n_p].reshape(bp, seq_p, NA_HEADS, NA_HD).transpose(0, 2, 1, 3)
            new_k.append(heads(k))
            new_v.append(heads(v))
            ctx = lambda t: t[:, e].transpose(0, 2, 1, 3).reshape(bs, -1, D_NA).astype(BF)
            attn_s = _na_attn(q, k, v, ctx(cache_na_k), ctx(cache_na_v), _bias_table(na_rpb[e]),
                              bs, seq_s, n_p // seq_s)
            v_op, bw, cw, sc = _s5_prep(s5_a_re[e], s5_a_im[e], s5_log_dt[e], s5_b_re[e], s5_b_im[e],
                                        s5_c_re[e], s5_c_im[e])
            y, fin = _s5(u, v_op, bw, cw, sc, _state_planes(state_s5_re[:, e], state_s5_im[:, e]),
                         n_p, seq_p, seq_s)
            fre, fim = _planes_state(fin)
            new_s5_re.append(fre)
            new_s5_im.append(fim)
            x = _even_out(st, xp, xs, mods[layer], y, u, attn_p, attn_s, row(s5_d[e]),
                          s5_w_glu[e].astype(BF), ev_w_out[e].astype(BF), g_ffn, wg, wu, wd, layer)
        else:
            o = layer // 2
            z, xbc, dt_raw, dtt_raw = _odd_in(st, x, mods[layer], g_mix, od_w_in[o].astype(BF))
            xc = _conv(xbc, od_conv_w[o], od_conv_b[o], n_p, seq_p, seq_s)
            y, fin = _ssd(xc, dt_raw, dtt_raw, ssd_dt_bias[o], ssd_a_log[o], row(jnp.repeat(ssd_d[o], SSD_P)),
                          state_ssd[:, o], n_p, seq_p, seq_s)
            new_ssd.append(fin)
            x = _odd_out(st, x, mods[layer], y, z, row(ssd_norm_g[o]), od_w_out[o].astype(BF),
                         g_ffn, wg, wu, wd, layer, row(final_norm_g), layer == depth - 1)
    if depth % 2 == 1:
        raise NotImplementedError("final norm is fused into the last (odd) layer")
    y_prompt = x[0].reshape(bp, seq_p, d)
    y_sample = x[1].reshape(bs, seq_s, d)
    return (y_prompt, y_sample, jnp.stack(new_k, axis=1), jnp.stack(new_v, axis=1),
            jnp.stack(new_s5_re, axis=1), jnp.stack(new_s5_im, axis=1), jnp.stack(new_ssd, axis=1))
```
